```python
import math
import jax
import jax.numpy as jnp
from jax import lax
import numpy as np

D_MODEL = 2048
BATCH = 32
SEQ = 256
DEPTH = 2
DEC_BATCH = 4
DEC_SEQ = 1024
PAST_LEN = 256

GRID_W = 64
EPS = 1e-6
ROPE_THETA = 10000.0
Q_BLOCK = 128
CHUNK = 64
N_MIXERS = 4
GROUP_W = D_MODEL // N_MIXERS
D_MIX = N_MIXERS * GROUP_W
HA = 4
DKA = GROUP_W // HA
DVA = GROUP_W // HA
LB_FLOOR = 1e-30
HB = 4
Q_RANK = D_MODEL // 4
KV_RANK = D_MODEL // 8
NOPE = 128
ROPE = 64
VB = GROUP_W // HB
MLA_SCALE = (NOPE + ROPE) ** -0.5
HC = 8
PC = GROUP_W // HC
NC = 128
GC = 2
CONV_W = 3
CONV_CH = GROUP_W + 2 * GC * NC
HD = 8
DHD = GROUP_W // HD
WIN_R = 8
WIN_C = 16
N_KEYS = 128
N_EXPERTS = N_KEYS * N_KEYS
PEER_HEADS = 8
PEER_QDIM = 256
PEER_TOPK = 16
PEER_TOKEN_BLOCK = 128

IN_SPLITS = (HA * DKA, HA * DKA, HA * DKA, HA * DVA, HA * DVA,
             Q_RANK, KV_RANK, ROPE,
             GROUP_W, GROUP_W, GC * NC, GC * NC, HC, HC,
             GROUP_W, GROUP_W, GROUP_W)
IN_W = sum(IN_SPLITS)
F32 = jnp.float32

kernel_name = 'hybrid_flow_hgrn2_mla_ssd_natten_peer'


def rmsnorm(x, w):
    xf = x.astype(F32)
    y = xf * lax.rsqrt(jnp.mean(xf * xf, axis=-1, keepdims=True) + EPS)
    return (y * w.astype(F32)).astype(x.dtype)


def split_cols(a, sizes):
    offs = []
    acc = 0
    for s in sizes[:-1]:
        acc += s
        offs.append(acc)
    return jnp.split(a, offs, axis=-1)


def flip_t(a):
    return jnp.flip(a, axis=1)


def conv_centred(x, w, b):
    ch = x.shape[-1]
    k = w.shape[0]
    y = lax.conv_general_dilated(x, w[:, None, :].astype(x.dtype), window_strides=(1,),
                                 padding=[(k // 2, k // 2)], dimension_numbers=('NWC', 'WIO', 'NWC'),
                                 feature_group_count=ch)
    return y + b.astype(x.dtype)


def rope_2d(x):
    t_len = x.shape[1]
    t = jnp.arange(t_len)
    half = ROPE // 2
    inv = 1.0 / (ROPE_THETA ** (jnp.arange(0, half, 2, dtype=F32) / half))
    rows = (t // GRID_W).astype(F32)
    cols = (t % GRID_W).astype(F32)

    def rot(xa, pos):
        ang = (pos[:, None] * inv[None, :]).reshape((t_len,) + (1,) * (x.ndim - 3) + (inv.shape[0],))
        cs, sn = jnp.cos(ang).astype(x.dtype), jnp.sin(ang).astype(x.dtype)
        x1, x2 = jnp.split(xa, 2, axis=-1)
        return jnp.concatenate([x1 * cs - x2 * sn, x2 * cs + x1 * sn], axis=-1)
    return jnp.concatenate([rot(x[..., :half], rows), rot(x[..., half:], cols)], axis=-1)


def block_attention(q, k, v, scale):
    b, tq, h, dq = q.shape
    nb = tq // Q_BLOCK
    qb = q.reshape(b, nb, Q_BLOCK, h, dq).transpose(1, 0, 2, 3, 4)

    def one(qi):
        s = jnp.einsum('bqhd,bkhd->bhqk', qi, k).astype(F32) * scale
        pr = jax.nn.softmax(s, axis=-1).astype(v.dtype)
        return jnp.einsum('bhqk,bkhd->bqhd', pr, v)
    o = lax.map(one, qb)
    return o.transpose(1, 0, 2, 3, 4).reshape(b, tq, h, v.shape[-1])


def masked_exp(mask, diff):
    return jnp.where(mask, jnp.exp(jnp.where(mask, diff, 0.0)), 0.0)


def gla_chunk_scan(q, k, v, logf, s0):
    b, t, h, dk = q.shape
    dv = v.shape[-1]
    n = t // CHUNK

    def chunks(a):
        return a.reshape(b, n, CHUNK, h, a.shape[-1]).transpose(1, 0, 3, 2, 4)
    cum = jnp.cumsum(chunks(logf), axis=3)
    mask = jnp.tril(jnp.ones((CHUNK, CHUNK), dtype=bool))[:, :, None]

    def step(state, inp):
        qi, ki, vi, ci = inp
        decay = masked_exp(mask, ci[:, :, :, None, :] - ci[:, :, None, :, :])
        att = jnp.einsum('bhtk,bhsk,bhtsk->bhts', qi, ki, decay)
        o = jnp.einsum('bhts,bhsv->bhtv', att, vi)
        o = o + jnp.einsum('bhtk,bhkv->bhtv', qi * jnp.exp(ci), state)
        last = ci[:, :, -1]
        state = jnp.exp(last)[..., None] * state + jnp.einsum(
            'bhsk,bhsv->bhkv', ki * jnp.exp(last[:, :, None, :] - ci), vi)
        return state, o
    s_fin, o = lax.scan(step, s0, (chunks(q), chunks(k), chunks(v), cum))
    return o.transpose(1, 0, 3, 2, 4).reshape(b, t, h, dv), s_fin


def ssd_chunk_scan(x, dt, a, bm, cm, s0):
    b, t, h, p = x.shape
    g, n_st = bm.shape[2], bm.shape[3]
    r = h // g
    n = t // CHUNK
    xdt = (x * dt[..., None]).reshape(b, n, CHUNK, g, r, p).transpose(1, 0, 2, 3, 4, 5)
    la = (dt * a).reshape(b, n, CHUNK, g, r).transpose(1, 0, 2, 3, 4)
    bc = bm.reshape(b, n, CHUNK, g, n_st).transpose(1, 0, 2, 3, 4)
    cc = cm.reshape(b, n, CHUNK, g, n_st).transpose(1, 0, 2, 3, 4)
    mask = jnp.tril(jnp.ones((CHUNK, CHUNK), dtype=bool))[None, :, :, None, None]

    def step(state, inp):
        xi, ai, bi, ci = inp
        cum = jnp.cumsum(ai, axis=1)
        lmat = masked_exp(mask, cum[:, :, None] - cum[:, None, :])
        cb = jnp.einsum('btgn,bsgn->btsg', ci, bi)
        y = jnp.einsum('btsg,btsgr,bsgrp->btgrp', cb, lmat, xi)
        y = y + jnp.einsum('btgn,bgrpn->btgrp', ci, state) * jnp.exp(cum)[..., None]
        last = cum[:, -1]
        w = jnp.exp(last[:, None] - cum)
        state = jnp.exp(last)[..., None, None] * state + jnp.einsum('bsgn,bsgr,bsgrp->bgrpn', bi, w, xi)
        return state, y
    s_fin, y = lax.scan(step, s0.reshape(b, g, r, p, n_st), (xdt, la, bc, cc))
    return y.transpose(1, 0, 2, 3, 4, 5).reshape(b, t, h, p), s_fin.reshape(b, h, p, n_st)


def hgrn2_mixer(parts, lb, norm_w, init_state):
    q, f_fwd, f_bwd, i_in, g_out = parts
    b, t, _ = q.shape
    q = jax.nn.silu(q.astype(F32)).reshape(b, t, HA, DKA) * DKA ** -0.5
    v = i_in.astype(F32).reshape(b, t, HA, DVA)
    lb = lb.astype(F32).reshape(2, HA, DKA)
    outs, finals = [], []
    for d, f_raw in enumerate((f_fwd, f_bwd)):
        fx = f_raw.astype(F32).reshape(b, t, HA, DKA)
        logf = jnp.logaddexp(jnp.log(jnp.maximum(lb[d], LB_FLOOR)), jnp.log1p(-lb[d]) + jax.nn.log_sigmoid(fx))
        k = (1.0 - lb[d]) * jax.nn.sigmoid(-fx)
        if init_state is None:
            s0 = jnp.zeros((b, HA, DKA, DVA), F32)
        else:
            s0 = init_state[:, d].astype(F32)
        if d == 0:
            o, sf = gla_chunk_scan(q, k, v, logf, s0)
        else:
            o, sf = gla_chunk_scan(flip_t(q), flip_t(k), flip_t(v), flip_t(logf), s0)
            o = flip_t(o)
        outs.append(o)
        finals.append(sf)
    o = rmsnorm(outs[0] + outs[1], norm_w.reshape(HA, DVA)) * jax.nn.silu(
        g_out.astype(F32).reshape(b, t, HA, DVA))
    return o.reshape(b, t, HA * DVA).astype(g_out.dtype), jnp.stack(finals, axis=1)


def ssd_mixer(parts, conv_w, conv_b, dt_bias, a_log, d_skip, norm_w, init_state):
    x, z, b_in, c_in, dt_fwd, dt_bwd = parts
    b, t, _ = x.shape
    xbc = jax.nn.silu(conv_centred(jnp.concatenate([x, b_in, c_in], axis=-1), conv_w, conv_b))
    xs, bs, cs = split_cols(xbc, (GROUP_W, GC * NC, GC * NC))
    xh = xs.astype(F32).reshape(b, t, HC, PC)
    bg = bs.astype(F32).reshape(b, t, GC, NC)
    cg = cs.astype(F32).reshape(b, t, GC, NC)
    a = -jnp.exp(a_log.astype(F32))
    y = d_skip.astype(F32)[:, None] * xh
    finals = []
    for d, dt_raw in enumerate((dt_fwd, dt_bwd)):
        dt = jax.nn.softplus(dt_raw.astype(F32) + dt_bias[d].astype(F32))
        if init_state is None:
            s0 = jnp.zeros((b, HC, PC, NC), F32)
        else:
            s0 = init_state[:, d].astype(F32)
        if d == 0:
            yd, sf = ssd_chunk_scan(xh, dt, a[d], bg, cg, s0)
        else:
            yd, sf = ssd_chunk_scan(flip_t(xh), flip_t(dt), a[d], flip_t(bg), flip_t(cg), s0)
            yd = flip_t(yd)
        y = y + yd
        finals.append(sf)
    y = y.reshape(b, t, GROUP_W) * jax.nn.silu(z.astype(F32))
    return rmsnorm(y, norm_w).astype(x.dtype), jnp.stack(finals, axis=1)


def mla_project(parts, p):
    cq, ckv, krope = parts
    b, t, _ = cq.shape
    q = (rmsnorm(cq, p['mla_q_norm_w']) @ p['mla_w_q_up']).reshape(b, t, HB, NOPE + ROPE)
    return q[..., :NOPE], q[..., NOPE:], rmsnorm(ckv, p['mla_kv_norm_w']), krope


def mla_keys_values(ckv, krope, w_kv_up):
    b, t, _ = ckv.shape
    kv = (ckv @ w_kv_up).reshape(b, t, HB, NOPE + VB)
    k = jnp.concatenate([kv[..., :NOPE], jnp.broadcast_to(krope[:, :, None, :], (b, t, HB, ROPE))], axis=-1)
    return k, kv[..., NOPE:]


def mla_context(parts, p):
    q_nope, q_rope, ckv, krope = mla_project(parts, p)
    k, v = mla_keys_values(ckv, krope, p['mla_w_kv_up'])
    o = block_attention(jnp.concatenate([q_nope, q_rope], axis=-1), k, v, MLA_SCALE)
    b, t = o.shape[:2]
    return o.reshape(b, t, HB * VB), ckv, krope


def mla_latent(parts, p, ckv_ctx, krope_ctx):
    q_nope, q_rope, ckv, krope = mla_project(parts, p)
    q = jnp.concatenate([q_nope, rope_2d(q_rope)], axis=-1)
    k_lat, v_lat = mla_keys_values(ckv, rope_2d(krope), p['mla_w_kv_up'])
    k_ctx, v_ctx = mla_keys_values(ckv_ctx.astype(ckv.dtype), krope_ctx.astype(krope.dtype), p['mla_w_kv_up'])
    o = block_attention(q, jnp.concatenate([k_lat, k_ctx], axis=1), jnp.concatenate([v_lat, v_ctx], axis=1),
                        MLA_SCALE)
    b, t = o.shape[:2]
    return o.reshape(b, t, HB * VB)


def natten_context(parts):
    b, t, _ = parts[0].shape
    q, k, v = [a.reshape(b, t, HD, DHD) for a in parts]
    o = block_attention(q, k, v, DHD ** -0.5)
    return o.reshape(b, t, GROUP_W), k, v


def natten_latent(parts, rpb, k_ctx, v_ctx):
    b, t, _ = parts[0].shape
    q, k, v = [a.reshape(b, t, HD, DHD) for a in parts]
    rows = t // GRID_W
    wr = min(WIN_R, rows)
    qg = q.reshape(b, rows, GRID_W, HD, DHD)
    kg = k.reshape(b, rows, GRID_W, HD, DHD)
    vg = v.reshape(b, rows, GRID_W, HD, DHD)
    k_ctx = k_ctx.astype(k.dtype)
    v_ctx = v_ctx.astype(v.dtype)
    cols = jnp.arange(GRID_W)
    cidx = jnp.clip(cols - WIN_C // 2, 0, GRID_W - WIN_C)[:, None] + jnp.arange(WIN_C)[None, :]
    c_off = cidx - cols[:, None] + (WIN_C - 1)
    rpb_c = rpb[:, :, c_off].astype(F32)
    scale = DHD ** -0.5
    n_loc = wr * WIN_C

    def one_row(r):
        rs = jnp.clip(r - wr // 2, 0, rows - wr)
        kn = lax.dynamic_slice_in_dim(kg, rs, wr, axis=1)[:, :, cidx]
        vn = lax.dynamic_slice_in_dim(vg, rs, wr, axis=1)[:, :, cidx]
        qr = lax.dynamic_index_in_dim(qg, r, axis=1, keepdims=False)
        r_off = rs + jnp.arange(wr) - r + (WIN_R - 1)
        bias = jnp.take(rpb_c, r_off, axis=1).transpose(0, 2, 1, 3)
        s_loc = jnp.einsum('bqhd,bwqjhd->bhqwj', qr, kn).astype(F32) * scale + bias[None]
        s_ctx = jnp.einsum('bqhd,bkhd->bhqk', qr, k_ctx).astype(F32) * scale
        s = jnp.concatenate([s_loc.reshape(b, HD, GRID_W, n_loc), s_ctx], axis=-1)
        pr = jax.nn.softmax(s, axis=-1).astype(v.dtype)
        p_loc = pr[..., :n_loc].reshape(b, HD, GRID_W, wr, WIN_C)
        o = jnp.einsum('bhqwj,bwqjhd->bqhd', p_loc, vn)
        return o + jnp.einsum('bhqk,bkhd->bqhd', pr[..., n_loc:], v_ctx)
    o = lax.map(one_row, jnp.arange(rows))
    return o.transpose(1, 0, 2, 3, 4).reshape(b, t, GROUP_W)


def peer_ffn(h, w_q, sub_keys, u_tab, v_tab):
    b, t, d = h.shape
    n = b * t
    hf = h.reshape(n, d)
    q = (hf @ w_q).reshape(n, PEER_HEADS, 2, PEER_QDIM // 2)
    s = jnp.einsum('nhpd,pkd->nhpk', q, sub_keys).astype(F32)
    s1, i1 = lax.top_k(s[:, :, 0], PEER_TOPK)
    s2, i2 = lax.top_k(s[:, :, 1], PEER_TOPK)
    cand_s = (s1[..., :, None] + s2[..., None, :]).reshape(n, PEER_HEADS, PEER_TOPK * PEER_TOPK)
    cand_i = (i1[..., :, None] * N_KEYS + i2[..., None, :]).reshape(n, PEER_HEADS, PEER_TOPK * PEER_TOPK)
    top_s, pos = lax.top_k(cand_s, PEER_TOPK)
    idx = jnp.take_along_axis(cand_i, pos, axis=-1)
    gate = jax.nn.softmax(top_s, axis=-1)
    nb = n // PEER_TOKEN_BLOCK

    def block(inp):
        xb, ib, gb = inp
        act = jax.nn.gelu(jnp.einsum('td,thkd->thk', xb, u_tab[ib]).astype(F32)) * gb
        return jnp.einsum('thk,thkd->td', act.astype(xb.dtype), v_tab[ib])
    out = lax.map(block, (hf.reshape(nb, PEER_TOKEN_BLOCK, d),
                          idx.reshape(nb, PEER_TOKEN_BLOCK, PEER_HEADS, PEER_TOPK),
                          gate.reshape(nb, PEER_TOKEN_BLOCK, PEER_HEADS, PEER_TOPK)))
    return out.reshape(b, t, d)


def trunk_layer(x, cond, p, cache):
    mod = jax.nn.silu(cond) @ p['w_ada'] + p['b_ada']
    if cond.ndim == 2:
        mod = mod[:, None, :]
    sh1, sc1, g1, sh2, sc2, g2 = jnp.split(mod, 6, axis=-1)
    h = rmsnorm(x, p['norm1_w']) * (1.0 + sc1) + sh1
    parts = split_cols(h @ p['w_in'], IN_SPLITS)
    pa, pb, pc, pd = parts[0:5], parts[5:8], parts[8:14], parts[14:17]
    latent = cache is not None
    o_a, st_a = hgrn2_mixer(pa, p['lb'], p['hgrn_norm_w'], cache[0] if latent else None)
    o_c, st_c = ssd_mixer(pc, p['ssd_conv_w'], p['ssd_conv_b'], p['ssd_dt_bias'], p['ssd_a_log'], p['ssd_d'],
                          p['ssd_norm_w'], cache[1] if latent else None)
    if latent:
        o_b = mla_latent(pb, p, cache[2], cache[3])
        o_d = natten_latent(pd, p['na_rpb'], cache[4], cache[5])
        new_state = None
    else:
        o_b, ckv, krope = mla_context(pb, p)
        o_d, k_na, v_na = natten_context(pd)
        new_state = (st_a, st_c, ckv, krope, k_na, v_na)
    mix = jnp.concatenate([o_a, o_b, o_c, o_d], axis=-1) @ p['w_out']
    x = x + g1 * mix
    h2 = rmsnorm(x, p['norm2_w']) * (1.0 + sc2) + sh2
    x = x + g2 * peer_ffn(h2, p['peer_w_q'], p['peer_sub_keys'], p['peer_u'], p['peer_v'])
    return x, new_state


def setup_inputs(seed: int = 0) -> dict:
    key = jax.random.key(seed)
    it = iter(jax.random.split(key, 48))

    def nrm(shape, scale):
        return jax.random.normal(next(it), shape, F32) * scale
    inp = {}
    inp['x_prompt'] = nrm((BATCH, SEQ, D_MODEL), 1.0)
    inp['x_sample'] = nrm((DEC_BATCH, DEC_SEQ, D_MODEL), 1.0)
    inp['c'] = nrm((DEC_BATCH, D_MODEL), 1.0)
    inp['state_hgrn'] = nrm((DEC_BATCH, DEPTH, 2, HA, DKA, DVA), 0.5)
    inp['state_ssd'] = nrm((DEC_BATCH, DEPTH, 2, HC, PC, NC), 0.5)
    inp['cache_mla_ckv'] = nrm((DEC_BATCH, DEPTH, PAST_LEN, KV_RANK), 1.0)
    inp['cache_mla_krope'] = nrm((DEC_BATCH, DEPTH, PAST_LEN, ROPE), 1.0)
    inp['cache_na_k'] = nrm((DEC_BATCH, DEPTH, PAST_LEN, HD, DHD), 1.0)
    inp['cache_na_v'] = nrm((DEC_BATCH, DEPTH, PAST_LEN, HD, DHD), 1.0)
    inp['c_ctx'] = nrm((D_MODEL,), 1.0)
    inp['w_ada'] = nrm((DEPTH, D_MODEL, 6 * D_MODEL), 0.5 * D_MODEL ** -0.5)
    inp['b_ada'] = nrm((DEPTH, 6 * D_MODEL), 0.01)
    inp['norm1_w'] = 1.0 + nrm((DEPTH, D_MODEL), 0.02)
    inp['norm2_w'] = 1.0 + nrm((DEPTH, D_MODEL), 0.02)
    inp['w_in'] = nrm((DEPTH, D_MODEL, IN_W), D_MODEL ** -0.5)
    inp['w_out'] = nrm((DEPTH, D_MIX, D_MODEL), D_MIX ** -0.5)
    inp['hgrn_lb_logits'] = nrm((DEPTH, 2, HA * DKA), 1.0)
    inp['hgrn_norm_w'] = 1.0 + nrm((DEPTH, HA * DVA), 0.02)
    inp['mla_q_norm_w'] = 1.0 + nrm((DEPTH, Q_RANK), 0.02)
    inp['mla_w_q_up'] = nrm((DEPTH, Q_RANK, HB * (NOPE + ROPE)), Q_RANK ** -0.5)
    inp['mla_kv_norm_w'] = 1.0 + nrm((DEPTH, KV_RANK), 0.02)
    inp['mla_w_kv_up'] = nrm((DEPTH, KV_RANK, HB * (NOPE + VB)), KV_RANK ** -0.5)
    inp['ssd_conv_w'] = nrm((DEPTH, CONV_W, CONV_CH), CONV_W ** -0.5)
    inp['ssd_conv_b'] = nrm((DEPTH, CONV_CH), 0.01)
    dt0 = jnp.exp(jax.random.uniform(next(it), (DEPTH, 2, HC), F32, math.log(1e-3), math.log(1e-1)))
    inp['ssd_dt_bias'] = dt0 + jnp.log(-jnp.expm1(-dt0))
    inp['ssd_a_log'] = jnp.log(jax.random.uniform(next(it), (DEPTH, 2, HC), F32, 1.0, 16.0))
    inp['ssd_d'] = 1.0 + nrm((DEPTH, HC), 0.1)
    inp['ssd_norm_w'] = 1.0 + nrm((DEPTH, GROUP_W), 0.02)
    inp['na_rpb'] = nrm((DEPTH, HD, 2 * WIN_R - 1, 2 * WIN_C - 1), 0.02)
    inp['peer_w_q'] = nrm((DEPTH, D_MODEL, PEER_HEADS * PEER_QDIM), D_MODEL ** -0.5)
    inp['peer_sub_keys'] = nrm((DEPTH, 2, N_KEYS, PEER_QDIM // 2), (PEER_QDIM // 2) ** -0.5)
    inp['peer_u'] = nrm((DEPTH, N_EXPERTS, D_MODEL), D_MODEL ** -0.5)
    inp['peer_v'] = nrm((DEPTH, N_EXPERTS, D_MODEL), 0.5)
    inp['final_norm_w'] = 1.0 + nrm((D_MODEL,), 0.02)
    return inp


def reference(x_prompt, x_sample, c, state_hgrn, state_ssd, cache_mla_ckv, cache_mla_krope, cache_na_k,
              cache_na_v, c_ctx, w_ada, b_ada, norm1_w, norm2_w, w_in, w_out, hgrn_lb_logits, hgrn_norm_w,
              mla_q_norm_w, mla_w_q_up, mla_kv_norm_w, mla_w_kv_up, ssd_conv_w, ssd_conv_b, ssd_dt_bias,
              ssd_a_log, ssd_d, ssd_norm_w, na_rpb, peer_w_q, peer_sub_keys, peer_u, peer_v, final_norm_w):
    lb_soft = jax.nn.softmax(hgrn_lb_logits.astype(F32), axis=0)
    lb_all = jnp.cumsum(lb_soft, axis=0) - lb_soft[0]
    stacked = {'w_ada': w_ada, 'b_ada': b_ada, 'norm1_w': norm1_w, 'norm2_w': norm2_w, 'w_in': w_in,
               'w_out': w_out, 'hgrn_norm_w': hgrn_norm_w, 'mla_q_norm_w': mla_q_norm_w,
               'mla_w_q_up': mla_w_q_up, 'mla_kv_norm_w': mla_kv_norm_w, 'mla_w_kv_up': mla_w_kv_up,
               'ssd_conv_w': ssd_conv_w, 'ssd_conv_b': ssd_conv_b, 'ssd_dt_bias': ssd_dt_bias,
               'ssd_a_log': ssd_a_log, 'ssd_d': ssd_d, 'ssd_norm_w': ssd_norm_w, 'na_rpb': na_rpb,
               'peer_w_q': peer_w_q, 'peer_sub_keys': peer_sub_keys, 'peer_u': peer_u, 'peer_v': peer_v}

    xp = x_prompt
    ctx_states = []
    for l in range(DEPTH):
        p = {name: arr[l] for name, arr in stacked.items()}
        p['lb'] = lb_all[l]
        xp, st = trunk_layer(xp, c_ctx, p, None)
        ctx_states.append(st)

    xs = x_sample
    for l in range(DEPTH):
        p = {name: arr[l] for name, arr in stacked.items()}
        p['lb'] = lb_all[l]
        cache_l = (state_hgrn[:, l], state_ssd[:, l], cache_mla_ckv[:, l], cache_mla_krope[:, l],
                   cache_na_k[:, l], cache_na_v[:, l])
        xs, _ = trunk_layer(xs, c, p, cache_l)

    y_prompt = rmsnorm(xp, final_norm_w)
    y_sample = rmsnorm(xs, final_norm_w)
    new_state_hgrn = jnp.stack([s[0] for s in ctx_states], axis=1)
    new_state_ssd = jnp.stack([s[1] for s in ctx_states], axis=1)
    new_cache_mla_ckv = jnp.stack([s[2] for s in ctx_states], axis=1)
    new_cache_mla_krope = jnp.stack([s[3] for s in ctx_states], axis=1)
    new_cache_na_k = jnp.stack([s[4] for s in ctx_states], axis=1)
    new_cache_na_v = jnp.stack([s[5] for s in ctx_states], axis=1)
    return (y_prompt, y_sample, new_state_hgrn, new_state_ssd, new_cache_mla_ckv, new_cache_mla_krope,
            new_cache_na_k, new_cache_na_v)
```

```python
import math
import jax
import jax.numpy as jnp
from jax import lax
import numpy as np
from jax.experimental import pallas as pl
from jax.experimental.pallas import tpu as pltpu

D_MODEL = 2048
BATCH = 32
SEQ = 256
DEPTH = 2
DEC_BATCH = 4
DEC_SEQ = 1024
PAST_LEN = 256

GRID_W = 64
EPS = 1e-6
ROPE_THETA = 10000.0
Q_BLOCK = 128
CHUNK = 64
N_MIXERS = 4
GROUP_W = D_MODEL // N_MIXERS
D_MIX = N_MIXERS * GROUP_W
HA = 4
DKA = GROUP_W // HA
DVA = GROUP_W // HA
LB_FLOOR = 1e-30
HB = 4
Q_RANK = D_MODEL // 4
KV_RANK = D_MODEL // 8
NOPE = 128
ROPE = 64
VB = GROUP_W // HB
MLA_SCALE = (NOPE + ROPE) ** -0.5
HC = 8
PC = GROUP_W // HC
NC = 128
GC = 2
CONV_W = 3
CONV_CH = GROUP_W + 2 * GC * NC
HD = 8
DHD = GROUP_W // HD
WIN_R = 8
WIN_C = 16
N_KEYS = 128
N_EXPERTS = N_KEYS * N_KEYS
PEER_HEADS = 8
PEER_QDIM = 256
PEER_TOPK = 16
PEER_TOKEN_BLOCK = 128

IN_SPLITS = (HA * DKA, HA * DKA, HA * DKA, HA * DVA, HA * DVA,
             Q_RANK, KV_RANK, ROPE,
             GROUP_W, GROUP_W, GC * NC, GC * NC, HC, HC,
             GROUP_W, GROUP_W, GROUP_W)
IN_W = sum(IN_SPLITS)
F32 = jnp.float32


def rmsnorm(x, w):
    xf = x.astype(F32)
    y = xf * lax.rsqrt(jnp.mean(xf * xf, axis=-1, keepdims=True) + EPS)
    return (y * w.astype(F32)).astype(x.dtype)


def split_cols(a, sizes):
    offs = []
    acc = 0
    for s in sizes[:-1]:
        acc += s
        offs.append(acc)
    return jnp.split(a, offs, axis=-1)


def flip_t(a):
    return jnp.flip(a, axis=1)


def conv_centred(x, w, b):
    ch = x.shape[-1]
    k = w.shape[0]
    y = lax.conv_general_dilated(x, w[:, None, :].astype(x.dtype), window_strides=(1,),
                                 padding=[(k // 2, k // 2)], dimension_numbers=('NWC', 'WIO', 'NWC'),
                                 feature_group_count=ch)
    return y + b.astype(x.dtype)


def rope_2d(x):
    t_len = x.shape[1]
    t = jnp.arange(t_len)
    half = ROPE // 2
    inv = 1.0 / (ROPE_THETA ** (jnp.arange(0, half, 2, dtype=F32) / half))
    rows = (t // GRID_W).astype(F32)
    cols = (t % GRID_W).astype(F32)

    def rot(xa, pos):
        ang = (pos[:, None] * inv[None, :]).reshape((t_len,) + (1,) * (x.ndim - 3) + (inv.shape[0],))
        cs, sn = jnp.cos(ang).astype(x.dtype), jnp.sin(ang).astype(x.dtype)
        x1, x2 = jnp.split(xa, 2, axis=-1)
        return jnp.concatenate([x1 * cs - x2 * sn, x2 * cs + x1 * sn], axis=-1)
    return jnp.concatenate([rot(x[..., :half], rows), rot(x[..., half:], cols)], axis=-1)


def block_attention(q, k, v, scale):
    b, tq, h, dq = q.shape
    nb = tq // Q_BLOCK
    qb = q.reshape(b, nb, Q_BLOCK, h, dq).transpose(1, 0, 2, 3, 4)

    def one(qi):
        s = jnp.einsum('bqhd,bkhd->bhqk', qi, k).astype(F32) * scale
        pr = jax.nn.softmax(s, axis=-1).astype(v.dtype)
        return jnp.einsum('bhqk,bkhd->bqhd', pr, v)
    o = lax.map(one, qb)
    return o.transpose(1, 0, 2, 3, 4).reshape(b, tq, h, v.shape[-1])


def masked_exp(mask, diff):
    return jnp.where(mask, jnp.exp(jnp.where(mask, diff, 0.0)), 0.0)


def gla_chunk_scan(q, k, v, logf, s0):
    b, t, h, dk = q.shape
    dv = v.shape[-1]
    n = t // CHUNK

    def chunks(a):
        return a.reshape(b, n, CHUNK, h, a.shape[-1]).transpose(1, 0, 3, 2, 4)
    cum = jnp.cumsum(chunks(logf), axis=3)
    mask = jnp.tril(jnp.ones((CHUNK, CHUNK), dtype=bool))[:, :, None]

    def step(state, inp):
        qi, ki, vi, ci = inp
        decay = masked_exp(mask, ci[:, :, :, None, :] - ci[:, :, None, :, :])
        att = jnp.einsum('bhtk,bhsk,bhtsk->bhts', qi, ki, decay)
        o = jnp.einsum('bhts,bhsv->bhtv', att, vi)
        o = o + jnp.einsum('bhtk,bhkv->bhtv', qi * jnp.exp(ci), state)
        last = ci[:, :, -1]
        state = jnp.exp(last)[..., None] * state + jnp.einsum(
            'bhsk,bhsv->bhkv', ki * jnp.exp(last[:, :, None, :] - ci), vi)
        return state, o
    s_fin, o = lax.scan(step, s0, (chunks(q), chunks(k), chunks(v), cum))
    return o.transpose(1, 0, 3, 2, 4).reshape(b, t, h, dv), s_fin


def ssd_chunk_scan(x, dt, a, bm, cm, s0):
    b, t, h, p = x.shape
    g, n_st = bm.shape[2], bm.shape[3]
    r = h // g
    n = t // CHUNK
    xdt = (x * dt[..., None]).reshape(b, n, CHUNK, g, r, p).transpose(1, 0, 2, 3, 4, 5)
    la = (dt * a).reshape(b, n, CHUNK, g, r).transpose(1, 0, 2, 3, 4)
    bc = bm.reshape(b, n, CHUNK, g, n_st).transpose(1, 0, 2, 3, 4)
    cc = cm.reshape(b, n, CHUNK, g, n_st).transpose(1, 0, 2, 3, 4)
    mask = jnp.tril(jnp.ones((CHUNK, CHUNK), dtype=bool))[None, :, :, None, None]

    def step(state, inp):
        xi, ai, bi, ci = inp
        cum = jnp.cumsum(ai, axis=1)
        lmat = masked_exp(mask, cum[:, :, None] - cum[:, None, :])
        cb = jnp.einsum('btgn,bsgn->btsg', ci, bi)
        y = jnp.einsum('btsg,btsgr,bsgrp->btgrp', cb, lmat, xi)
        y = y + jnp.einsum('btgn,bgrpn->btgrp', ci, state) * jnp.exp(cum)[..., None]
        last = cum[:, -1]
        w = jnp.exp(last[:, None] - cum)
        state = jnp.exp(last)[..., None, None] * state + jnp.einsum('bsgn,bsgr,bsgrp->bgrpn', bi, w, xi)
        return state, y
    s_fin, y = lax.scan(step, s0.reshape(b, g, r, p, n_st), (xdt, la, bc, cc))
    return y.transpose(1, 0, 2, 3, 4, 5).reshape(b, t, h, p), s_fin.reshape(b, h, p, n_st)


def hgrn2_mixer(parts, lb, norm_w, init_state):
    q, f_fwd, f_bwd, i_in, g_out = parts
    b, t, _ = q.shape
    q = jax.nn.silu(q.astype(F32)).reshape(b, t, HA, DKA) * DKA ** -0.5
    v = i_in.astype(F32).reshape(b, t, HA, DVA)
    lb = lb.astype(F32).reshape(2, HA, DKA)
    outs, finals = [], []
    for d, f_raw in enumerate((f_fwd, f_bwd)):
        fx = f_raw.astype(F32).reshape(b, t, HA, DKA)
        logf = jnp.logaddexp(jnp.log(jnp.maximum(lb[d], LB_FLOOR)), jnp.log1p(-lb[d]) + jax.nn.log_sigmoid(fx))
        k = (1.0 - lb[d]) * jax.nn.sigmoid(-fx)
        if init_state is None:
            s0 = jnp.zeros((b, HA, DKA, DVA), F32)
        else:
            s0 = init_state[:, d].astype(F32)
        if d == 0:
            o, sf = gla_chunk_scan(q, k, v, logf, s0)
        else:
            o, sf = gla_chunk_scan(flip_t(q), flip_t(k), flip_t(v), flip_t(logf), s0)
            o = flip_t(o)
        outs.append(o)
        finals.append(sf)
    o = rmsnorm(outs[0] + outs[1], norm_w.reshape(HA, DVA)) * jax.nn.silu(
        g_out.astype(F32).reshape(b, t, HA, DVA))
    return o.reshape(b, t, HA * DVA).astype(g_out.dtype), jnp.stack(finals, axis=1)


def ssd_mixer(parts, conv_w, conv_b, dt_bias, a_log, d_skip, norm_w, init_state):
    x, z, b_in, c_in, dt_fwd, dt_bwd = parts
    b, t, _ = x.shape
    xbc = jax.nn.silu(conv_centred(jnp.concatenate([x, b_in, c_in], axis=-1), conv_w, conv_b))
    xs, bs, cs = split_cols(xbc, (GROUP_W, GC * NC, GC * NC))
    xh = xs.astype(F32).reshape(b, t, HC, PC)
    bg = bs.astype(F32).reshape(b, t, GC, NC)
    cg = cs.astype(F32).reshape(b, t, GC, NC)
    a = -jnp.exp(a_log.astype(F32))
    y = d_skip.astype(F32)[:, None] * xh
    finals = []
    for d, dt_raw in enumerate((dt_fwd, dt_bwd)):
        dt = jax.nn.softplus(dt_raw.astype(F32) + dt_bias[d].astype(F32))
        if init_state is None:
            s0 = jnp.zeros((b, HC, PC, NC), F32)
        else:
            s0 = init_state[:, d].astype(F32)
        if d == 0:
            yd, sf = ssd_chunk_scan(xh, dt, a[d], bg, cg, s0)
        else:
            yd, sf = ssd_chunk_scan(flip_t(xh), flip_t(dt), a[d], flip_t(bg), flip_t(cg), s0)
            yd = flip_t(yd)
        y = y + yd
        finals.append(sf)
    y = y.reshape(b, t, GROUP_W) * jax.nn.silu(z.astype(F32))
    return rmsnorm(y, norm_w).astype(x.dtype), jnp.stack(finals, axis=1)


def mla_project(parts, p):
    cq, ckv, krope = parts
    b, t, _ = cq.shape
    q = (rmsnorm(cq, p['mla_q_norm_w']) @ p['mla_w_q_up']).reshape(b, t, HB, NOPE + ROPE)
    return q[..., :NOPE], q[..., NOPE:], rmsnorm(ckv, p['mla_kv_norm_w']), krope


def mla_keys_values(ckv, krope, w_kv_up):
    b, t, _ = ckv.shape
    kv = (ckv @ w_kv_up).reshape(b, t, HB, NOPE + VB)
    k = jnp.concatenate([kv[..., :NOPE], jnp.broadcast_to(krope[:, :, None, :], (b, t, HB, ROPE))], axis=-1)
    return k, kv[..., NOPE:]


def mla_context(parts, p):
    q_nope, q_rope, ckv, krope = mla_project(parts, p)
    k, v = mla_keys_values(ckv, krope, p['mla_w_kv_up'])
    o = block_attention(jnp.concatenate([q_nope, q_rope], axis=-1), k, v, MLA_SCALE)
    b, t = o.shape[:2]
    return o.reshape(b, t, HB * VB), ckv, krope


def mla_latent(parts, p, ckv_ctx, krope_ctx):
    q_nope, q_rope, ckv, krope = mla_project(parts, p)
    q = jnp.concatenate([q_nope, rope_2d(q_rope)], axis=-1)
    k_lat, v_lat = mla_keys_values(ckv, rope_2d(krope), p['mla_w_kv_up'])
    k_ctx, v_ctx = mla_keys_values(ckv_ctx.astype(ckv.dtype), krope_ctx.astype(krope.dtype), p['mla_w_kv_up'])
    o = block_attention(q, jnp.concatenate([k_lat, k_ctx], axis=1), jnp.concatenate([v_lat, v_ctx], axis=1),
                        MLA_SCALE)
    b, t = o.shape[:2]
    return o.reshape(b, t, HB * VB)


def natten_context(parts):
    b, t, _ = parts[0].shape
    q, k, v = [a.reshape(b, t, HD, DHD) for a in parts]
    o = block_attention(q, k, v, DHD ** -0.5)
    return o.reshape(b, t, GROUP_W), k, v


def natten_latent(parts, rpb, k_ctx, v_ctx):
    b, t, _ = parts[0].shape
    q, k, v = [a.reshape(b, t, HD, DHD) for a in parts]
    rows = t // GRID_W
    wr = min(WIN_R, rows)
    qg = q.reshape(b, rows, GRID_W, HD, DHD)
    kg = k.reshape(b, rows, GRID_W, HD, DHD)
    vg = v.reshape(b, rows, GRID_W, HD, DHD)
    k_ctx = k_ctx.astype(k.dtype)
    v_ctx = v_ctx.astype(v.dtype)
    cols = jnp.arange(GRID_W)
    cidx = jnp.clip(cols - WIN_C // 2, 0, GRID_W - WIN_C)[:, None] + jnp.arange(WIN_C)[None, :]
    c_off = cidx - cols[:, None] + (WIN_C - 1)
    rpb_c = rpb[:, :, c_off].astype(F32)
    scale = DHD ** -0.5
    n_loc = wr * WIN_C

    def one_row(r):
        rs = jnp.clip(r - wr // 2, 0, rows - wr)
        kn = lax.dynamic_slice_in_dim(kg, rs, wr, axis=1)[:, :, cidx]
        vn = lax.dynamic_slice_in_dim(vg, rs, wr, axis=1)[:, :, cidx]
        qr = lax.dynamic_index_in_dim(qg, r, axis=1, keepdims=False)
        r_off = rs + jnp.arange(wr) - r + (WIN_R - 1)
        bias = jnp.take(rpb_c, r_off, axis=1).transpose(0, 2, 1, 3)
        s_loc = jnp.einsum('bqhd,bwqjhd->bhqwj', qr, kn).astype(F32) * scale + bias[None]
        s_ctx = jnp.einsum('bqhd,bkhd->bhqk', qr, k_ctx).astype(F32) * scale
        s = jnp.concatenate([s_loc.reshape(b, HD, GRID_W, n_loc), s_ctx], axis=-1)
        pr = jax.nn.softmax(s, axis=-1).astype(v.dtype)
        p_loc = pr[..., :n_loc].reshape(b, HD, GRID_W, wr, WIN_C)
        o = jnp.einsum('bhqwj,bwqjhd->bqhd', p_loc, vn)
        return o + jnp.einsum('bhqk,bkhd->bqhd', pr[..., n_loc:], v_ctx)
    o = lax.map(one_row, jnp.arange(rows))
    return o.transpose(1, 0, 2, 3, 4).reshape(b, t, GROUP_W)


def peer_ffn(h, w_q, sub_keys, u_tab, v_tab):
    b, t, d = h.shape
    n = b * t
    hf = h.reshape(n, d)
    q = (hf @ w_q).reshape(n, PEER_HEADS, 2, PEER_QDIM // 2)
    s = jnp.einsum('nhpd,pkd->nhpk', q, sub_keys).astype(F32)
    s1, i1 = lax.top_k(s[:, :, 0], PEER_TOPK)
    s2, i2 = lax.top_k(s[:, :, 1], PEER_TOPK)
    cand_s = (s1[..., :, None] + s2[..., None, :]).reshape(n, PEER_HEADS, PEER_TOPK * PEER_TOPK)
    cand_i = (i1[..., :, None] * N_KEYS + i2[..., None, :]).reshape(n, PEER_HEADS, PEER_TOPK * PEER_TOPK)
    top_s, pos = lax.top_k(cand_s, PEER_TOPK)
    idx = jnp.take_along_axis(cand_i, pos, axis=-1)
    gate = jax.nn.softmax(top_s, axis=-1)
    nb = n // PEER_TOKEN_BLOCK

    def block(inp):
        xb, ib, gb = inp
        act = jax.nn.gelu(jnp.einsum('td,thkd->thk', xb, u_tab[ib]).astype(F32)) * gb
        return jnp.einsum('thk,thkd->td', act.astype(xb.dtype), v_tab[ib])
    out = lax.map(block, (hf.reshape(nb, PEER_TOKEN_BLOCK, d),
                          idx.reshape(nb, PEER_TOKEN_BLOCK, PEER_HEADS, PEER_TOPK),
                          gate.reshape(nb, PEER_TOKEN_BLOCK, PEER_HEADS, PEER_TOPK)))
    return out.reshape(b, t, d)


def trunk_layer(x, cond, p, cache):
    mod = jax.nn.silu(cond) @ p['w_ada'] + p['b_ada']
    if cond.ndim == 2:
        mod = mod[:, None, :]
    sh1, sc1, g1, sh2, sc2, g2 = jnp.split(mod, 6, axis=-1)
    h = rmsnorm(x, p['norm1_w']) * (1.0 + sc1) + sh1
    parts = split_cols(h @ p['w_in'], IN_SPLITS)
    pa, pb, pc, pd = parts[0:5], parts[5:8], parts[8:14], parts[14:17]
    latent = cache is not None
    o_a, st_a = hgrn2_mixer(pa, p['lb'], p['hgrn_norm_w'], cache[0] if latent else None)
    o_c, st_c = ssd_mixer(pc, p['ssd_conv_w'], p['ssd_conv_b'], p['ssd_dt_bias'], p['ssd_a_log'], p['ssd_d'],
                          p['ssd_norm_w'], cache[1] if latent else None)
    if latent:
        o_b = mla_latent(pb, p, cache[2], cache[3])
        o_d = natten_latent(pd, p['na_rpb'], cache[4], cache[5])
        new_state = None
    else:
        o_b, ckv, krope = mla_context(pb, p)
        o_d, k_na, v_na = natten_context(pd)
        new_state = (st_a, st_c, ckv, krope, k_na, v_na)
    mix = jnp.concatenate([o_a, o_b, o_c, o_d], axis=-1) @ p['w_out']
    x = x + g1 * mix
    h2 = rmsnorm(x, p['norm2_w']) * (1.0 + sc2) + sh2
    x = x + g2 * peer_ffn(h2, p['peer_w_q'], p['peer_sub_keys'], p['peer_u'], p['peer_v'])
    return x, new_state


def _final_norm_body(x_ref, w_ref, o_ref):
    x = x_ref[...]
    y = x * lax.rsqrt(jnp.mean(x * x, axis=-1, keepdims=True) + EPS)
    o_ref[...] = y * w_ref[...]


def final_rmsnorm(x, w):
    b, t, d = x.shape
    rows = 512
    out = pl.pallas_call(
        _final_norm_body,
        grid=(b * t // rows,),
        in_specs=[pl.BlockSpec((rows, d), lambda i: (i, 0)), pl.BlockSpec((1, d), lambda i: (0, 0))],
        out_specs=pl.BlockSpec((rows, d), lambda i: (i, 0)),
        out_shape=jax.ShapeDtypeStruct((b * t, d), x.dtype),
        name="final_rmsnorm",
    )(x.reshape(b * t, d), w.reshape(1, d))
    return out.reshape(b, t, d)


def kernel(x_prompt, x_sample, c, state_hgrn, state_ssd, cache_mla_ckv, cache_mla_krope, cache_na_k,
           cache_na_v, c_ctx, w_ada, b_ada, norm1_w, norm2_w, w_in, w_out, hgrn_lb_logits, hgrn_norm_w,
           mla_q_norm_w, mla_w_q_up, mla_kv_norm_w, mla_w_kv_up, ssd_conv_w, ssd_conv_b, ssd_dt_bias,
           ssd_a_log, ssd_d, ssd_norm_w, na_rpb, peer_w_q, peer_sub_keys, peer_u, peer_v, final_norm_w):
    lb_soft = jax.nn.softmax(hgrn_lb_logits.astype(F32), axis=0)
    lb_all = jnp.cumsum(lb_soft, axis=0) - lb_soft[0]
    stacked = {'w_ada': w_ada, 'b_ada': b_ada, 'norm1_w': norm1_w, 'norm2_w': norm2_w, 'w_in': w_in,
               'w_out': w_out, 'hgrn_norm_w': hgrn_norm_w, 'mla_q_norm_w': mla_q_norm_w,
               'mla_w_q_up': mla_w_q_up, 'mla_kv_norm_w': mla_kv_norm_w, 'mla_w_kv_up': mla_w_kv_up,
               'ssd_conv_w': ssd_conv_w, 'ssd_conv_b': ssd_conv_b, 'ssd_dt_bias': ssd_dt_bias,
               'ssd_a_log': ssd_a_log, 'ssd_d': ssd_d, 'ssd_norm_w': ssd_norm_w, 'na_rpb': na_rpb,
               'peer_w_q': peer_w_q, 'peer_sub_keys': peer_sub_keys, 'peer_u': peer_u, 'peer_v': peer_v}

    xp = x_prompt
    ctx_states = []
    for l in range(DEPTH):
        p = {name: arr[l] for name, arr in stacked.items()}
        p['lb'] = lb_all[l]
        xp, st = trunk_layer(xp, c_ctx, p, None)
        ctx_states.append(st)

    xs = x_sample
    for l in range(DEPTH):
        p = {name: arr[l] for name, arr in stacked.items()}
        p['lb'] = lb_all[l]
        cache_l = (state_hgrn[:, l], state_ssd[:, l], cache_mla_ckv[:, l], cache_mla_krope[:, l],
                   cache_na_k[:, l], cache_na_v[:, l])
        xs, _ = trunk_layer(xs, c, p, cache_l)

    y_prompt = final_rmsnorm(xp, final_norm_w)
    y_sample = final_rmsnorm(xs, final_norm_w)
    new_state_hgrn = jnp.stack([s[0] for s in ctx_states], axis=1)
    new_state_ssd = jnp.stack([s[1] for s in ctx_states], axis=1)
    new_cache_mla_ckv = jnp.stack([s[2] for s in ctx_states], axis=1)
    new_cache_mla_krope = jnp.stack([s[3] for s in ctx_states], axis=1)
    new_cache_na_k = jnp.stack([s[4] for s in ctx_states], axis=1)
    new_cache_na_v = jnp.stack([s[5] for s in ctx_states], axis=1)
    return (y_prompt, y_sample, new_state_hgrn, new_state_ssd, new_cache_mla_ckv, new_cache_mla_krope,
            new_cache_na_k, new_cache_na_v)
```

```python
import math
import jax
import jax.numpy as jnp
from jax import lax
import numpy as np
from jax.experimental import pallas as pl
from jax.experimental.pallas import tpu as pltpu

D_MODEL = 2048
BATCH = 32
SEQ = 256
DEPTH = 2
DEC_BATCH = 4
DEC_SEQ = 1024
PAST_LEN = 256

GRID_W = 64
EPS = 1e-6
ROPE_THETA = 10000.0
Q_BLOCK = 128
CHUNK = 64
N_MIXERS = 4
GROUP_W = D_MODEL // N_MIXERS
D_MIX = N_MIXERS * GROUP_W
HA = 4
DKA = GROUP_W // HA
DVA = GROUP_W // HA
LB_FLOOR = 1e-30
HB = 4
Q_RANK = D_MODEL // 4
KV_RANK = D_MODEL // 8
NOPE = 128
ROPE = 64
VB = GROUP_W // HB
MLA_SCALE = (NOPE + ROPE) ** -0.5
HC = 8
PC = GROUP_W // HC
NC = 128
GC = 2
CONV_W = 3
CONV_CH = GROUP_W + 2 * GC * NC
HD = 8
DHD = GROUP_W // HD
WIN_R = 8
WIN_C = 16
N_KEYS = 128
N_EXPERTS = N_KEYS * N_KEYS
PEER_HEADS = 8
PEER_QDIM = 256
PEER_TOPK = 16
PEER_TOKEN_BLOCK = 128

IN_SPLITS = (HA * DKA, HA * DKA, HA * DKA, HA * DVA, HA * DVA,
             Q_RANK, KV_RANK, ROPE,
             GROUP_W, GROUP_W, GC * NC, GC * NC, HC, HC,
             GROUP_W, GROUP_W, GROUP_W)
IN_W = sum(IN_SPLITS)
F32 = jnp.float32


def rmsnorm(x, w):
    xf = x.astype(F32)
    y = xf * lax.rsqrt(jnp.mean(xf * xf, axis=-1, keepdims=True) + EPS)
    return (y * w.astype(F32)).astype(x.dtype)


def split_cols(a, sizes):
    offs = []
    acc = 0
    for s in sizes[:-1]:
        acc += s
        offs.append(acc)
    return jnp.split(a, offs, axis=-1)


def flip_t(a):
    return jnp.flip(a, axis=1)


def conv_centred(x, w, b):
    ch = x.shape[-1]
    k = w.shape[0]
    y = lax.conv_general_dilated(x, w[:, None, :].astype(x.dtype), window_strides=(1,),
                                 padding=[(k // 2, k // 2)], dimension_numbers=('NWC', 'WIO', 'NWC'),
                                 feature_group_count=ch)
    return y + b.astype(x.dtype)


def rope_2d(x):
    t_len = x.shape[1]
    t = jnp.arange(t_len)
    half = ROPE // 2
    inv = 1.0 / (ROPE_THETA ** (jnp.arange(0, half, 2, dtype=F32) / half))
    rows = (t // GRID_W).astype(F32)
    cols = (t % GRID_W).astype(F32)

    def rot(xa, pos):
        ang = (pos[:, None] * inv[None, :]).reshape((t_len,) + (1,) * (x.ndim - 3) + (inv.shape[0],))
        cs, sn = jnp.cos(ang).astype(x.dtype), jnp.sin(ang).astype(x.dtype)
        x1, x2 = jnp.split(xa, 2, axis=-1)
        return jnp.concatenate([x1 * cs - x2 * sn, x2 * cs + x1 * sn], axis=-1)
    return jnp.concatenate([rot(x[..., :half], rows), rot(x[..., half:], cols)], axis=-1)


def block_attention(q, k, v, scale):
    b, tq, h, dq = q.shape
    nb = tq // Q_BLOCK
    qb = q.reshape(b, nb, Q_BLOCK, h, dq).transpose(1, 0, 2, 3, 4)

    def one(qi):
        s = jnp.einsum('bqhd,bkhd->bhqk', qi, k).astype(F32) * scale
        pr = jax.nn.softmax(s, axis=-1).astype(v.dtype)
        return jnp.einsum('bhqk,bkhd->bqhd', pr, v)
    o = lax.map(one, qb)
    return o.transpose(1, 0, 2, 3, 4).reshape(b, tq, h, v.shape[-1])


def masked_exp(mask, diff):
    return jnp.where(mask, jnp.exp(jnp.where(mask, diff, 0.0)), 0.0)


def gla_chunk_scan(q, k, v, logf, s0):
    b, t, h, dk = q.shape
    dv = v.shape[-1]
    n = t // CHUNK

    def chunks(a):
        return a.reshape(b, n, CHUNK, h, a.shape[-1]).transpose(1, 0, 3, 2, 4)
    cum = jnp.cumsum(chunks(logf), axis=3)
    mask = jnp.tril(jnp.ones((CHUNK, CHUNK), dtype=bool))[:, :, None]

    def step(state, inp):
        qi, ki, vi, ci = inp
        decay = masked_exp(mask, ci[:, :, :, None, :] - ci[:, :, None, :, :])
        att = jnp.einsum('bhtk,bhsk,bhtsk->bhts', qi, ki, decay)
        o = jnp.einsum('bhts,bhsv->bhtv', att, vi)
        o = o + jnp.einsum('bhtk,bhkv->bhtv', qi * jnp.exp(ci), state)
        last = ci[:, :, -1]
        state = jnp.exp(last)[..., None] * state + jnp.einsum(
            'bhsk,bhsv->bhkv', ki * jnp.exp(last[:, :, None, :] - ci), vi)
        return state, o
    s_fin, o = lax.scan(step, s0, (chunks(q), chunks(k), chunks(v), cum))
    return o.transpose(1, 0, 3, 2, 4).reshape(b, t, h, dv), s_fin


def ssd_chunk_scan(x, dt, a, bm, cm, s0):
    b, t, h, p = x.shape
    g, n_st = bm.shape[2], bm.shape[3]
    r = h // g
    n = t // CHUNK
    xdt = (x * dt[..., None]).reshape(b, n, CHUNK, g, r, p).transpose(1, 0, 2, 3, 4, 5)
    la = (dt * a).reshape(b, n, CHUNK, g, r).transpose(1, 0, 2, 3, 4)
    bc = bm.reshape(b, n, CHUNK, g, n_st).transpose(1, 0, 2, 3, 4)
    cc = cm.reshape(b, n, CHUNK, g, n_st).transpose(1, 0, 2, 3, 4)
    mask = jnp.tril(jnp.ones((CHUNK, CHUNK), dtype=bool))[None, :, :, None, None]

    def step(state, inp):
        xi, ai, bi, ci = inp
        cum = jnp.cumsum(ai, axis=1)
        lmat = masked_exp(mask, cum[:, :, None] - cum[:, None, :])
        cb = jnp.einsum('btgn,bsgn->btsg', ci, bi)
        y = jnp.einsum('btsg,btsgr,bsgrp->btgrp', cb, lmat, xi)
        y = y + jnp.einsum('btgn,bgrpn->btgrp', ci, state) * jnp.exp(cum)[..., None]
        last = cum[:, -1]
        w = jnp.exp(last[:, None] - cum)
        state = jnp.exp(last)[..., None, None] * state + jnp.einsum('bsgn,bsgr,bsgrp->bgrpn', bi, w, xi)
        return state, y
    s_fin, y = lax.scan(step, s0.reshape(b, g, r, p, n_st), (xdt, la, bc, cc))
    return y.transpose(1, 0, 2, 3, 4, 5).reshape(b, t, h, p), s_fin.reshape(b, h, p, n_st)


def hgrn2_mixer(parts, lb, norm_w, init_state):
    q, f_fwd, f_bwd, i_in, g_out = parts
    b, t, _ = q.shape
    q = jax.nn.silu(q.astype(F32)).reshape(b, t, HA, DKA) * DKA ** -0.5
    v = i_in.astype(F32).reshape(b, t, HA, DVA)
    lb = lb.astype(F32).reshape(2, HA, DKA)
    outs, finals = [], []
    for d, f_raw in enumerate((f_fwd, f_bwd)):
        fx = f_raw.astype(F32).reshape(b, t, HA, DKA)
        logf = jnp.logaddexp(jnp.log(jnp.maximum(lb[d], LB_FLOOR)), jnp.log1p(-lb[d]) + jax.nn.log_sigmoid(fx))
        k = (1.0 - lb[d]) * jax.nn.sigmoid(-fx)
        if init_state is None:
            s0 = jnp.zeros((b, HA, DKA, DVA), F32)
        else:
            s0 = init_state[:, d].astype(F32)
        if d == 0:
            o, sf = gla_chunk_scan(q, k, v, logf, s0)
        else:
            o, sf = gla_chunk_scan(flip_t(q), flip_t(k), flip_t(v), flip_t(logf), s0)
            o = flip_t(o)
        outs.append(o)
        finals.append(sf)
    o = rmsnorm(outs[0] + outs[1], norm_w.reshape(HA, DVA)) * jax.nn.silu(
        g_out.astype(F32).reshape(b, t, HA, DVA))
    return o.reshape(b, t, HA * DVA).astype(g_out.dtype), jnp.stack(finals, axis=1)


def ssd_mixer(parts, conv_w, conv_b, dt_bias, a_log, d_skip, norm_w, init_state):
    x, z, b_in, c_in, dt_fwd, dt_bwd = parts
    b, t, _ = x.shape
    xbc = jax.nn.silu(conv_centred(jnp.concatenate([x, b_in, c_in], axis=-1), conv_w, conv_b))
    xs, bs, cs = split_cols(xbc, (GROUP_W, GC * NC, GC * NC))
    xh = xs.astype(F32).reshape(b, t, HC, PC)
    bg = bs.astype(F32).reshape(b, t, GC, NC)
    cg = cs.astype(F32).reshape(b, t, GC, NC)
    a = -jnp.exp(a_log.astype(F32))
    y = d_skip.astype(F32)[:, None] * xh
    finals = []
    for d, dt_raw in enumerate((dt_fwd, dt_bwd)):
        dt = jax.nn.softplus(dt_raw.astype(F32) + dt_bias[d].astype(F32))
        if init_state is None:
            s0 = jnp.zeros((b, HC, PC, NC), F32)
        else:
            s0 = init_state[:, d].astype(F32)
        if d == 0:
            yd, sf = ssd_chunk_scan(xh, dt, a[d], bg, cg, s0)
        else:
            yd, sf = ssd_chunk_scan(flip_t(xh), flip_t(dt), a[d], flip_t(bg), flip_t(cg), s0)
            yd = flip_t(yd)
        y = y + yd
        finals.append(sf)
    y = y.reshape(b, t, GROUP_W) * jax.nn.silu(z.astype(F32))
    return rmsnorm(y, norm_w).astype(x.dtype), jnp.stack(finals, axis=1)


def mla_project(parts, p):
    cq, ckv, krope = parts
    b, t, _ = cq.shape
    q = (rmsnorm(cq, p['mla_q_norm_w']) @ p['mla_w_q_up']).reshape(b, t, HB, NOPE + ROPE)
    return q[..., :NOPE], q[..., NOPE:], rmsnorm(ckv, p['mla_kv_norm_w']), krope


def mla_keys_values(ckv, krope, w_kv_up):
    b, t, _ = ckv.shape
    kv = (ckv @ w_kv_up).reshape(b, t, HB, NOPE + VB)
    k = jnp.concatenate([kv[..., :NOPE], jnp.broadcast_to(krope[:, :, None, :], (b, t, HB, ROPE))], axis=-1)
    return k, kv[..., NOPE:]


def mla_context(parts, p):
    q_nope, q_rope, ckv, krope = mla_project(parts, p)
    k, v = mla_keys_values(ckv, krope, p['mla_w_kv_up'])
    o = block_attention(jnp.concatenate([q_nope, q_rope], axis=-1), k, v, MLA_SCALE)
    b, t = o.shape[:2]
    return o.reshape(b, t, HB * VB), ckv, krope


def mla_latent(parts, p, ckv_ctx, krope_ctx):
    q_nope, q_rope, ckv, krope = mla_project(parts, p)
    q = jnp.concatenate([q_nope, rope_2d(q_rope)], axis=-1)
    k_lat, v_lat = mla_keys_values(ckv, rope_2d(krope), p['mla_w_kv_up'])
    k_ctx, v_ctx = mla_keys_values(ckv_ctx.astype(ckv.dtype), krope_ctx.astype(krope.dtype), p['mla_w_kv_up'])
    o = block_attention(q, jnp.concatenate([k_lat, k_ctx], axis=1), jnp.concatenate([v_lat, v_ctx], axis=1),
                        MLA_SCALE)
    b, t = o.shape[:2]
    return o.reshape(b, t, HB * VB)


def natten_context(parts):
    b, t, _ = parts[0].shape
    q, k, v = [a.reshape(b, t, HD, DHD) for a in parts]
    o = block_attention(q, k, v, DHD ** -0.5)
    return o.reshape(b, t, GROUP_W), k, v


def natten_latent(parts, rpb, k_ctx, v_ctx):
    b, t, _ = parts[0].shape
    q, k, v = [a.reshape(b, t, HD, DHD) for a in parts]
    rows = t // GRID_W
    wr = min(WIN_R, rows)
    qg = q.reshape(b, rows, GRID_W, HD, DHD)
    kg = k.reshape(b, rows, GRID_W, HD, DHD)
    vg = v.reshape(b, rows, GRID_W, HD, DHD)
    k_ctx = k_ctx.astype(k.dtype)
    v_ctx = v_ctx.astype(v.dtype)
    cols = jnp.arange(GRID_W)
    cidx = jnp.clip(cols - WIN_C // 2, 0, GRID_W - WIN_C)[:, None] + jnp.arange(WIN_C)[None, :]
    c_off = cidx - cols[:, None] + (WIN_C - 1)
    rpb_c = rpb[:, :, c_off].astype(F32)
    scale = DHD ** -0.5
    n_loc = wr * WIN_C

    def one_row(r):
        rs = jnp.clip(r - wr // 2, 0, rows - wr)
        kn = lax.dynamic_slice_in_dim(kg, rs, wr, axis=1)[:, :, cidx]
        vn = lax.dynamic_slice_in_dim(vg, rs, wr, axis=1)[:, :, cidx]
        qr = lax.dynamic_index_in_dim(qg, r, axis=1, keepdims=False)
        r_off = rs + jnp.arange(wr) - r + (WIN_R - 1)
        bias = jnp.take(rpb_c, r_off, axis=1).transpose(0, 2, 1, 3)
        s_loc = jnp.einsum('bqhd,bwqjhd->bhqwj', qr, kn).astype(F32) * scale + bias[None]
        s_ctx = jnp.einsum('bqhd,bkhd->bhqk', qr, k_ctx).astype(F32) * scale
        s = jnp.concatenate([s_loc.reshape(b, HD, GRID_W, n_loc), s_ctx], axis=-1)
        pr = jax.nn.softmax(s, axis=-1).astype(v.dtype)
        p_loc = pr[..., :n_loc].reshape(b, HD, GRID_W, wr, WIN_C)
        o = jnp.einsum('bhqwj,bwqjhd->bqhd', p_loc, vn)
        return o + jnp.einsum('bhqk,bkhd->bqhd', pr[..., n_loc:], v_ctx)
    o = lax.map(one_row, jnp.arange(rows))
    return o.transpose(1, 0, 2, 3, 4).reshape(b, t, GROUP_W)


BF16 = jnp.bfloat16
LANES = 128
PEER_KDIM = PEER_QDIM // 2
PEER_HP = PEER_HEADS * 2
PEER_PREP_TE = 512
PEER_SCORE_TM = 512
PEER_SELECT_TM = 256
PEER_TM = 512
PEER_TE = 1024
PEER_ROWS = PEER_TE // N_KEYS
PEER_CAND = [(a, b) for a in range(PEER_TOPK) for b in range(PEER_TOPK) if (a + 1) * (b + 1) <= PEER_TOPK]
VMEM_LIMIT = 56 * 1024 * 1024


def _peer_prep_body(u_ref, v_ref, ub_ref, vt_ref):
    ub_ref[...] = u_ref[...].astype(BF16)
    vt_ref[...] = v_ref[...].T.astype(BF16)


def peer_prep_tables(peer_u, peer_v):
    depth, n_e, d = peer_u.shape
    te = PEER_PREP_TE
    return pl.pallas_call(
        _peer_prep_body,
        grid=(depth, n_e // te),
        in_specs=[pl.BlockSpec((None, te, d), lambda l, j: (l, j, 0)),
                  pl.BlockSpec((None, te, d), lambda l, j: (l, j, 0))],
        out_specs=[pl.BlockSpec((None, te, d), lambda l, j: (l, j, 0)),
                   pl.BlockSpec((None, d, te), lambda l, j: (l, 0, j))],
        out_shape=[jax.ShapeDtypeStruct((depth, n_e, d), BF16), jax.ShapeDtypeStruct((depth, d, n_e), BF16)],
        compiler_params=pltpu.CompilerParams(dimension_semantics=("parallel", "parallel"),
                                             vmem_limit_bytes=VMEM_LIMIT),
        name="peer_prep_tables",
    )(peer_u, peer_v)


def _peer_score_body(h_ref, wqt_ref, sk_ref, xt_ref, st_ref):
    xt = h_ref[...].T.astype(BF16)
    xt_ref[...] = xt
    qt = jnp.dot(wqt_ref[...], xt, preferred_element_type=F32).astype(BF16)
    for hp in range(PEER_HP):
        st_ref[hp] = jnp.dot(sk_ref[hp % 2], qt[hp * PEER_KDIM:(hp + 1) * PEER_KDIM, :],
                             preferred_element_type=F32)


def peer_scores(hf, wq_t, sub_keys):
    n, d = hf.shape
    tm = PEER_SCORE_TM
    return pl.pallas_call(
        _peer_score_body,
        grid=(n // tm,),
        in_specs=[pl.BlockSpec((tm, d), lambda i: (i, 0)),
                  pl.BlockSpec(wq_t.shape, lambda i: (0, 0)),
                  pl.BlockSpec(sub_keys.shape, lambda i: (0, 0, 0))],
        out_specs=[pl.BlockSpec((d, tm), lambda i: (0, i)),
                   pl.BlockSpec((PEER_HP, N_KEYS, tm), lambda i: (0, 0, i))],
        out_shape=[jax.ShapeDtypeStruct((d, n), BF16), jax.ShapeDtypeStruct((PEER_HP, N_KEYS, n), F32)],
        compiler_params=pltpu.CompilerParams(dimension_semantics=("parallel",), vmem_limit_bytes=VMEM_LIMIT),
        name="peer_scores",
    )(hf, wq_t, sub_keys)


def _peer_select_body(st_ref, e1_ref, cnt1_ref, rank2_ref, e2_ref, rank1_s, vtop_s, cnt_s, zinv_s):
    tm = st_ref.shape[-1]
    kio = lax.broadcasted_iota(jnp.int32, (N_KEYS, LANES), 0).astype(F32)
    neg = jnp.float32(-jnp.inf)

    def group(g, carry):
        ln = pl.ds(pl.multiple_of(g * LANES, LANES), LANES)

        for hp in range(PEER_HP):
            h, part = hp // 2, hp % 2

            def extract(it, sr):
                s, rank = sr
                m = jnp.max(s, axis=0, keepdims=True)
                first = jnp.min(jnp.where(s == m, kio, float(N_KEYS)), axis=0, keepdims=True)
                sel = kio == first
                vtop_s[part, it, h:h + 1, ln] = m
                return jnp.where(sel, neg, s), jnp.where(sel, it, rank)
            _, rank = lax.fori_loop(0, PEER_TOPK, extract,
                                    (st_ref[hp, :, ln], jnp.full((N_KEYS, LANES), PEER_TOPK, jnp.int32)))
            rank = rank.astype(F32)
            if part == 0:
                rank1_s[h, :, ln] = rank
            else:
                rank2_ref[h, :, ln] = rank

        v1 = [vtop_s[0, a, :, ln] for a in range(PEER_TOPK)]
        v2 = [vtop_s[1, b, :, ln] for b in range(PEER_TOPK)]
        sums = [v1[a] + v2[b] for a, b in PEER_CAND]
        n_c = len(PEER_CAND)
        beaten = [jnp.zeros((PEER_HEADS, LANES), F32) for _ in range(n_c)]
        for i in range(n_c):
            for j in range(i + 1, n_c):
                ge = sums[i] >= sums[j]
                beaten[j] = beaten[j] + jnp.where(ge, 1.0, 0.0)
                beaten[i] = beaten[i] + jnp.where(ge, 0.0, 1.0)
        z = jnp.zeros((PEER_HEADS, LANES), F32)
        cnt = [jnp.zeros((PEER_HEADS, LANES), F32) for _ in range(PEER_TOPK)]
        for i, (a, b) in enumerate(PEER_CAND):
            keep = beaten[i] < float(PEER_TOPK)
            z = z + jnp.where(keep, jnp.exp(sums[i] - sums[0]), 0.0)
            cnt[a] = cnt[a] + jnp.where(keep, 1.0, 0.0)
        for a in range(PEER_TOPK):
            cnt_s[a, :, ln] = cnt[a]
        zinv_s[:, ln] = 1.0 / z

        for h in range(PEER_HEADS):
            rank1 = rank1_s[h, :, ln]
            s1 = st_ref[2 * h, :, ln]
            e1 = jnp.exp(s1 - vtop_s[0, 0, h:h + 1, ln]) * zinv_s[h:h + 1, ln]
            e1_ref[h, :, ln] = jnp.where(rank1 < float(PEER_TOPK), e1, 0.0)
            c1 = jnp.zeros((N_KEYS, LANES), F32)
            for a in range(PEER_TOPK):
                c1 = c1 + jnp.where(rank1 == float(a), cnt_s[a, h:h + 1, ln], 0.0)
            cnt1_ref[h, :, ln] = c1
            e2_ref[h, :, ln] = jnp.exp(st_ref[2 * h + 1, :, ln] - vtop_s[1, 0, h:h + 1, ln])
        return carry
    lax.fori_loop(0, tm // LANES, group, 0)


def peer_select(st):
    _, _, n = st.shape
    tm = PEER_SELECT_TM
    spec = pl.BlockSpec((PEER_HEADS, N_KEYS, tm), lambda i: (0, 0, i))
    shp = jax.ShapeDtypeStruct((PEER_HEADS, N_KEYS, n), F32)
    return pl.pallas_call(
        _peer_select_body,
        grid=(n // tm,),
        in_specs=[pl.BlockSpec((PEER_HP, N_KEYS, tm), lambda i: (0, 0, i))],
        out_specs=[spec, spec, spec, spec],
        out_shape=[shp, shp, shp, shp],
        scratch_shapes=[pltpu.VMEM((PEER_HEADS, N_KEYS, tm), F32),
                        pltpu.VMEM((2, PEER_TOPK, PEER_HEADS, tm), F32),
                        pltpu.VMEM((PEER_TOPK, PEER_HEADS, tm), F32),
                        pltpu.VMEM((PEER_HEADS, tm), F32)],
        compiler_params=pltpu.CompilerParams(dimension_semantics=("parallel",), vmem_limit_bytes=VMEM_LIMIT),
        name="peer_select",
    )(st)


def _gelu_tanh(x):
    return 0.5 * x * (1.0 + jnp.tanh(math.sqrt(2.0 / math.pi) * (x + 0.044715 * (x * x * x))))


def _peer_expert_body(u_ref, vt_ref, xt_ref, e1_ref, cnt1_ref, rank2_ref, e2_ref, o_ref, st_s, at_s):
    j = pl.program_id(1)
    tm = xt_ref.shape[-1]

    @pl.when(j == 0)
    def _():
        o_ref[...] = jnp.zeros_like(o_ref)

    st_s[...] = jnp.dot(u_ref[...], xt_ref[...], preferred_element_type=F32)

    def group(g, carry):
        ln = pl.ds(pl.multiple_of(g * LANES, LANES), LANES)
        for r in range(PEER_ROWS):
            gate = jnp.zeros((N_KEYS, LANES), F32)
            for h in range(PEER_HEADS):
                keep = rank2_ref[h, :, ln] < cnt1_ref[h, r:r + 1, ln]
                gate = gate + jnp.where(keep, e2_ref[h, :, ln], 0.0) * e1_ref[h, r:r + 1, ln]
            rows = slice(r * N_KEYS, (r + 1) * N_KEYS)
            at_s[rows, ln] = (_gelu_tanh(st_s[rows, ln]) * gate).astype(BF16)
        return carry
    lax.fori_loop(0, tm // LANES, group, 0)

    o_ref[...] += jnp.dot(vt_ref[...], at_s[...], preferred_element_type=F32)


def peer_experts(ub, vt, xt, e1, cnt1, rank2, e2):
    n_e, d = ub.shape
    n = xt.shape[1]
    tm, te = PEER_TM, PEER_TE
    row_spec = pl.BlockSpec((PEER_HEADS, PEER_ROWS, tm), lambda i, j: (0, j, i))
    col_spec = pl.BlockSpec((PEER_HEADS, N_KEYS, tm), lambda i, j: (0, 0, i))
    return pl.pallas_call(
        _peer_expert_body,
        grid=(n // tm, n_e // te),
        in_specs=[pl.BlockSpec((te, d), lambda i, j: (j, 0)),
                  pl.BlockSpec((d, te), lambda i, j: (0, j)),
                  pl.BlockSpec((d, tm), lambda i, j: (0, i)),
                  row_spec, row_spec, col_spec, col_spec],
        out_specs=pl.BlockSpec((d, tm), lambda i, j: (0, i)),
        out_shape=jax.ShapeDtypeStruct((d, n), F32),
        scratch_shapes=[pltpu.VMEM((te, tm), F32), pltpu.VMEM((te, tm), BF16)],
        compiler_params=pltpu.CompilerParams(dimension_semantics=("parallel", "arbitrary"),
                                             vmem_limit_bytes=VMEM_LIMIT),
        name="peer_experts",
    )(ub, vt, xt, e1, cnt1, rank2, e2)


def peer_ffn(hf, wq_t, sub_keys_b, ub, vt):
    xt, st = peer_scores(hf, wq_t, sub_keys_b)
    e1, cnt1, rank2, e2 = peer_select(st)
    return peer_experts(ub, vt, xt, e1, cnt1, rank2, e2).T


def trunk_layer(x, cond, p, cache):
    mod = jax.nn.silu(cond) @ p['w_ada'] + p['b_ada']
    if cond.ndim == 2:
        mod = mod[:, None, :]
    sh1, sc1, g1, sh2, sc2, g2 = jnp.split(mod, 6, axis=-1)
    h = rmsnorm(x, p['norm1_w']) * (1.0 + sc1) + sh1
    parts = split_cols(h @ p['w_in'], IN_SPLITS)
    pa, pb, pc, pd = parts[0:5], parts[5:8], parts[8:14], parts[14:17]
    latent = cache is not None
    o_a, st_a = hgrn2_mixer(pa, p['lb'], p['hgrn_norm_w'], cache[0] if latent else None)
    o_c, st_c = ssd_mixer(pc, p['ssd_conv_w'], p['ssd_conv_b'], p['ssd_dt_bias'], p['ssd_a_log'], p['ssd_d'],
                          p['ssd_norm_w'], cache[1] if latent else None)
    if latent:
        o_b = mla_latent(pb, p, cache[2], cache[3])
        o_d = natten_latent(pd, p['na_rpb'], cache[4], cache[5])
        new_state = None
    else:
        o_b, ckv, krope = mla_context(pb, p)
        o_d, k_na, v_na = natten_context(pd)
        new_state = (st_a, st_c, ckv, krope, k_na, v_na)
    mix = jnp.concatenate([o_a, o_b, o_c, o_d], axis=-1) @ p['w_out']
    x = x + g1 * mix
    h2 = rmsnorm(x, p['norm2_w']) * (1.0 + sc2) + sh2
    return x, h2, jnp.broadcast_to(g2, x.shape), new_state


def _final_norm_body(x_ref, w_ref, o_ref):
    x = x_ref[...]
    y = x * lax.rsqrt(jnp.mean(x * x, axis=-1, keepdims=True) + EPS)
    o_ref[...] = y * w_ref[...]


def final_rmsnorm(x, w):
    b, t, d = x.shape
    rows = 512
    out = pl.pallas_call(
        _final_norm_body,
        grid=(b * t // rows,),
        in_specs=[pl.BlockSpec((rows, d), lambda i: (i, 0)), pl.BlockSpec((1, d), lambda i: (0, 0))],
        out_specs=pl.BlockSpec((rows, d), lambda i: (i, 0)),
        out_shape=jax.ShapeDtypeStruct((b * t, d), x.dtype),
        name="final_rmsnorm",
    )(x.reshape(b * t, d), w.reshape(1, d))
    return out.reshape(b, t, d)


def kernel(x_prompt, x_sample, c, state_hgrn, state_ssd, cache_mla_ckv, cache_mla_krope, cache_na_k,
           cache_na_v, c_ctx, w_ada, b_ada, norm1_w, norm2_w, w_in, w_out, hgrn_lb_logits, hgrn_norm_w,
           mla_q_norm_w, mla_w_q_up, mla_kv_norm_w, mla_w_kv_up, ssd_conv_w, ssd_conv_b, ssd_dt_bias,
           ssd_a_log, ssd_d, ssd_norm_w, na_rpb, peer_w_q, peer_sub_keys, peer_u, peer_v, final_norm_w):
    lb_soft = jax.nn.softmax(hgrn_lb_logits.astype(F32), axis=0)
    lb_all = jnp.cumsum(lb_soft, axis=0) - lb_soft[0]
    stacked = {'w_ada': w_ada, 'b_ada': b_ada, 'norm1_w': norm1_w, 'norm2_w': norm2_w, 'w_in': w_in,
               'w_out': w_out, 'hgrn_norm_w': hgrn_norm_w, 'mla_q_norm_w': mla_q_norm_w,
               'mla_w_q_up': mla_w_q_up, 'mla_kv_norm_w': mla_kv_norm_w, 'mla_w_kv_up': mla_w_kv_up,
               'ssd_conv_w': ssd_conv_w, 'ssd_conv_b': ssd_conv_b, 'ssd_dt_bias': ssd_dt_bias,
               'ssd_a_log': ssd_a_log, 'ssd_d': ssd_d, 'ssd_norm_w': ssd_norm_w, 'na_rpb': na_rpb,
               'peer_w_q': peer_w_q, 'peer_sub_keys': peer_sub_keys, 'peer_u': peer_u, 'peer_v': peer_v}

    ub_all, vt_all = peer_prep_tables(peer_u, peer_v)
    n_ctx = BATCH * SEQ

    xp, xs = x_prompt, x_sample
    ctx_states = []
    for l in range(DEPTH):
        p = {name: arr[l] for name, arr in stacked.items()}
        p['lb'] = lb_all[l]
        xp, h2p, g2p, st = trunk_layer(xp, c_ctx, p, None)
        ctx_states.append(st)
        cache_l = (state_hgrn[:, l], state_ssd[:, l], cache_mla_ckv[:, l], cache_mla_krope[:, l],
                   cache_na_k[:, l], cache_na_v[:, l])
        xs, h2s, g2s, _ = trunk_layer(xs, c, p, cache_l)
        hf = jnp.concatenate([h2p.reshape(n_ctx, D_MODEL), h2s.reshape(-1, D_MODEL)], axis=0)
        ff = peer_ffn(hf, peer_w_q[l].T.astype(BF16), peer_sub_keys[l].astype(BF16), ub_all[l], vt_all[l])
        xp = xp + g2p * ff[:n_ctx].reshape(xp.shape)
        xs = xs + g2s * ff[n_ctx:].reshape(xs.shape)

    y_prompt = final_rmsnorm(xp, final_norm_w)
    y_sample = final_rmsnorm(xs, final_norm_w)
    new_state_hgrn = jnp.stack([s[0] for s in ctx_states], axis=1)
    new_state_ssd = jnp.stack([s[1] for s in ctx_states], axis=1)
    new_cache_mla_ckv = jnp.stack([s[2] for s in ctx_states], axis=1)
    new_cache_mla_krope = jnp.stack([s[3] for s in ctx_states], axis=1)
    new_cache_na_k = jnp.stack([s[4] for s in ctx_states], axis=1)
    new_cache_na_v = jnp.stack([s[5] for s in ctx_states], axis=1)
    return (y_prompt, y_sample, new_state_hgrn, new_state_ssd, new_cache_mla_ckv, new_cache_mla_krope,
            new_cache_na_k, new_cache_na_v)
```

```python
import math
import jax
import jax.numpy as jnp
from jax import lax
import numpy as np
from jax.experimental import pallas as pl
from jax.experimental.pallas import tpu as pltpu

D_MODEL = 2048
BATCH = 32
SEQ = 256
DEPTH = 2
DEC_BATCH = 4
DEC_SEQ = 1024
PAST_LEN = 256

GRID_W = 64
EPS = 1e-6
ROPE_THETA = 10000.0
Q_BLOCK = 128
CHUNK = 64
N_MIXERS = 4
GROUP_W = D_MODEL // N_MIXERS
D_MIX = N_MIXERS * GROUP_W
HA = 4
DKA = GROUP_W // HA
DVA = GROUP_W // HA
LB_FLOOR = 1e-30
HB = 4
Q_RANK = D_MODEL // 4
KV_RANK = D_MODEL // 8
NOPE = 128
ROPE = 64
VB = GROUP_W // HB
MLA_SCALE = (NOPE + ROPE) ** -0.5
HC = 8
PC = GROUP_W // HC
NC = 128
GC = 2
CONV_W = 3
CONV_CH = GROUP_W + 2 * GC * NC
HD = 8
DHD = GROUP_W // HD
WIN_R = 8
WIN_C = 16
N_KEYS = 128
N_EXPERTS = N_KEYS * N_KEYS
PEER_HEADS = 8
PEER_QDIM = 256
PEER_TOPK = 16
PEER_TOKEN_BLOCK = 128

IN_SPLITS = (HA * DKA, HA * DKA, HA * DKA, HA * DVA, HA * DVA,
             Q_RANK, KV_RANK, ROPE,
             GROUP_W, GROUP_W, GC * NC, GC * NC, HC, HC,
             GROUP_W, GROUP_W, GROUP_W)
IN_W = sum(IN_SPLITS)
F32 = jnp.float32


def rmsnorm(x, w):
    xf = x.astype(F32)
    y = xf * lax.rsqrt(jnp.mean(xf * xf, axis=-1, keepdims=True) + EPS)
    return (y * w.astype(F32)).astype(x.dtype)


def split_cols(a, sizes):
    offs = []
    acc = 0
    for s in sizes[:-1]:
        acc += s
        offs.append(acc)
    return jnp.split(a, offs, axis=-1)


def flip_t(a):
    return jnp.flip(a, axis=1)


def conv_centred(x, w, b):
    ch = x.shape[-1]
    k = w.shape[0]
    y = lax.conv_general_dilated(x, w[:, None, :].astype(x.dtype), window_strides=(1,),
                                 padding=[(k // 2, k // 2)], dimension_numbers=('NWC', 'WIO', 'NWC'),
                                 feature_group_count=ch)
    return y + b.astype(x.dtype)


def rope_2d(x):
    t_len = x.shape[1]
    t = jnp.arange(t_len)
    half = ROPE // 2
    inv = 1.0 / (ROPE_THETA ** (jnp.arange(0, half, 2, dtype=F32) / half))
    rows = (t // GRID_W).astype(F32)
    cols = (t % GRID_W).astype(F32)

    def rot(xa, pos):
        ang = (pos[:, None] * inv[None, :]).reshape((t_len,) + (1,) * (x.ndim - 3) + (inv.shape[0],))
        cs, sn = jnp.cos(ang).astype(x.dtype), jnp.sin(ang).astype(x.dtype)
        x1, x2 = jnp.split(xa, 2, axis=-1)
        return jnp.concatenate([x1 * cs - x2 * sn, x2 * cs + x1 * sn], axis=-1)
    return jnp.concatenate([rot(x[..., :half], rows), rot(x[..., half:], cols)], axis=-1)


def block_attention(q, k, v, scale):
    b, tq, h, dq = q.shape
    nb = tq // Q_BLOCK
    qb = q.reshape(b, nb, Q_BLOCK, h, dq).transpose(1, 0, 2, 3, 4)

    def one(qi):
        s = jnp.einsum('bqhd,bkhd->bhqk', qi, k).astype(F32) * scale
        pr = jax.nn.softmax(s, axis=-1).astype(v.dtype)
        return jnp.einsum('bhqk,bkhd->bqhd', pr, v)
    o = lax.map(one, qb)
    return o.transpose(1, 0, 2, 3, 4).reshape(b, tq, h, v.shape[-1])


def masked_exp(mask, diff):
    return jnp.where(mask, jnp.exp(jnp.where(mask, diff, 0.0)), 0.0)


def gla_chunk_scan(q, k, v, logf, s0):
    b, t, h, dk = q.shape
    dv = v.shape[-1]
    n = t // CHUNK

    def chunks(a):
        return a.reshape(b, n, CHUNK, h, a.shape[-1]).transpose(1, 0, 3, 2, 4)
    cum = jnp.cumsum(chunks(logf), axis=3)
    mask = jnp.tril(jnp.ones((CHUNK, CHUNK), dtype=bool))[:, :, None]

    def step(state, inp):
        qi, ki, vi, ci = inp
        decay = masked_exp(mask, ci[:, :, :, None, :] - ci[:, :, None, :, :])
        att = jnp.einsum('bhtk,bhsk,bhtsk->bhts', qi, ki, decay)
        o = jnp.einsum('bhts,bhsv->bhtv', att, vi)
        o = o + jnp.einsum('bhtk,bhkv->bhtv', qi * jnp.exp(ci), state)
        last = ci[:, :, -1]
        state = jnp.exp(last)[..., None] * state + jnp.einsum(
            'bhsk,bhsv->bhkv', ki * jnp.exp(last[:, :, None, :] - ci), vi)
        return state, o
    s_fin, o = lax.scan(step, s0, (chunks(q), chunks(k), chunks(v), cum))
    return o.transpose(1, 0, 3, 2, 4).reshape(b, t, h, dv), s_fin


def ssd_chunk_scan(x, dt, a, bm, cm, s0):
    b, t, h, p = x.shape
    g, n_st = bm.shape[2], bm.shape[3]
    r = h // g
    n = t // CHUNK
    xdt = (x * dt[..., None]).reshape(b, n, CHUNK, g, r, p).transpose(1, 0, 2, 3, 4, 5)
    la = (dt * a).reshape(b, n, CHUNK, g, r).transpose(1, 0, 2, 3, 4)
    bc = bm.reshape(b, n, CHUNK, g, n_st).transpose(1, 0, 2, 3, 4)
    cc = cm.reshape(b, n, CHUNK, g, n_st).transpose(1, 0, 2, 3, 4)
    mask = jnp.tril(jnp.ones((CHUNK, CHUNK), dtype=bool))[None, :, :, None, None]

    def step(state, inp):
        xi, ai, bi, ci = inp
        cum = jnp.cumsum(ai, axis=1)
        lmat = masked_exp(mask, cum[:, :, None] - cum[:, None, :])
        cb = jnp.einsum('btgn,bsgn->btsg', ci, bi)
        y = jnp.einsum('btsg,btsgr,bsgrp->btgrp', cb, lmat, xi)
        y = y + jnp.einsum('btgn,bgrpn->btgrp', ci, state) * jnp.exp(cum)[..., None]
        last = cum[:, -1]
        w = jnp.exp(last[:, None] - cum)
        state = jnp.exp(last)[..., None, None] * state + jnp.einsum('bsgn,bsgr,bsgrp->bgrpn', bi, w, xi)
        return state, y
    s_fin, y = lax.scan(step, s0.reshape(b, g, r, p, n_st), (xdt, la, bc, cc))
    return y.transpose(1, 0, 2, 3, 4, 5).reshape(b, t, h, p), s_fin.reshape(b, h, p, n_st)


HG_SUB = 16
HG_NSUB = CHUNK // HG_SUB


def _split3_bf16(x):
    hi = x.astype(BF16)
    r1 = x - hi.astype(F32)
    mid = r1.astype(BF16)
    lo = (r1 - mid.astype(F32)).astype(BF16)
    return hi, mid, lo


def _hgrn_direction(rev, q_ref, f_ref, v_ref, la, lc, om, st_s, d, o_ref):
    fx = f_ref[...]
    logf = jnp.logaddexp(la, lc + jax.nn.log_sigmoid(fx))
    kk = om * jax.nn.sigmoid(-fx)
    qq = jax.nn.silu(q_ref[...]) * (DKA ** -0.5)
    vv = v_ref[...]
    t_io = lax.broadcasted_iota(jnp.int32, (CHUNK, CHUNK), 0)
    s_io = lax.broadcasted_iota(jnp.int32, (CHUNK, CHUNK), 1)
    tri = jnp.where((s_io >= t_io) if rev else (s_io <= t_io), 1.0, 0.0).astype(BF16)
    c = sum(jnp.dot(tri, part, preferred_element_type=F32) for part in _split3_bf16(logf))
    row = lax.broadcasted_iota(jnp.int32, (CHUNK, 1), 0)
    sub_row = lax.broadcasted_iota(jnp.int32, (HG_SUB, 1), 0)
    lane = lax.broadcasted_iota(jnp.int32, (HG_SUB, CHUNK), 1)
    outs = []
    for h in range(HA):
        cs = slice(h * DKA, (h + 1) * DKA)
        ch, qh, kh, vh = c[:, cs], qq[:, cs], kk[:, cs], vv[:, cs]
        st = st_s[d, h]
        o = _dot_nt((qh * jnp.exp(ch)).astype(BF16), st.astype(BF16))
        att = jnp.zeros((CHUNK, CHUNK), F32)
        for i in range(HG_NSUB):
            if rev:
                if i == HG_NSUB - 1:
                    continue
                c_ref = ch[(i + 1) * HG_SUB:(i + 1) * HG_SUB + 1]
                k_side = row >= (i + 1) * HG_SUB
            else:
                if i == 0:
                    continue
                c_ref = ch[i * HG_SUB - 1:i * HG_SUB]
                k_side = row < i * HG_SUB
            q_side = (row >= i * HG_SUB) & (row < (i + 1) * HG_SUB)
            qs = jnp.where(q_side, qh * jnp.exp(jnp.where(q_side, ch - c_ref, 0.0)), 0.0)
            ks = jnp.where(k_side, kh * jnp.exp(jnp.where(k_side, c_ref - ch, 0.0)), 0.0)
            att = att + _dot_nt(qs.astype(BF16), ks.astype(BF16))
        strips = []
        for i in range(HG_NSUB):
            blk = slice(i * HG_SUB, (i + 1) * HG_SUB)
            cb, qb, kb = ch[blk], qh[blk], kh[blk]
            strip = jnp.zeros((HG_SUB, CHUNK), F32)
            for s in range(HG_SUB):
                causal = (sub_row <= s) if rev else (sub_row >= s)
                w = jnp.exp(jnp.where(causal, cb - cb[s:s + 1], 0.0))
                col = jnp.sum(jnp.where(causal, w * qb * kb[s:s + 1], 0.0), axis=-1, keepdims=True)
                strip = jnp.where(lane == i * HG_SUB + s, col, strip)
            strips.append(strip)
        att = att + jnp.concatenate(strips, axis=0)
        o = o + jnp.dot(att.astype(BF16), vh.astype(BF16), preferred_element_type=F32)
        outs.append(o)
        c_end = ch[0:1] if rev else ch[CHUNK - 1:CHUNK]
        kd = kh * jnp.exp(c_end - ch)
        st_s[d, h] = st * jnp.exp(c_end) + _dot_tn(vh.astype(BF16), kd.astype(BF16))
    o_ref[...] = jnp.concatenate(outs, axis=-1)


def _hgrn_body(qf_ref, ff_ref, vf_ref, qb_ref, fb_ref, vb_ref, la_ref, lc_ref, om_ref, s0_ref,
               of_ref, ob_ref, sfin_ref, st_s):
    i = pl.program_id(1)

    @pl.when(i == 0)
    def _():
        for d in range(2):
            for h in range(HA):
                st_s[d, h] = s0_ref[d, h].T

    _hgrn_direction(False, qf_ref, ff_ref, vf_ref, la_ref[0:1], lc_ref[0:1], om_ref[0:1], st_s, 0, of_ref)
    _hgrn_direction(True, qb_ref, fb_ref, vb_ref, la_ref[1:2], lc_ref[1:2], om_ref[1:2], st_s, 1, ob_ref)

    @pl.when(i == pl.num_programs(1) - 1)
    def _():
        for d in range(2):
            for h in range(HA):
                sfin_ref[d, h] = st_s[d, h].T


def hgrn_scan(q, f_fwd, f_bwd, v, lb, s0):
    b, t, w = q.shape
    n = t // CHUNK
    lb = lb.astype(F32)
    la = jnp.log(jnp.maximum(lb, LB_FLOOR))
    lc = jnp.log1p(-lb)
    om = 1.0 - lb
    fwd = pl.BlockSpec((None, CHUNK, w), lambda bi, i: (bi, i, 0))
    bwd = pl.BlockSpec((None, CHUNK, w), lambda bi, i: (bi, n - 1 - i, 0))
    par = pl.BlockSpec((2, w), lambda bi, i: (0, 0))
    st = pl.BlockSpec((None, 2, HA, DKA, DVA), lambda bi, i: (bi, 0, 0, 0, 0))
    return pl.pallas_call(
        _hgrn_body,
        grid=(b, n),
        in_specs=[fwd, fwd, fwd, bwd, bwd, bwd, par, par, par, st],
        out_specs=[fwd, bwd, st],
        out_shape=[jax.ShapeDtypeStruct((b, t, w), F32), jax.ShapeDtypeStruct((b, t, w), F32),
                   jax.ShapeDtypeStruct((b, 2, HA, DKA, DVA), F32)],
        scratch_shapes=[pltpu.VMEM((2, HA, DVA, DKA), F32)],
        compiler_params=pltpu.CompilerParams(dimension_semantics=("parallel", "arbitrary"),
                                             vmem_limit_bytes=VMEM_LIMIT),
        name="hgrn_scan",
    )(q, f_fwd, v, q, f_bwd, v, la, lc, om, s0)


def _hgrn_combine_body(of_ref, ob_ref, g_ref, w_ref, o_ref):
    y = of_ref[...] + ob_ref[...]
    g = g_ref[...]
    outs = []
    for h in range(HA):
        cs = slice(h * DVA, (h + 1) * DVA)
        yh = y[:, cs]
        yn = yh * lax.rsqrt(jnp.mean(yh * yh, axis=-1, keepdims=True) + EPS)
        outs.append(yn * w_ref[:, cs] * jax.nn.silu(g[:, cs]))
    o_ref[...] = jnp.concatenate(outs, axis=-1)


def hgrn_combine(o_f, o_b, g_out, norm_w):
    b, t, w = o_f.shape
    rows = 256
    blk = pl.BlockSpec((rows, w), lambda i: (i, 0))
    out = pl.pallas_call(
        _hgrn_combine_body,
        grid=(b * t // rows,),
        in_specs=[blk, blk, blk, pl.BlockSpec((1, w), lambda i: (0, 0))],
        out_specs=blk,
        out_shape=jax.ShapeDtypeStruct((b * t, w), F32),
        compiler_params=pltpu.CompilerParams(dimension_semantics=("parallel",)),
        name="hgrn_combine",
    )(o_f.reshape(b * t, w), o_b.reshape(b * t, w), g_out.reshape(b * t, w), norm_w.reshape(1, w))
    return out.reshape(b, t, w)


def hgrn2_mixer(parts, lb, norm_w, init_state):
    q, f_fwd, f_bwd, i_in, g_out = parts
    b = q.shape[0]
    s0 = jnp.zeros((b, 2, HA, DKA, DVA), F32) if init_state is None else init_state.astype(F32)
    o_f, o_b, finals = hgrn_scan(q, f_fwd, f_bwd, i_in, lb, s0)
    return hgrn_combine(o_f, o_b, g_out, norm_w), finals


def ssd_mixer(parts, conv_w, conv_b, dt_bias, a_log, d_skip, norm_w, init_state):
    x, z, b_in, c_in, dt_fwd, dt_bwd = parts
    b, t, _ = x.shape
    xbc = jax.nn.silu(conv_centred(jnp.concatenate([x, b_in, c_in], axis=-1), conv_w, conv_b))
    xs, bs, cs = split_cols(xbc, (GROUP_W, GC * NC, GC * NC))
    xh = xs.astype(F32).reshape(b, t, HC, PC)
    bg = bs.astype(F32).reshape(b, t, GC, NC)
    cg = cs.astype(F32).reshape(b, t, GC, NC)
    a = -jnp.exp(a_log.astype(F32))
    y = d_skip.astype(F32)[:, None] * xh
    finals = []
    for d, dt_raw in enumerate((dt_fwd, dt_bwd)):
        dt = jax.nn.softplus(dt_raw.astype(F32) + dt_bias[d].astype(F32))
        if init_state is None:
            s0 = jnp.zeros((b, HC, PC, NC), F32)
        else:
            s0 = init_state[:, d].astype(F32)
        if d == 0:
            yd, sf = ssd_chunk_scan(xh, dt, a[d], bg, cg, s0)
        else:
            yd, sf = ssd_chunk_scan(flip_t(xh), flip_t(dt), a[d], flip_t(bg), flip_t(cg), s0)
            yd = flip_t(yd)
        y = y + yd
        finals.append(sf)
    y = y.reshape(b, t, GROUP_W) * jax.nn.silu(z.astype(F32))
    return rmsnorm(y, norm_w).astype(x.dtype), jnp.stack(finals, axis=1)


def mla_project(parts, p):
    cq, ckv, krope = parts
    b, t, _ = cq.shape
    q = (rmsnorm(cq, p['mla_q_norm_w']) @ p['mla_w_q_up']).reshape(b, t, HB, NOPE + ROPE)
    return q[..., :NOPE], q[..., NOPE:], rmsnorm(ckv, p['mla_kv_norm_w']), krope


def mla_keys_values(ckv, krope, w_kv_up):
    b, t, _ = ckv.shape
    kv = (ckv @ w_kv_up).reshape(b, t, HB, NOPE + VB)
    k = jnp.concatenate([kv[..., :NOPE], jnp.broadcast_to(krope[:, :, None, :], (b, t, HB, ROPE))], axis=-1)
    return k, kv[..., NOPE:]


def mla_context(parts, p):
    q_nope, q_rope, ckv, krope = mla_project(parts, p)
    k, v = mla_keys_values(ckv, krope, p['mla_w_kv_up'])
    o = block_attention(jnp.concatenate([q_nope, q_rope], axis=-1), k, v, MLA_SCALE)
    b, t = o.shape[:2]
    return o.reshape(b, t, HB * VB), ckv, krope


def mla_latent(parts, p, ckv_ctx, krope_ctx):
    q_nope, q_rope, ckv, krope = mla_project(parts, p)
    q = jnp.concatenate([q_nope, rope_2d(q_rope)], axis=-1)
    k_lat, v_lat = mla_keys_values(ckv, rope_2d(krope), p['mla_w_kv_up'])
    k_ctx, v_ctx = mla_keys_values(ckv_ctx.astype(ckv.dtype), krope_ctx.astype(krope.dtype), p['mla_w_kv_up'])
    o = block_attention(q, jnp.concatenate([k_lat, k_ctx], axis=1), jnp.concatenate([v_lat, v_ctx], axis=1),
                        MLA_SCALE)
    b, t = o.shape[:2]
    return o.reshape(b, t, HB * VB)


def natten_context(parts):
    b, t, _ = parts[0].shape
    q, k, v = [a.reshape(b, t, HD, DHD) for a in parts]
    o = block_attention(q, k, v, DHD ** -0.5)
    return o.reshape(b, t, GROUP_W), k, v


BF16 = jnp.bfloat16
LANES = 128
VMEM_LIMIT = 56 * 1024 * 1024
NEG_BIG = -1e30

NA_ROWS = DEC_SEQ // GRID_W
NA_WR = min(WIN_R, NA_ROWS)
NA_WIN = NA_WR * GRID_W


def _dot_nt(a, b):
    return lax.dot_general(a, b, (((1,), (1,)), ((), ())), preferred_element_type=F32)


def _dot_tn(a, b):
    return lax.dot_general(a, b, (((0,), (0,)), ((), ())), preferred_element_type=F32)


def natten_bias_table(rpb):
    cols = jnp.arange(GRID_W)
    start = jnp.clip(cols - WIN_C // 2, 0, GRID_W - WIN_C)
    in_win = (cols[None, :] >= start[:, None]) & (cols[None, :] < start[:, None] + WIN_C)
    c_off = jnp.clip(cols[None, :] - cols[:, None] + (WIN_C - 1), 0, 2 * WIN_C - 2)
    r_off = jnp.arange(NA_WR)[:, None] - (NA_WR - 1) + jnp.arange(NA_WR)[None, :] + (WIN_R - 1)
    tab = rpb.astype(F32)[:, r_off][:, :, :, c_off]
    tab = jnp.where(in_win[None, None, None], tab, NEG_BIG)
    return tab.transpose(0, 1, 3, 2, 4).reshape(HD, NA_WR, GRID_W, NA_WIN)


def _natten_body(q_ref, k_ref, v_ref, kc_ref, vc_ref, bias_ref, o_ref):
    r = pl.program_id(1)
    rs = jnp.clip(r - NA_WR // 2, 0, NA_ROWS - NA_WR)
    win = pl.ds(pl.multiple_of(rs * GRID_W, GRID_W), NA_WIN)
    q = q_ref[...] * (DHD ** -0.5)
    outs = []
    for h in range(HD):
        cs = slice(h * DHD, (h + 1) * DHD)
        qh = q[:, cs].astype(BF16)
        s_loc = _dot_nt(qh, k_ref[win, cs].astype(BF16)) + bias_ref[h]
        s_ctx = _dot_nt(qh, kc_ref[:, cs].astype(BF16))
        m = jnp.maximum(jnp.max(s_loc, axis=-1, keepdims=True), jnp.max(s_ctx, axis=-1, keepdims=True))
        p_loc = jnp.exp(s_loc - m)
        p_ctx = jnp.exp(s_ctx - m)
        inv = 1.0 / (jnp.sum(p_loc, axis=-1, keepdims=True) + jnp.sum(p_ctx, axis=-1, keepdims=True))
        outs.append(jnp.dot((p_loc * inv).astype(BF16), v_ref[win, cs].astype(BF16), preferred_element_type=F32)
                    + jnp.dot((p_ctx * inv).astype(BF16), vc_ref[:, cs].astype(BF16),
                              preferred_element_type=F32))
    o_ref[...] = jnp.concatenate(outs, axis=-1)


def natten_latent(parts, rpb, k_ctx, v_ctx):
    q, k, v = parts
    b, t, w = q.shape
    assert t == DEC_SEQ and w == GROUP_W
    n_ctx = k_ctx.shape[1]
    bias = natten_bias_table(rpb)

    def delta_idx(r):
        return jnp.clip(r - NA_WR // 2, 0, NA_ROWS - NA_WR) - r + (NA_WR - 1)
    full = pl.BlockSpec((None, t, w), lambda bi, r: (bi, 0, 0))
    ctx = pl.BlockSpec((None, n_ctx, w), lambda bi, r: (bi, 0, 0))
    return pl.pallas_call(
        _natten_body,
        grid=(b, NA_ROWS),
        in_specs=[pl.BlockSpec((None, GRID_W, w), lambda bi, r: (bi, r, 0)), full, full, ctx, ctx,
                  pl.BlockSpec((HD, None, GRID_W, NA_WIN), lambda bi, r: (0, delta_idx(r), 0, 0))],
        out_specs=pl.BlockSpec((None, GRID_W, w), lambda bi, r: (bi, r, 0)),
        out_shape=jax.ShapeDtypeStruct((b, t, w), F32),
        compiler_params=pltpu.CompilerParams(dimension_semantics=("parallel", "arbitrary"),
                                             vmem_limit_bytes=VMEM_LIMIT),
        name="natten_latent",
    )(q, k, v, k_ctx.reshape(b, n_ctx, w), v_ctx.reshape(b, n_ctx, w), bias)


PEER_KDIM = PEER_QDIM // 2
PEER_HP = PEER_HEADS * 2
PEER_PREP_TE = 512
PEER_SCORE_TM = 512
PEER_SELECT_TM = 256
PEER_TM = 512
PEER_TE = 1024
PEER_ROWS = PEER_TE // N_KEYS
PEER_CAND = [(a, b) for a in range(PEER_TOPK) for b in range(PEER_TOPK) if (a + 1) * (b + 1) <= PEER_TOPK]


def _peer_prep_body(u_ref, v_ref, ub_ref, vt_ref):
    ub_ref[...] = u_ref[...].astype(BF16)
    vt_ref[...] = v_ref[...].T.astype(BF16)


def peer_prep_tables(peer_u, peer_v):
    depth, n_e, d = peer_u.shape
    te = PEER_PREP_TE
    return pl.pallas_call(
        _peer_prep_body,
        grid=(depth, n_e // te),
        in_specs=[pl.BlockSpec((None, te, d), lambda l, j: (l, j, 0)),
                  pl.BlockSpec((None, te, d), lambda l, j: (l, j, 0))],
        out_specs=[pl.BlockSpec((None, te, d), lambda l, j: (l, j, 0)),
                   pl.BlockSpec((None, d, te), lambda l, j: (l, 0, j))],
        out_shape=[jax.ShapeDtypeStruct((depth, n_e, d), BF16), jax.ShapeDtypeStruct((depth, d, n_e), BF16)],
        compiler_params=pltpu.CompilerParams(dimension_semantics=("parallel", "parallel"),
                                             vmem_limit_bytes=VMEM_LIMIT),
        name="peer_prep_tables",
    )(peer_u, peer_v)


def _peer_score_body(h_ref, wqt_ref, sk_ref, xt_ref, st_ref):
    xt = h_ref[...].T.astype(BF16)
    xt_ref[...] = xt
    qt = jnp.dot(wqt_ref[...], xt, preferred_element_type=F32).astype(BF16)
    for hp in range(PEER_HP):
        st_ref[hp] = jnp.dot(sk_ref[hp % 2], qt[hp * PEER_KDIM:(hp + 1) * PEER_KDIM, :],
                             preferred_element_type=F32)


def peer_scores(hf, wq_t, sub_keys):
    n, d = hf.shape
    tm = PEER_SCORE_TM
    return pl.pallas_call(
        _peer_score_body,
        grid=(n // tm,),
        in_specs=[pl.BlockSpec((tm, d), lambda i: (i, 0)),
                  pl.BlockSpec(wq_t.shape, lambda i: (0, 0)),
                  pl.BlockSpec(sub_keys.shape, lambda i: (0, 0, 0))],
        out_specs=[pl.BlockSpec((d, tm), lambda i: (0, i)),
                   pl.BlockSpec((PEER_HP, N_KEYS, tm), lambda i: (0, 0, i))],
        out_shape=[jax.ShapeDtypeStruct((d, n), BF16), jax.ShapeDtypeStruct((PEER_HP, N_KEYS, n), F32)],
        compiler_params=pltpu.CompilerParams(dimension_semantics=("parallel",), vmem_limit_bytes=VMEM_LIMIT),
        name="peer_scores",
    )(hf, wq_t, sub_keys)


def _peer_select_body(st_ref, e1_ref, cnt1_ref, rank2_ref, e2_ref, rank1_s, vtop_s, cnt_s, zinv_s):
    tm = st_ref.shape[-1]
    kio = lax.broadcasted_iota(jnp.int32, (N_KEYS, LANES), 0).astype(F32)
    neg = jnp.float32(-jnp.inf)

    def group(g, carry):
        ln = pl.ds(pl.multiple_of(g * LANES, LANES), LANES)

        for hp in range(PEER_HP):
            h, part = hp // 2, hp % 2

            def extract(it, sr):
                s, rank = sr
                m = jnp.max(s, axis=0, keepdims=True)
                first = jnp.min(jnp.where(s == m, kio, float(N_KEYS)), axis=0, keepdims=True)
                sel = kio == first
                vtop_s[part, it, h:h + 1, ln] = m
                return jnp.where(sel, neg, s), jnp.where(sel, it, rank)
            _, rank = lax.fori_loop(0, PEER_TOPK, extract,
                                    (st_ref[hp, :, ln], jnp.full((N_KEYS, LANES), PEER_TOPK, jnp.int32)))
            rank = rank.astype(F32)
            if part == 0:
                rank1_s[h, :, ln] = rank
            else:
                rank2_ref[h, :, ln] = rank

        v1 = [vtop_s[0, a, :, ln] for a in range(PEER_TOPK)]
        v2 = [vtop_s[1, b, :, ln] for b in range(PEER_TOPK)]
        sums = [v1[a] + v2[b] for a, b in PEER_CAND]
        n_c = len(PEER_CAND)
        beaten = [jnp.zeros((PEER_HEADS, LANES), F32) for _ in range(n_c)]
        for i in range(n_c):
            for j in range(i + 1, n_c):
                ge = sums[i] >= sums[j]
                beaten[j] = beaten[j] + jnp.where(ge, 1.0, 0.0)
                beaten[i] = beaten[i] + jnp.where(ge, 0.0, 1.0)
        z = jnp.zeros((PEER_HEADS, LANES), F32)
        cnt = [jnp.zeros((PEER_HEADS, LANES), F32) for _ in range(PEER_TOPK)]
        for i, (a, b) in enumerate(PEER_CAND):
            keep = beaten[i] < float(PEER_TOPK)
            z = z + jnp.where(keep, jnp.exp(sums[i] - sums[0]), 0.0)
            cnt[a] = cnt[a] + jnp.where(keep, 1.0, 0.0)
        for a in range(PEER_TOPK):
            cnt_s[a, :, ln] = cnt[a]
        zinv_s[:, ln] = 1.0 / z

        for h in range(PEER_HEADS):
            rank1 = rank1_s[h, :, ln]
            s1 = st_ref[2 * h, :, ln]
            e1 = jnp.exp(s1 - vtop_s[0, 0, h:h + 1, ln]) * zinv_s[h:h + 1, ln]
            e1_ref[h, :, ln] = jnp.where(rank1 < float(PEER_TOPK), e1, 0.0)
            c1 = jnp.zeros((N_KEYS, LANES), F32)
            for a in range(PEER_TOPK):
                c1 = c1 + jnp.where(rank1 == float(a), cnt_s[a, h:h + 1, ln], 0.0)
            cnt1_ref[h, :, ln] = c1
            e2_ref[h, :, ln] = jnp.exp(st_ref[2 * h + 1, :, ln] - vtop_s[1, 0, h:h + 1, ln])
        return carry
    lax.fori_loop(0, tm // LANES, group, 0)


def peer_select(st):
    _, _, n = st.shape
    tm = PEER_SELECT_TM
    spec = pl.BlockSpec((PEER_HEADS, N_KEYS, tm), lambda i: (0, 0, i))
    shp = jax.ShapeDtypeStruct((PEER_HEADS, N_KEYS, n), F32)
    return pl.pallas_call(
        _peer_select_body,
        grid=(n // tm,),
        in_specs=[pl.BlockSpec((PEER_HP, N_KEYS, tm), lambda i: (0, 0, i))],
        out_specs=[spec, spec, spec, spec],
        out_shape=[shp, shp, shp, shp],
        scratch_shapes=[pltpu.VMEM((PEER_HEADS, N_KEYS, tm), F32),
                        pltpu.VMEM((2, PEER_TOPK, PEER_HEADS, tm), F32),
                        pltpu.VMEM((PEER_TOPK, PEER_HEADS, tm), F32),
                        pltpu.VMEM((PEER_HEADS, tm), F32)],
        compiler_params=pltpu.CompilerParams(dimension_semantics=("parallel",), vmem_limit_bytes=VMEM_LIMIT),
        name="peer_select",
    )(st)


def _gelu_tanh(x):
    return 0.5 * x * (1.0 + jnp.tanh(math.sqrt(2.0 / math.pi) * (x + 0.044715 * (x * x * x))))


def _peer_expert_body(u_ref, vt_ref, xt_ref, e1_ref, cnt1_ref, rank2_ref, e2_ref, o_ref, st_s, at_s):
    j = pl.program_id(1)
    tm = xt_ref.shape[-1]

    @pl.when(j == 0)
    def _():
        o_ref[...] = jnp.zeros_like(o_ref)

    st_s[...] = jnp.dot(u_ref[...], xt_ref[...], preferred_element_type=F32)

    def group(g, carry):
        ln = pl.ds(pl.multiple_of(g * LANES, LANES), LANES)
        for r in range(PEER_ROWS):
            gate = jnp.zeros((N_KEYS, LANES), F32)
            for h in range(PEER_HEADS):
                keep = rank2_ref[h, :, ln] < cnt1_ref[h, r:r + 1, ln]
                gate = gate + jnp.where(keep, e2_ref[h, :, ln], 0.0) * e1_ref[h, r:r + 1, ln]
            rows = slice(r * N_KEYS, (r + 1) * N_KEYS)
            at_s[rows, ln] = (_gelu_tanh(st_s[rows, ln]) * gate).astype(BF16)
        return carry
    lax.fori_loop(0, tm // LANES, group, 0)

    o_ref[...] += jnp.dot(vt_ref[...], at_s[...], preferred_element_type=F32)


def peer_experts(ub, vt, xt, e1, cnt1, rank2, e2):
    n_e, d = ub.shape
    n = xt.shape[1]
    tm, te = PEER_TM, PEER_TE
    row_spec = pl.BlockSpec((PEER_HEADS, PEER_ROWS, tm), lambda i, j: (0, j, i))
    col_spec = pl.BlockSpec((PEER_HEADS, N_KEYS, tm), lambda i, j: (0, 0, i))
    return pl.pallas_call(
        _peer_expert_body,
        grid=(n // tm, n_e // te),
        in_specs=[pl.BlockSpec((te, d), lambda i, j: (j, 0)),
                  pl.BlockSpec((d, te), lambda i, j: (0, j)),
                  pl.BlockSpec((d, tm), lambda i, j: (0, i)),
                  row_spec, row_spec, col_spec, col_spec],
        out_specs=pl.BlockSpec((d, tm), lambda i, j: (0, i)),
        out_shape=jax.ShapeDtypeStruct((d, n), F32),
        scratch_shapes=[pltpu.VMEM((te, tm), F32), pltpu.VMEM((te, tm), BF16)],
        compiler_params=pltpu.CompilerParams(dimension_semantics=("parallel", "arbitrary"),
                                             vmem_limit_bytes=VMEM_LIMIT),
        name="peer_experts",
    )(ub, vt, xt, e1, cnt1, rank2, e2)


def peer_ffn(hf, wq_t, sub_keys_b, ub, vt):
    xt, st = peer_scores(hf, wq_t, sub_keys_b)
    e1, cnt1, rank2, e2 = peer_select(st)
    return peer_experts(ub, vt, xt, e1, cnt1, rank2, e2).T


def trunk_layer(x, cond, p, cache):
    mod = jax.nn.silu(cond) @ p['w_ada'] + p['b_ada']
    if cond.ndim == 2:
        mod = mod[:, None, :]
    sh1, sc1, g1, sh2, sc2, g2 = jnp.split(mod, 6, axis=-1)
    h = rmsnorm(x, p['norm1_w']) * (1.0 + sc1) + sh1
    parts = split_cols(h @ p['w_in'], IN_SPLITS)
    pa, pb, pc, pd = parts[0:5], parts[5:8], parts[8:14], parts[14:17]
    latent = cache is not None
    o_a, st_a = hgrn2_mixer(pa, p['lb'], p['hgrn_norm_w'], cache[0] if latent else None)
    o_c, st_c = ssd_mixer(pc, p['ssd_conv_w'], p['ssd_conv_b'], p['ssd_dt_bias'], p['ssd_a_log'], p['ssd_d'],
                          p['ssd_norm_w'], cache[1] if latent else None)
    if latent:
        o_b = mla_latent(pb, p, cache[2], cache[3])
        o_d = natten_latent(pd, p['na_rpb'], cache[4], cache[5])
        new_state = None
    else:
        o_b, ckv, krope = mla_context(pb, p)
        o_d, k_na, v_na = natten_context(pd)
        new_state = (st_a, st_c, ckv, krope, k_na, v_na)
    mix = jnp.concatenate([o_a, o_b, o_c, o_d], axis=-1) @ p['w_out']
    x = x + g1 * mix
    h2 = rmsnorm(x, p['norm2_w']) * (1.0 + sc2) + sh2
    return x, h2, jnp.broadcast_to(g2, x.shape), new_state


def _final_norm_body(x_ref, w_ref, o_ref):
    x = x_ref[...]
    y = x * lax.rsqrt(jnp.mean(x * x, axis=-1, keepdims=True) + EPS)
    o_ref[...] = y * w_ref[...]


def final_rmsnorm(x, w):
    b, t, d = x.shape
    rows = 512
    out = pl.pallas_call(
        _final_norm_body,
        grid=(b * t // rows,),
        in_specs=[pl.BlockSpec((rows, d), lambda i: (i, 0)), pl.BlockSpec((1, d), lambda i: (0, 0))],
        out_specs=pl.BlockSpec((rows, d), lambda i: (i, 0)),
        out_shape=jax.ShapeDtypeStruct((b * t, d), x.dtype),
        name="final_rmsnorm",
    )(x.reshape(b * t, d), w.reshape(1, d))
    return out.reshape(b, t, d)


def kernel(x_prompt, x_sample, c, state_hgrn, state_ssd, cache_mla_ckv, cache_mla_krope, cache_na_k,
           cache_na_v, c_ctx, w_ada, b_ada, norm1_w, norm2_w, w_in, w_out, hgrn_lb_logits, hgrn_norm_w,
           mla_q_norm_w, mla_w_q_up, mla_kv_norm_w, mla_w_kv_up, ssd_conv_w, ssd_conv_b, ssd_dt_bias,
           ssd_a_log, ssd_d, ssd_norm_w, na_rpb, peer_w_q, peer_sub_keys, peer_u, peer_v, final_norm_w):
    lb_soft = jax.nn.softmax(hgrn_lb_logits.astype(F32), axis=0)
    lb_all = jnp.cumsum(lb_soft, axis=0) - lb_soft[0]
    stacked = {'w_ada': w_ada, 'b_ada': b_ada, 'norm1_w': norm1_w, 'norm2_w': norm2_w, 'w_in': w_in,
               'w_out': w_out, 'hgrn_norm_w': hgrn_norm_w, 'mla_q_norm_w': mla_q_norm_w,
               'mla_w_q_up': mla_w_q_up, 'mla_kv_norm_w': mla_kv_norm_w, 'mla_w_kv_up': mla_w_kv_up,
               'ssd_conv_w': ssd_conv_w, 'ssd_conv_b': ssd_conv_b, 'ssd_dt_bias': ssd_dt_bias,
               'ssd_a_log': ssd_a_log, 'ssd_d': ssd_d, 'ssd_norm_w': ssd_norm_w, 'na_rpb': na_rpb,
               'peer_w_q': peer_w_q, 'peer_sub_keys': peer_sub_keys, 'peer_u': peer_u, 'peer_v': peer_v}

    ub_all, vt_all = peer_prep_tables(peer_u, peer_v)
    n_ctx = BATCH * SEQ

    xp, xs = x_prompt, x_sample
    ctx_states = []
    for l in range(DEPTH):
        p = {name: arr[l] for name, arr in stacked.items()}
        p['lb'] = lb_all[l]
        xp, h2p, g2p, st = trunk_layer(xp, c_ctx, p, None)
        ctx_states.append(st)
        cache_l = (state_hgrn[:, l], state_ssd[:, l], cache_mla_ckv[:, l], cache_mla_krope[:, l],
                   cache_na_k[:, l], cache_na_v[:, l])
        xs, h2s, g2s, _ = trunk_layer(xs, c, p, cache_l)
        hf = jnp.concatenate([h2p.reshape(n_ctx, D_MODEL), h2s.reshape(-1, D_MODEL)], axis=0)
        ff = peer_ffn(hf, peer_w_q[l].T.astype(BF16), peer_sub_keys[l].astype(BF16), ub_all[l], vt_all[l])
        xp = xp + g2p * ff[:n_ctx].reshape(xp.shape)
        xs = xs + g2s * ff[n_ctx:].reshape(xs.shape)

    y_prompt = final_rmsnorm(xp, final_norm_w)
    y_sample = final_rmsnorm(xs, final_norm_w)
    new_state_hgrn = jnp.stack([s[0] for s in ctx_states], axis=1)
    new_state_ssd = jnp.stack([s[1] for s in ctx_states], axis=1)
    new_cache_mla_ckv = jnp.stack([s[2] for s in ctx_states], axis=1)
    new_cache_mla_krope = jnp.stack([s[3] for s in ctx_states], axis=1)
    new_cache_na_k = jnp.stack([s[4] for s in ctx_states], axis=1)
    new_cache_na_v = jnp.stack([s[5] for s in ctx_states], axis=1)
    return (y_prompt, y_sample, new_state_hgrn, new_state_ssd, new_cache_mla_ckv, new_cache_mla_krope,
            new_cache_na_k, new_cache_na_v)
```

```python
import functools
import math
import jax
import jax.numpy as jnp
from jax import lax
import numpy as np
from jax.experimental import pallas as pl
from jax.experimental.pallas import tpu as pltpu

D_MODEL = 2048
BATCH = 32
SEQ = 256
DEPTH = 2
DEC_BATCH = 4
DEC_SEQ = 1024
PAST_LEN = 256

GRID_W = 64
EPS = 1e-6
ROPE_THETA = 10000.0
Q_BLOCK = 128
CHUNK = 64
N_MIXERS = 4
GROUP_W = D_MODEL // N_MIXERS
D_MIX = N_MIXERS * GROUP_W
HA = 4
DKA = GROUP_W // HA
DVA = GROUP_W // HA
LB_FLOOR = 1e-30
HB = 4
Q_RANK = D_MODEL // 4
KV_RANK = D_MODEL // 8
NOPE = 128
ROPE = 64
VB = GROUP_W // HB
MLA_SCALE = (NOPE + ROPE) ** -0.5
HC = 8
PC = GROUP_W // HC
NC = 128
GC = 2
CONV_W = 3
CONV_CH = GROUP_W + 2 * GC * NC
HD = 8
DHD = GROUP_W // HD
WIN_R = 8
WIN_C = 16
N_KEYS = 128
N_EXPERTS = N_KEYS * N_KEYS
PEER_HEADS = 8
PEER_QDIM = 256
PEER_TOPK = 16
PEER_TOKEN_BLOCK = 128

IN_SPLITS = (HA * DKA, HA * DKA, HA * DKA, HA * DVA, HA * DVA,
             Q_RANK, KV_RANK, ROPE,
             GROUP_W, GROUP_W, GC * NC, GC * NC, HC, HC,
             GROUP_W, GROUP_W, GROUP_W)
IN_W = sum(IN_SPLITS)
F32 = jnp.float32


def rmsnorm(x, w):
    xf = x.astype(F32)
    y = xf * lax.rsqrt(jnp.mean(xf * xf, axis=-1, keepdims=True) + EPS)
    return (y * w.astype(F32)).astype(x.dtype)


def split_cols(a, sizes):
    offs = []
    acc = 0
    for s in sizes[:-1]:
        acc += s
        offs.append(acc)
    return jnp.split(a, offs, axis=-1)


def flip_t(a):
    return jnp.flip(a, axis=1)


def conv_centred(x, w, b):
    ch = x.shape[-1]
    k = w.shape[0]
    y = lax.conv_general_dilated(x, w[:, None, :].astype(x.dtype), window_strides=(1,),
                                 padding=[(k // 2, k // 2)], dimension_numbers=('NWC', 'WIO', 'NWC'),
                                 feature_group_count=ch)
    return y + b.astype(x.dtype)


def rope_2d(x):
    t_len = x.shape[1]
    t = jnp.arange(t_len)
    half = ROPE // 2
    inv = 1.0 / (ROPE_THETA ** (jnp.arange(0, half, 2, dtype=F32) / half))
    rows = (t // GRID_W).astype(F32)
    cols = (t % GRID_W).astype(F32)

    def rot(xa, pos):
        ang = (pos[:, None] * inv[None, :]).reshape((t_len,) + (1,) * (x.ndim - 3) + (inv.shape[0],))
        cs, sn = jnp.cos(ang).astype(x.dtype), jnp.sin(ang).astype(x.dtype)
        x1, x2 = jnp.split(xa, 2, axis=-1)
        return jnp.concatenate([x1 * cs - x2 * sn, x2 * cs + x1 * sn], axis=-1)
    return jnp.concatenate([rot(x[..., :half], rows), rot(x[..., half:], cols)], axis=-1)


def block_attention(q, k, v, scale):
    b, tq, h, dq = q.shape
    nb = tq // Q_BLOCK
    qb = q.reshape(b, nb, Q_BLOCK, h, dq).transpose(1, 0, 2, 3, 4)

    def one(qi):
        s = jnp.einsum('bqhd,bkhd->bhqk', qi, k).astype(F32) * scale
        pr = jax.nn.softmax(s, axis=-1).astype(v.dtype)
        return jnp.einsum('bhqk,bkhd->bqhd', pr, v)
    o = lax.map(one, qb)
    return o.transpose(1, 0, 2, 3, 4).reshape(b, tq, h, v.shape[-1])


def masked_exp(mask, diff):
    return jnp.where(mask, jnp.exp(jnp.where(mask, diff, 0.0)), 0.0)


def gla_chunk_scan(q, k, v, logf, s0):
    b, t, h, dk = q.shape
    dv = v.shape[-1]
    n = t // CHUNK

    def chunks(a):
        return a.reshape(b, n, CHUNK, h, a.shape[-1]).transpose(1, 0, 3, 2, 4)
    cum = jnp.cumsum(chunks(logf), axis=3)
    mask = jnp.tril(jnp.ones((CHUNK, CHUNK), dtype=bool))[:, :, None]

    def step(state, inp):
        qi, ki, vi, ci = inp
        decay = masked_exp(mask, ci[:, :, :, None, :] - ci[:, :, None, :, :])
        att = jnp.einsum('bhtk,bhsk,bhtsk->bhts', qi, ki, decay)
        o = jnp.einsum('bhts,bhsv->bhtv', att, vi)
        o = o + jnp.einsum('bhtk,bhkv->bhtv', qi * jnp.exp(ci), state)
        last = ci[:, :, -1]
        state = jnp.exp(last)[..., None] * state + jnp.einsum(
            'bhsk,bhsv->bhkv', ki * jnp.exp(last[:, :, None, :] - ci), vi)
        return state, o
    s_fin, o = lax.scan(step, s0, (chunks(q), chunks(k), chunks(v), cum))
    return o.transpose(1, 0, 3, 2, 4).reshape(b, t, h, dv), s_fin


def ssd_chunk_scan(x, dt, a, bm, cm, s0):
    b, t, h, p = x.shape
    g, n_st = bm.shape[2], bm.shape[3]
    r = h // g
    n = t // CHUNK
    xdt = (x * dt[..., None]).reshape(b, n, CHUNK, g, r, p).transpose(1, 0, 2, 3, 4, 5)
    la = (dt * a).reshape(b, n, CHUNK, g, r).transpose(1, 0, 2, 3, 4)
    bc = bm.reshape(b, n, CHUNK, g, n_st).transpose(1, 0, 2, 3, 4)
    cc = cm.reshape(b, n, CHUNK, g, n_st).transpose(1, 0, 2, 3, 4)
    mask = jnp.tril(jnp.ones((CHUNK, CHUNK), dtype=bool))[None, :, :, None, None]

    def step(state, inp):
        xi, ai, bi, ci = inp
        cum = jnp.cumsum(ai, axis=1)
        lmat = masked_exp(mask, cum[:, :, None] - cum[:, None, :])
        cb = jnp.einsum('btgn,bsgn->btsg', ci, bi)
        y = jnp.einsum('btsg,btsgr,bsgrp->btgrp', cb, lmat, xi)
        y = y + jnp.einsum('btgn,bgrpn->btgrp', ci, state) * jnp.exp(cum)[..., None]
        last = cum[:, -1]
        w = jnp.exp(last[:, None] - cum)
        state = jnp.exp(last)[..., None, None] * state + jnp.einsum('bsgn,bsgr,bsgrp->bgrpn', bi, w, xi)
        return state, y
    s_fin, y = lax.scan(step, s0.reshape(b, g, r, p, n_st), (xdt, la, bc, cc))
    return y.transpose(1, 0, 2, 3, 4, 5).reshape(b, t, h, p), s_fin.reshape(b, h, p, n_st)


HG_SUB = 16
HG_NSUB = CHUNK // HG_SUB


def _split3_bf16(x):
    hi = x.astype(BF16)
    r1 = x - hi.astype(F32)
    mid = r1.astype(BF16)
    lo = (r1 - mid.astype(F32)).astype(BF16)
    return hi, mid, lo


def _hgrn_direction(rev, q_ref, f_ref, v_ref, la, lc, om, st_s, d, o_ref):
    fx = f_ref[...]
    logf = jnp.logaddexp(la, lc + jax.nn.log_sigmoid(fx))
    kk = om * jax.nn.sigmoid(-fx)
    qq = jax.nn.silu(q_ref[...]) * (DKA ** -0.5)
    vv = v_ref[...]
    t_io = lax.broadcasted_iota(jnp.int32, (CHUNK, CHUNK), 0)
    s_io = lax.broadcasted_iota(jnp.int32, (CHUNK, CHUNK), 1)
    tri = jnp.where((s_io >= t_io) if rev else (s_io <= t_io), 1.0, 0.0).astype(BF16)
    c = sum(jnp.dot(tri, part, preferred_element_type=F32) for part in _split3_bf16(logf))
    row = lax.broadcasted_iota(jnp.int32, (CHUNK, 1), 0)
    sub_row = lax.broadcasted_iota(jnp.int32, (HG_SUB, 1), 0)
    lane = lax.broadcasted_iota(jnp.int32, (HG_SUB, CHUNK), 1)
    outs = []
    for h in range(HA):
        cs = slice(h * DKA, (h + 1) * DKA)
        ch, qh, kh, vh = c[:, cs], qq[:, cs], kk[:, cs], vv[:, cs]
        st = st_s[d, h]
        o = _dot_nt((qh * jnp.exp(ch)).astype(BF16), st.astype(BF16))
        att = jnp.zeros((CHUNK, CHUNK), F32)
        for i in range(HG_NSUB):
            if rev:
                if i == HG_NSUB - 1:
                    continue
                c_ref = ch[(i + 1) * HG_SUB:(i + 1) * HG_SUB + 1]
                k_side = row >= (i + 1) * HG_SUB
            else:
                if i == 0:
                    continue
                c_ref = ch[i * HG_SUB - 1:i * HG_SUB]
                k_side = row < i * HG_SUB
            q_side = (row >= i * HG_SUB) & (row < (i + 1) * HG_SUB)
            qs = jnp.where(q_side, qh * jnp.exp(jnp.where(q_side, ch - c_ref, 0.0)), 0.0)
            ks = jnp.where(k_side, kh * jnp.exp(jnp.where(k_side, c_ref - ch, 0.0)), 0.0)
            att = att + _dot_nt(qs.astype(BF16), ks.astype(BF16))
        strips = []
        for i in range(HG_NSUB):
            blk = slice(i * HG_SUB, (i + 1) * HG_SUB)
            cb, qb, kb = ch[blk], qh[blk], kh[blk]
            strip = jnp.zeros((HG_SUB, CHUNK), F32)
            for s in range(HG_SUB):
                causal = (sub_row <= s) if rev else (sub_row >= s)
                w = jnp.exp(jnp.where(causal, cb - cb[s:s + 1], 0.0))
                col = jnp.sum(jnp.where(causal, w * qb * kb[s:s + 1], 0.0), axis=-1, keepdims=True)
                strip = jnp.where(lane == i * HG_SUB + s, col, strip)
            strips.append(strip)
        att = att + jnp.concatenate(strips, axis=0)
        o = o + jnp.dot(att.astype(BF16), vh.astype(BF16), preferred_element_type=F32)
        outs.append(o)
        c_end = ch[0:1] if rev else ch[CHUNK - 1:CHUNK]
        kd = kh * jnp.exp(c_end - ch)
        st_s[d, h] = st * jnp.exp(c_end) + _dot_tn(vh.astype(BF16), kd.astype(BF16))
    o_ref[...] = jnp.concatenate(outs, axis=-1)


def _hgrn_body(qf_ref, ff_ref, vf_ref, qb_ref, fb_ref, vb_ref, la_ref, lc_ref, om_ref, s0_ref,
               of_ref, ob_ref, sfin_ref, st_s):
    i = pl.program_id(1)

    @pl.when(i == 0)
    def _():
        for d in range(2):
            for h in range(HA):
                st_s[d, h] = s0_ref[d, h].T

    _hgrn_direction(False, qf_ref, ff_ref, vf_ref, la_ref[0:1], lc_ref[0:1], om_ref[0:1], st_s, 0, of_ref)
    _hgrn_direction(True, qb_ref, fb_ref, vb_ref, la_ref[1:2], lc_ref[1:2], om_ref[1:2], st_s, 1, ob_ref)

    @pl.when(i == pl.num_programs(1) - 1)
    def _():
        for d in range(2):
            for h in range(HA):
                sfin_ref[d, h] = st_s[d, h].T


def hgrn_scan(proj3, lb, s0):
    b, t, _ = proj3.shape
    w = HA * DKA
    n = t // CHUNK
    lb = lb.astype(F32)
    la = jnp.log(jnp.maximum(lb, LB_FLOOR))
    lc = jnp.log1p(-lb)
    om = 1.0 - lb

    def fwd_col(c):
        return pl.BlockSpec((None, CHUNK, w), lambda bi, i: (bi, i, c))

    def bwd_col(c):
        return pl.BlockSpec((None, CHUNK, w), lambda bi, i: (bi, n - 1 - i, c))
    fwd, bwd = fwd_col(0), bwd_col(0)
    par = pl.BlockSpec((2, w), lambda bi, i: (0, 0))
    st = pl.BlockSpec((None, 2, HA, DKA, DVA), lambda bi, i: (bi, 0, 0, 0, 0))
    q, f_fwd, f_bwd, v = proj3, proj3, proj3, proj3
    return pl.pallas_call(
        _hgrn_body,
        grid=(b, n),
        in_specs=[fwd_col(COL_HQ), fwd_col(COL_HFF), fwd_col(COL_HI), bwd_col(COL_HQ), bwd_col(COL_HFB),
                  bwd_col(COL_HI), par, par, par, st],
        out_specs=[fwd, bwd, st],
        out_shape=[jax.ShapeDtypeStruct((b, t, w), F32), jax.ShapeDtypeStruct((b, t, w), F32),
                   jax.ShapeDtypeStruct((b, 2, HA, DKA, DVA), F32)],
        scratch_shapes=[pltpu.VMEM((2, HA, DVA, DKA), F32)],
        compiler_params=pltpu.CompilerParams(dimension_semantics=("parallel", "arbitrary"),
                                             vmem_limit_bytes=VMEM_LIMIT),
        name="hgrn_scan",
    )(q, f_fwd, v, q, f_bwd, v, la, lc, om, s0)


def _hgrn_combine_body(of_ref, ob_ref, g_ref, w_ref, o_ref):
    y = of_ref[...] + ob_ref[...]
    g = g_ref[...]
    outs = []
    for h in range(HA):
        cs = slice(h * DVA, (h + 1) * DVA)
        yh = y[:, cs]
        yn = yh * lax.rsqrt(jnp.mean(yh * yh, axis=-1, keepdims=True) + EPS)
        outs.append(yn * w_ref[:, cs] * jax.nn.silu(g[:, cs]))
    o_ref[...] = jnp.concatenate(outs, axis=-1)


def hgrn_combine(o_f, o_b, proj2, norm_w):
    b, t, w = o_f.shape
    rows = 256
    blk = pl.BlockSpec((rows, w), lambda i: (i, 0))
    return pl.pallas_call(
        _hgrn_combine_body,
        grid=(b * t // rows,),
        in_specs=[blk, blk, pl.BlockSpec((rows, w), lambda i: (i, COL_HG)), pl.BlockSpec((1, w), lambda i: (0, 0))],
        out_specs=blk,
        out_shape=jax.ShapeDtypeStruct((b * t, w), F32),
        compiler_params=pltpu.CompilerParams(dimension_semantics=("parallel",)),
        name="hgrn_combine",
    )(o_f.reshape(b * t, w), o_b.reshape(b * t, w), proj2, norm_w.astype(F32).reshape(1, w))


def ssd_mixer(parts, conv_w, conv_b, dt_bias, a_log, d_skip, norm_w, init_state):
    x, z, b_in, c_in, dt_fwd, dt_bwd = parts
    b, t, _ = x.shape
    xbc = jax.nn.silu(conv_centred(jnp.concatenate([x, b_in, c_in], axis=-1), conv_w, conv_b))
    xs, bs, cs = split_cols(xbc, (GROUP_W, GC * NC, GC * NC))
    xh = xs.astype(F32).reshape(b, t, HC, PC)
    bg = bs.astype(F32).reshape(b, t, GC, NC)
    cg = cs.astype(F32).reshape(b, t, GC, NC)
    a = -jnp.exp(a_log.astype(F32))
    y = d_skip.astype(F32)[:, None] * xh
    finals = []
    for d, dt_raw in enumerate((dt_fwd, dt_bwd)):
        dt = jax.nn.softplus(dt_raw.astype(F32) + dt_bias[d].astype(F32))
        if init_state is None:
            s0 = jnp.zeros((b, HC, PC, NC), F32)
        else:
            s0 = init_state[:, d].astype(F32)
        if d == 0:
            yd, sf = ssd_chunk_scan(xh, dt, a[d], bg, cg, s0)
        else:
            yd, sf = ssd_chunk_scan(flip_t(xh), flip_t(dt), a[d], flip_t(bg), flip_t(cg), s0)
            yd = flip_t(yd)
        y = y + yd
        finals.append(sf)
    y = y.reshape(b, t, GROUP_W) * jax.nn.silu(z.astype(F32))
    return rmsnorm(y, norm_w).astype(x.dtype), jnp.stack(finals, axis=1)


def mla_project(parts, p):
    cq, ckv, krope = parts
    b, t, _ = cq.shape
    q = (rmsnorm(cq, p['mla_q_norm_w']) @ p['mla_w_q_up']).reshape(b, t, HB, NOPE + ROPE)
    return q[..., :NOPE], q[..., NOPE:], rmsnorm(ckv, p['mla_kv_norm_w']), krope


def mla_keys_values(ckv, krope, w_kv_up):
    b, t, _ = ckv.shape
    kv = (ckv @ w_kv_up).reshape(b, t, HB, NOPE + VB)
    k = jnp.concatenate([kv[..., :NOPE], jnp.broadcast_to(krope[:, :, None, :], (b, t, HB, ROPE))], axis=-1)
    return k, kv[..., NOPE:]


def mla_context(parts, p):
    q_nope, q_rope, ckv, krope = mla_project(parts, p)
    k, v = mla_keys_values(ckv, krope, p['mla_w_kv_up'])
    o = block_attention(jnp.concatenate([q_nope, q_rope], axis=-1), k, v, MLA_SCALE)
    b, t = o.shape[:2]
    return o.reshape(b, t, HB * VB), ckv, krope


def mla_latent(parts, p, ckv_ctx, krope_ctx):
    q_nope, q_rope, ckv, krope = mla_project(parts, p)
    q = jnp.concatenate([q_nope, rope_2d(q_rope)], axis=-1)
    k_lat, v_lat = mla_keys_values(ckv, rope_2d(krope), p['mla_w_kv_up'])
    k_ctx, v_ctx = mla_keys_values(ckv_ctx.astype(ckv.dtype), krope_ctx.astype(krope.dtype), p['mla_w_kv_up'])
    o = block_attention(q, jnp.concatenate([k_lat, k_ctx], axis=1), jnp.concatenate([v_lat, v_ctx], axis=1),
                        MLA_SCALE)
    b, t = o.shape[:2]
    return o.reshape(b, t, HB * VB)


def natten_context(parts):
    b, t, _ = parts[0].shape
    q, k, v = [a.reshape(b, t, HD, DHD) for a in parts]
    o = block_attention(q, k, v, DHD ** -0.5)
    return o.reshape(b, t, GROUP_W), k, v


BF16 = jnp.bfloat16
LANES = 128
VMEM_LIMIT = 56 * 1024 * 1024
NEG_BIG = -1e30

NA_ROWS = DEC_SEQ // GRID_W
NA_WR = min(WIN_R, NA_ROWS)
NA_WIN = NA_WR * GRID_W


def _dot_nt(a, b):
    return lax.dot_general(a, b, (((1,), (1,)), ((), ())), preferred_element_type=F32)


def _dot_tn(a, b):
    return lax.dot_general(a, b, (((0,), (0,)), ((), ())), preferred_element_type=F32)


def natten_bias_table(rpb):
    cols = jnp.arange(GRID_W)
    start = jnp.clip(cols - WIN_C // 2, 0, GRID_W - WIN_C)
    in_win = (cols[None, :] >= start[:, None]) & (cols[None, :] < start[:, None] + WIN_C)
    c_off = jnp.clip(cols[None, :] - cols[:, None] + (WIN_C - 1), 0, 2 * WIN_C - 2)
    r_off = jnp.arange(NA_WR)[:, None] - (NA_WR - 1) + jnp.arange(NA_WR)[None, :] + (WIN_R - 1)
    tab = rpb.astype(F32)[:, r_off][:, :, :, c_off]
    tab = jnp.where(in_win[None, None, None], tab, NEG_BIG)
    return tab.transpose(0, 1, 3, 2, 4).reshape(HD, NA_WR, GRID_W, NA_WIN)


def _natten_body(q_ref, k_ref, v_ref, kc_ref, vc_ref, bias_ref, o_ref):
    r = pl.program_id(1)
    rs = jnp.clip(r - NA_WR // 2, 0, NA_ROWS - NA_WR)
    win = pl.ds(pl.multiple_of(rs * GRID_W, GRID_W), NA_WIN)
    q = q_ref[...] * (DHD ** -0.5)
    outs = []
    for h in range(HD):
        cs = slice(h * DHD, (h + 1) * DHD)
        qh = q[:, cs].astype(BF16)
        s_loc = _dot_nt(qh, k_ref[win, cs].astype(BF16)) + bias_ref[h]
        s_ctx = _dot_nt(qh, kc_ref[:, cs].astype(BF16))
        m = jnp.maximum(jnp.max(s_loc, axis=-1, keepdims=True), jnp.max(s_ctx, axis=-1, keepdims=True))
        p_loc = jnp.exp(s_loc - m)
        p_ctx = jnp.exp(s_ctx - m)
        inv = 1.0 / (jnp.sum(p_loc, axis=-1, keepdims=True) + jnp.sum(p_ctx, axis=-1, keepdims=True))
        outs.append(jnp.dot((p_loc * inv).astype(BF16), v_ref[win, cs].astype(BF16), preferred_element_type=F32)
                    + jnp.dot((p_ctx * inv).astype(BF16), vc_ref[:, cs].astype(BF16),
                              preferred_element_type=F32))
    o_ref[...] = jnp.concatenate(outs, axis=-1)


def natten_latent(proj3, rpb, k_ctx, v_ctx):
    b, t, _ = proj3.shape
    w = GROUP_W
    assert t == DEC_SEQ
    n_ctx = k_ctx.shape[1]
    bias = natten_bias_table(rpb)
    q = k = v = proj3

    def delta_idx(r):
        return jnp.clip(r - NA_WR // 2, 0, NA_ROWS - NA_WR) - r + (NA_WR - 1)
    ctx = pl.BlockSpec((None, n_ctx, w), lambda bi, r: (bi, 0, 0))
    return pl.pallas_call(
        _natten_body,
        grid=(b, NA_ROWS),
        in_specs=[pl.BlockSpec((None, GRID_W, w), lambda bi, r: (bi, r, COL_NQ)),
                  pl.BlockSpec((None, t, w), lambda bi, r: (bi, 0, COL_NK)),
                  pl.BlockSpec((None, t, w), lambda bi, r: (bi, 0, COL_NV)), ctx, ctx,
                  pl.BlockSpec((HD, None, GRID_W, NA_WIN), lambda bi, r: (0, delta_idx(r), 0, 0))],
        out_specs=pl.BlockSpec((None, GRID_W, w), lambda bi, r: (bi, r, 0)),
        out_shape=jax.ShapeDtypeStruct((b, t, w), F32),
        compiler_params=pltpu.CompilerParams(dimension_semantics=("parallel", "arbitrary"),
                                             vmem_limit_bytes=VMEM_LIMIT),
        name="natten_latent",
    )(q, k, v, k_ctx.reshape(b, n_ctx, w), v_ctx.reshape(b, n_ctx, w), bias)


PEER_KDIM = PEER_QDIM // 2
PEER_HP = PEER_HEADS * 2
PEER_PREP_TE = 512
PEER_SCORE_TM = 512
PEER_SELECT_TM = 256
PEER_TM = 512
PEER_TE = 1024
PEER_ROWS = PEER_TE // N_KEYS
PEER_CAND = [(a, b) for a in range(PEER_TOPK) for b in range(PEER_TOPK) if (a + 1) * (b + 1) <= PEER_TOPK]


def _peer_prep_body(u_ref, v_ref, ub_ref, vt_ref):
    ub_ref[...] = u_ref[...].astype(BF16)
    vt_ref[...] = v_ref[...].T.astype(BF16)


def peer_prep_tables(peer_u, peer_v):
    depth, n_e, d = peer_u.shape
    te = PEER_PREP_TE
    return pl.pallas_call(
        _peer_prep_body,
        grid=(depth, n_e // te),
        in_specs=[pl.BlockSpec((None, te, d), lambda l, j: (l, j, 0)),
                  pl.BlockSpec((None, te, d), lambda l, j: (l, j, 0))],
        out_specs=[pl.BlockSpec((None, te, d), lambda l, j: (l, j, 0)),
                   pl.BlockSpec((None, d, te), lambda l, j: (l, 0, j))],
        out_shape=[jax.ShapeDtypeStruct((depth, n_e, d), BF16), jax.ShapeDtypeStruct((depth, d, n_e), BF16)],
        compiler_params=pltpu.CompilerParams(dimension_semantics=("parallel", "parallel"),
                                             vmem_limit_bytes=VMEM_LIMIT),
        name="peer_prep_tables",
    )(peer_u, peer_v)


def _peer_score_body(h_ref, wqt_ref, sk_ref, xt_ref, st_ref):
    xt = h_ref[...].T.astype(BF16)
    xt_ref[...] = xt
    qt = jnp.dot(wqt_ref[...], xt, preferred_element_type=F32).astype(BF16)
    for hp in range(PEER_HP):
        st_ref[hp] = jnp.dot(sk_ref[hp % 2], qt[hp * PEER_KDIM:(hp + 1) * PEER_KDIM, :],
                             preferred_element_type=F32)


def peer_scores(hf, wq_t, sub_keys):
    n, d = hf.shape
    tm = PEER_SCORE_TM
    return pl.pallas_call(
        _peer_score_body,
        grid=(n // tm,),
        in_specs=[pl.BlockSpec((tm, d), lambda i: (i, 0)),
                  pl.BlockSpec(wq_t.shape, lambda i: (0, 0)),
                  pl.BlockSpec(sub_keys.shape, lambda i: (0, 0, 0))],
        out_specs=[pl.BlockSpec((d, tm), lambda i: (0, i)),
                   pl.BlockSpec((PEER_HP, N_KEYS, tm), lambda i: (0, 0, i))],
        out_shape=[jax.ShapeDtypeStruct((d, n), BF16), jax.ShapeDtypeStruct((PEER_HP, N_KEYS, n), F32)],
        compiler_params=pltpu.CompilerParams(dimension_semantics=("parallel",), vmem_limit_bytes=VMEM_LIMIT),
        name="peer_scores",
    )(hf, wq_t, sub_keys)


def _peer_select_body(st_ref, e1_ref, cnt1_ref, rank2_ref, e2_ref, rank1_s, vtop_s, cnt_s, zinv_s):
    tm = st_ref.shape[-1]
    kio = lax.broadcasted_iota(jnp.int32, (N_KEYS, LANES), 0).astype(F32)
    neg = jnp.float32(-jnp.inf)

    def group(g, carry):
        ln = pl.ds(pl.multiple_of(g * LANES, LANES), LANES)

        for hp in range(PEER_HP):
            h, part = hp // 2, hp % 2

            def extract(it, sr):
                s, rank = sr
                m = jnp.max(s, axis=0, keepdims=True)
                first = jnp.min(jnp.where(s == m, kio, float(N_KEYS)), axis=0, keepdims=True)
                sel = kio == first
                vtop_s[part, it, h:h + 1, ln] = m
                return jnp.where(sel, neg, s), jnp.where(sel, it, rank)
            _, rank = lax.fori_loop(0, PEER_TOPK, extract,
                                    (st_ref[hp, :, ln], jnp.full((N_KEYS, LANES), PEER_TOPK, jnp.int32)))
            rank = rank.astype(F32)
            if part == 0:
                rank1_s[h, :, ln] = rank
            else:
                rank2_ref[h, :, ln] = rank

        v1 = [vtop_s[0, a, :, ln] for a in range(PEER_TOPK)]
        v2 = [vtop_s[1, b, :, ln] for b in range(PEER_TOPK)]
        sums = [v1[a] + v2[b] for a, b in PEER_CAND]
        n_c = len(PEER_CAND)
        beaten = [jnp.zeros((PEER_HEADS, LANES), F32) for _ in range(n_c)]
        for i in range(n_c):
            for j in range(i + 1, n_c):
                ge = sums[i] >= sums[j]
                beaten[j] = beaten[j] + jnp.where(ge, 1.0, 0.0)
                beaten[i] = beaten[i] + jnp.where(ge, 0.0, 1.0)
        z = jnp.zeros((PEER_HEADS, LANES), F32)
        cnt = [jnp.zeros((PEER_HEADS, LANES), F32) for _ in range(PEER_TOPK)]
        for i, (a, b) in enumerate(PEER_CAND):
            keep = beaten[i] < float(PEER_TOPK)
            z = z + jnp.where(keep, jnp.exp(sums[i] - sums[0]), 0.0)
            cnt[a] = cnt[a] + jnp.where(keep, 1.0, 0.0)
        for a in range(PEER_TOPK):
            cnt_s[a, :, ln] = cnt[a]
        zinv_s[:, ln] = 1.0 / z

        for h in range(PEER_HEADS):
            rank1 = rank1_s[h, :, ln]
            s1 = st_ref[2 * h, :, ln]
            e1 = jnp.exp(s1 - vtop_s[0, 0, h:h + 1, ln]) * zinv_s[h:h + 1, ln]
            e1_ref[h, :, ln] = jnp.where(rank1 < float(PEER_TOPK), e1, 0.0)
            c1 = jnp.zeros((N_KEYS, LANES), F32)
            for a in range(PEER_TOPK):
                c1 = c1 + jnp.where(rank1 == float(a), cnt_s[a, h:h + 1, ln], 0.0)
            cnt1_ref[h, :, ln] = c1
            e2_ref[h, :, ln] = jnp.exp(st_ref[2 * h + 1, :, ln] - vtop_s[1, 0, h:h + 1, ln])
        return carry
    lax.fori_loop(0, tm // LANES, group, 0)


def peer_select(st):
    _, _, n = st.shape
    tm = PEER_SELECT_TM
    spec = pl.BlockSpec((PEER_HEADS, N_KEYS, tm), lambda i: (0, 0, i))
    shp = jax.ShapeDtypeStruct((PEER_HEADS, N_KEYS, n), F32)
    return pl.pallas_call(
        _peer_select_body,
        grid=(n // tm,),
        in_specs=[pl.BlockSpec((PEER_HP, N_KEYS, tm), lambda i: (0, 0, i))],
        out_specs=[spec, spec, spec, spec],
        out_shape=[shp, shp, shp, shp],
        scratch_shapes=[pltpu.VMEM((PEER_HEADS, N_KEYS, tm), F32),
                        pltpu.VMEM((2, PEER_TOPK, PEER_HEADS, tm), F32),
                        pltpu.VMEM((PEER_TOPK, PEER_HEADS, tm), F32),
                        pltpu.VMEM((PEER_HEADS, tm), F32)],
        compiler_params=pltpu.CompilerParams(dimension_semantics=("parallel",), vmem_limit_bytes=VMEM_LIMIT),
        name="peer_select",
    )(st)


def _gelu_tanh(x):
    return 0.5 * x * (1.0 + jnp.tanh(math.sqrt(2.0 / math.pi) * (x + 0.044715 * (x * x * x))))


def _peer_expert_body(u_ref, vt_ref, xt_ref, e1_ref, cnt1_ref, rank2_ref, e2_ref, o_ref, st_s, at_s):
    j = pl.program_id(1)
    tm = xt_ref.shape[-1]

    @pl.when(j == 0)
    def _():
        o_ref[...] = jnp.zeros_like(o_ref)

    st_s[...] = jnp.dot(u_ref[...], xt_ref[...], preferred_element_type=F32)

    def group(g, carry):
        ln = pl.ds(pl.multiple_of(g * LANES, LANES), LANES)
        for r in range(PEER_ROWS):
            gate = jnp.zeros((N_KEYS, LANES), F32)
            for h in range(PEER_HEADS):
                keep = rank2_ref[h, :, ln] < cnt1_ref[h, r:r + 1, ln]
                gate = gate + jnp.where(keep, e2_ref[h, :, ln], 0.0) * e1_ref[h, r:r + 1, ln]
            rows = slice(r * N_KEYS, (r + 1) * N_KEYS)
            at_s[rows, ln] = (_gelu_tanh(st_s[rows, ln]) * gate).astype(BF16)
        return carry
    lax.fori_loop(0, tm // LANES, group, 0)

    o_ref[...] += jnp.dot(vt_ref[...], at_s[...], preferred_element_type=F32)


def peer_experts(ub, vt, xt, e1, cnt1, rank2, e2):
    n_e, d = ub.shape
    n = xt.shape[1]
    tm, te = PEER_TM, PEER_TE
    row_spec = pl.BlockSpec((PEER_HEADS, PEER_ROWS, tm), lambda i, j: (0, j, i))
    col_spec = pl.BlockSpec((PEER_HEADS, N_KEYS, tm), lambda i, j: (0, 0, i))
    return pl.pallas_call(
        _peer_expert_body,
        grid=(n // tm, n_e // te),
        in_specs=[pl.BlockSpec((te, d), lambda i, j: (j, 0)),
                  pl.BlockSpec((d, te), lambda i, j: (0, j)),
                  pl.BlockSpec((d, tm), lambda i, j: (0, i)),
                  row_spec, row_spec, col_spec, col_spec],
        out_specs=pl.BlockSpec((d, tm), lambda i, j: (0, i)),
        out_shape=jax.ShapeDtypeStruct((d, n), F32),
        scratch_shapes=[pltpu.VMEM((te, tm), F32), pltpu.VMEM((te, tm), BF16)],
        compiler_params=pltpu.CompilerParams(dimension_semantics=("parallel", "arbitrary"),
                                             vmem_limit_bytes=VMEM_LIMIT),
        name="peer_experts",
    )(ub, vt, xt, e1, cnt1, rank2, e2)


def peer_ffn(hf, wq_t, sub_keys_b, ub, vt):
    xt, st = peer_scores(hf, wq_t, sub_keys_b)
    e1, cnt1, rank2, e2 = peer_select(st)
    return peer_experts(ub, vt, xt, e1, cnt1, rank2, e2)


PROJ_ORDER = (0, 1, 2, 3, 4, 5, 8, 9, 14, 15, 16, 6, 10, 11, 7, 12, 13)
PROJ_W = 6528
COL_HQ, COL_HFF, COL_HFB, COL_HI, COL_HG, COL_CQ, COL_SX, COL_SZ, COL_NQ, COL_NK, COL_NV = range(11)
COL_CKV, COL_SB, COL_SC = 22, 23, 24
COL_SMALL = 50
SMALL_DT = ROPE
PROJ_TM = 512
PROJ_TN = 2176
MOD_SH1, MOD_SC1, MOD_G1, MOD_SH2, MOD_SC2, MOD_G2 = range(6)


def in_proj_weight(w_in):
    offs = np.cumsum((0,) + IN_SPLITS)
    cols = [w_in[:, offs[k]:offs[k + 1]] for k in PROJ_ORDER]
    cols.append(jnp.zeros((w_in.shape[0], PROJ_W - IN_W), w_in.dtype))
    return jnp.concatenate(cols, axis=1).astype(BF16)


def _adaln_body(c_ref, w_ref, b_ref, o_ref):
    a = jax.nn.silu(c_ref[...]).astype(BF16)
    o_ref[...] = jnp.dot(a, w_ref[...].astype(BF16), preferred_element_type=F32) + b_ref[...]


def adaln(cond, w_ada, b_ada):
    r, d = cond.shape
    n = w_ada.shape[1]
    tn = 1024
    return pl.pallas_call(
        _adaln_body,
        grid=(n // tn,),
        in_specs=[pl.BlockSpec((r, d), lambda j: (0, 0)), pl.BlockSpec((d, tn), lambda j: (0, j)),
                  pl.BlockSpec((1, tn), lambda j: (0, j))],
        out_specs=pl.BlockSpec((r, tn), lambda j: (0, j)),
        out_shape=jax.ShapeDtypeStruct((r, n), F32),
        compiler_params=pltpu.CompilerParams(dimension_semantics=("parallel",), vmem_limit_bytes=VMEM_LIMIT),
        name="adaln",
    )(cond, w_ada, b_ada.reshape(1, n))


def _rms_rows(x, w):
    return x * lax.rsqrt(jnp.mean(x * x, axis=-1, keepdims=True) + EPS) * w


def _in_proj_body(x_ref, nw_ref, sc_ref, sh_ref, w_ref, o_ref, h_s):
    @pl.when(pl.program_id(1) == 0)
    def _():
        h_s[...] = (_rms_rows(x_ref[...], nw_ref[...]) * (1.0 + sc_ref[...]) + sh_ref[...]).astype(BF16)
    o_ref[...] = jnp.dot(h_s[...], w_ref[...], preferred_element_type=F32)


def in_proj(x, norm_w, mod, w_perm, rows_per_mod):
    n, d = x.shape
    tm, tn = PROJ_TM, PROJ_TN

    def mod_spec(part):
        return pl.BlockSpec((None, 1, d), lambda i, j: ((i * tm) // rows_per_mod, 0, part))
    return pl.pallas_call(
        _in_proj_body,
        grid=(n // tm, PROJ_W // tn),
        in_specs=[pl.BlockSpec((tm, d), lambda i, j: (i, 0)), pl.BlockSpec((1, d), lambda i, j: (0, 0)),
                  mod_spec(MOD_SC1), mod_spec(MOD_SH1), pl.BlockSpec((d, tn), lambda i, j: (0, j))],
        out_specs=pl.BlockSpec((tm, tn), lambda i, j: (i, j)),
        out_shape=jax.ShapeDtypeStruct((n, PROJ_W), F32),
        scratch_shapes=[pltpu.VMEM((tm, d), BF16)],
        compiler_params=pltpu.CompilerParams(dimension_semantics=("parallel", "arbitrary"),
                                             vmem_limit_bytes=VMEM_LIMIT),
        name="in_proj",
    )(x, norm_w.reshape(1, d), mod, mod, w_perm)


def _out_proj_body(oa_ref, ob_ref, oc_ref, od_ref, w_ref, x_ref, g_ref, sc_ref, sh_ref, nw_ref, xo_ref, h_ref):
    mix = None
    for k, o_ref in enumerate((oa_ref, ob_ref, oc_ref, od_ref)):
        t = jnp.dot(o_ref[...].astype(BF16), w_ref[k * GROUP_W:(k + 1) * GROUP_W, :], preferred_element_type=F32)
        mix = t if mix is None else mix + t
    x = x_ref[...] + g_ref[...] * mix
    xo_ref[...] = x
    h_ref[...] = _rms_rows(x, nw_ref[...]) * (1.0 + sc_ref[...]) + sh_ref[...]


def out_proj(o_parts, w_out_b, x, mod, norm2_w, rows_per_mod):
    n, d = x.shape
    tm = 256

    def mod_spec(part):
        return pl.BlockSpec((None, 1, d), lambda i: ((i * tm) // rows_per_mod, 0, part))
    part = pl.BlockSpec((tm, GROUP_W), lambda i: (i, 0))
    row = pl.BlockSpec((tm, d), lambda i: (i, 0))
    return pl.pallas_call(
        _out_proj_body,
        grid=(n // tm,),
        in_specs=[part, part, part, part, pl.BlockSpec((D_MIX, d), lambda i: (0, 0)), row,
                  mod_spec(MOD_G1), mod_spec(MOD_SC2), mod_spec(MOD_SH2), pl.BlockSpec((1, d), lambda i: (0, 0))],
        out_specs=[row, row],
        out_shape=[jax.ShapeDtypeStruct((n, d), F32), jax.ShapeDtypeStruct((n, d), F32)],
        compiler_params=pltpu.CompilerParams(dimension_semantics=("parallel",), vmem_limit_bytes=VMEM_LIMIT),
        name="out_proj",
    )(*o_parts, w_out_b, x, mod, mod, mod, norm2_w.reshape(1, d))


def _norm_matmul_body(x_ref, nw_ref, w_ref, y_ref, xn_ref):
    xn = _rms_rows(x_ref[...], nw_ref[...])
    xn_ref[...] = xn
    y_ref[...] = jnp.dot(xn.astype(BF16), w_ref[...], preferred_element_type=F32)


def _matmul_body(x_ref, w_ref, y_ref):
    y_ref[...] = jnp.dot(x_ref[...].astype(BF16), w_ref[...], preferred_element_type=F32)


def norm_matmul(x, col, k, norm_w, w_b):
    n = x.shape[0]
    n_out = w_b.shape[1]
    tm = min(512, n)
    x_spec = pl.BlockSpec((tm, k), lambda i: (i, col))
    w_spec = pl.BlockSpec((k, n_out), lambda i: (0, 0))
    y_spec = pl.BlockSpec((tm, n_out), lambda i: (i, 0))
    params = pltpu.CompilerParams(dimension_semantics=("parallel",), vmem_limit_bytes=VMEM_LIMIT)
    if norm_w is None:
        return pl.pallas_call(_matmul_body, grid=(n // tm,), in_specs=[x_spec, w_spec], out_specs=y_spec,
                              out_shape=jax.ShapeDtypeStruct((n, n_out), F32), compiler_params=params,
                              name="matmul")(x, w_b)
    return pl.pallas_call(
        _norm_matmul_body, grid=(n // tm,),
        in_specs=[x_spec, pl.BlockSpec((1, k), lambda i: (0, 0)), w_spec],
        out_specs=[y_spec, pl.BlockSpec((tm, k), lambda i: (i, 0))],
        out_shape=[jax.ShapeDtypeStruct((n, n_out), F32), jax.ShapeDtypeStruct((n, k), F32)],
        compiler_params=params, name="norm_matmul")(x, norm_w.reshape(1, k), w_b)


ATTN_TQ = Q_BLOCK


def _attn_body(*refs, n_heads, parts, dv, scale, n_seg):
    n_p = len(parts)
    q_refs = refs[:n_p]
    pos = n_p
    segs = []
    for _ in range(n_seg):
        segs.append((refs[pos:pos + n_p], refs[pos + n_p]))
        pos += n_p + 1
    o_ref = refs[pos]
    outs = []
    for h in range(n_heads):
        scores = []
        for k_refs, _ in segs:
            s = None
            for q_ref, k_ref, (d, shared) in zip(q_refs, k_refs, parts):
                qh = q_ref[:, h * d:(h + 1) * d].astype(BF16)
                kh = (k_ref[:, 0:d] if shared else k_ref[:, h * d:(h + 1) * d]).astype(BF16)
                t = _dot_nt(qh, kh)
                s = t if s is None else s + t
            scores.append(s * scale)
        m = None
        for s in scores:
            ms = jnp.max(s, axis=-1, keepdims=True)
            m = ms if m is None else jnp.maximum(m, ms)
        probs = [jnp.exp(s - m) for s in scores]
        denom = None
        for p in probs:
            ps = jnp.sum(p, axis=-1, keepdims=True)
            denom = ps if denom is None else denom + ps
        inv = 1.0 / denom
        o = None
        for p, (_, v_ref) in zip(probs, segs):
            t = jnp.dot((p * inv).astype(BF16), v_ref[:, h * dv:(h + 1) * dv].astype(BF16),
                        preferred_element_type=F32)
            o = t if o is None else o + t
        outs.append(o)
    o_ref[...] = jnp.concatenate(outs, axis=-1)


def attention(q_parts, segments, n_heads, parts, dv, scale):
    b, tq, _ = q_parts[0][0].shape
    args, specs = [], []
    for arr, w, c in q_parts:
        args.append(arr)
        specs.append(pl.BlockSpec((None, ATTN_TQ, w), lambda bi, i, c=c: (bi, i, c)))
    for k_parts, v in segments:
        for arr, w, c in list(k_parts) + [v]:
            args.append(arr)
            specs.append(pl.BlockSpec((None, arr.shape[1], w), lambda bi, i, c=c: (bi, 0, c)))
    body = functools.partial(_attn_body, n_heads=n_heads, parts=parts, dv=dv, scale=scale, n_seg=len(segments))
    return pl.pallas_call(
        body,
        grid=(b, tq // ATTN_TQ),
        in_specs=specs,
        out_specs=pl.BlockSpec((None, ATTN_TQ, n_heads * dv), lambda bi, i: (bi, i, 0)),
        out_shape=jax.ShapeDtypeStruct((b, tq, n_heads * dv), F32),
        compiler_params=pltpu.CompilerParams(dimension_semantics=("parallel", "arbitrary"),
                                             vmem_limit_bytes=VMEM_LIMIT),
        name="attention",
    )(*args)


def mla_weights(w_q_up, w_kv_up):
    wq = w_q_up.reshape(Q_RANK, HB, NOPE + ROPE)
    wq = jnp.concatenate([wq[:, :, :NOPE].reshape(Q_RANK, HB * NOPE), wq[:, :, NOPE:].reshape(Q_RANK, HB * ROPE)], 1)
    wkv = w_kv_up.reshape(KV_RANK, HB, NOPE + VB)
    wkv = jnp.concatenate([wkv[:, :, :NOPE].reshape(KV_RANK, HB * NOPE), wkv[:, :, NOPE:].reshape(KV_RANK, HB * VB)], 1)
    return wq.astype(BF16), wkv.astype(BF16)


def rope_tables(t_len):
    t = jnp.arange(t_len)
    half = ROPE // 2
    inv = 1.0 / (ROPE_THETA ** (jnp.arange(0, half, 2, dtype=F32) / half))
    a_r = (t // GRID_W).astype(F32)[:, None] * inv[None, :]
    a_c = (t % GRID_W).astype(F32)[:, None] * inv[None, :]
    cos = jnp.concatenate([jnp.cos(a_r), jnp.cos(a_r), jnp.cos(a_c), jnp.cos(a_c)], axis=1)
    sin = jnp.concatenate([-jnp.sin(a_r), jnp.sin(a_r), -jnp.sin(a_c), jnp.sin(a_c)], axis=1)
    return cos, sin


def _rope_body(q_ref, small_ref, cos_ref, sin_ref, qo_ref, ko_ref):
    quarter = ROPE // 4
    r_io = lax.broadcasted_iota(jnp.int32, (ROPE, ROPE), 0)
    c_io = lax.broadcasted_iota(jnp.int32, (ROPE, ROPE), 1)
    partner = jnp.where((c_io // quarter) % 2 == 0, c_io + quarter, c_io - quarter)
    swap = jnp.where(r_io == partner, 1.0, 0.0).astype(BF16)
    cos, sin = cos_ref[...], sin_ref[...]

    def rot(x):
        xs = sum(jnp.dot(part, swap, preferred_element_type=F32) for part in _split3_bf16(x))
        return x * cos + xs * sin
    qo_ref[...] = jnp.concatenate([rot(q_ref[:, h * ROPE:(h + 1) * ROPE]) for h in range(HB)], axis=-1)
    ko_ref[...] = rot(small_ref[:, 0:ROPE])


def rope_rotate(q, proj3):
    b, t, _ = q.shape
    cos, sin = rope_tables(t)
    tb = 256
    tab = pl.BlockSpec((tb, ROPE), lambda bi, i: (i, 0))
    return pl.pallas_call(
        _rope_body,
        grid=(b, t // tb),
        in_specs=[pl.BlockSpec((None, tb, HB * ROPE), lambda bi, i: (bi, i, HB * NOPE // (HB * ROPE))),
                  pl.BlockSpec((None, tb, LANES), lambda bi, i: (bi, i, COL_SMALL)), tab, tab],
        out_specs=[pl.BlockSpec((None, tb, HB * ROPE), lambda bi, i: (bi, i, 0)),
                   pl.BlockSpec((None, tb, ROPE), lambda bi, i: (bi, i, 0))],
        out_shape=[jax.ShapeDtypeStruct((b, t, HB * ROPE), F32), jax.ShapeDtypeStruct((b, t, ROPE), F32)],
        compiler_params=pltpu.CompilerParams(dimension_semantics=("parallel", "parallel")),
        name="rope_rotate",
    )(q, proj3, cos, sin)


MLA_PARTS = ((NOPE, False), (ROPE, True))


def mla_mixer(proj2, b, t, p, cache):
    wq_b, wkv_b = p['mla_wq_b'], p['mla_wkv_b']
    q, _ = norm_matmul(proj2, COL_CQ, Q_RANK, p['mla_q_norm_w'], wq_b)
    kv, ckv_n = norm_matmul(proj2, COL_CKV, KV_RANK, p['mla_kv_norm_w'], wkv_b)
    q3, kv3, proj3 = q.reshape(b, t, -1), kv.reshape(b, t, -1), proj2.reshape(b, t, PROJ_W)
    if cache is None:
        o = attention([(q3, HB * NOPE, 0), (q3, HB * ROPE, 2)],
                      [([(kv3, HB * NOPE, 0), (proj3, LANES, COL_SMALL)], (kv3, HB * VB, 1))],
                      HB, MLA_PARTS, VB, MLA_SCALE)
        return o, ckv_n.reshape(b, t, KV_RANK), proj3[:, :, COL_SMALL * LANES:COL_SMALL * LANES + ROPE]
    ckv_ctx, krope_ctx = cache
    n_ctx = ckv_ctx.shape[1]
    kv_ctx = norm_matmul(ckv_ctx.reshape(b * n_ctx, KV_RANK).astype(F32), 0, KV_RANK, None, wkv_b)
    kv_ctx = kv_ctx.reshape(b, n_ctx, -1)
    q_rot, k_rot = rope_rotate(q3, proj3)
    o = attention([(q3, HB * NOPE, 0), (q_rot, HB * ROPE, 0)],
                  [([(kv3, HB * NOPE, 0), (k_rot, ROPE, 0)], (kv3, HB * VB, 1)),
                   ([(kv_ctx, HB * NOPE, 0), (krope_ctx.astype(F32), ROPE, 0)], (kv_ctx, HB * VB, 1))],
                  HB, MLA_PARTS, VB, MLA_SCALE)
    return o, None, None


def _ssd_conv_body(x_ref, b_ref, c_ref, w_ref, bias_ref, o_ref):
    t = x_ref.shape[0]
    xbc = jnp.concatenate([x_ref[...], b_ref[...], c_ref[...]], axis=-1)
    row = lax.broadcasted_iota(jnp.int32, (t, 1), 0)
    prev = jnp.where(row == 0, 0.0, pltpu.roll(xbc, 1, axis=0))
    nxt = jnp.where(row == t - 1, 0.0, pltpu.roll(xbc, t - 1, axis=0))
    o_ref[...] = jax.nn.silu(prev * w_ref[0:1] + xbc * w_ref[1:2] + nxt * w_ref[2:3] + bias_ref[...])


def ssd_conv(proj3, conv_w, conv_b):
    b, t, _ = proj3.shape
    return pl.pallas_call(
        _ssd_conv_body,
        grid=(b,),
        in_specs=[pl.BlockSpec((None, t, GROUP_W), lambda bi: (bi, 0, COL_SX)),
                  pl.BlockSpec((None, t, GC * NC), lambda bi: (bi, 0, COL_SB)),
                  pl.BlockSpec((None, t, GC * NC), lambda bi: (bi, 0, COL_SC)),
                  pl.BlockSpec((CONV_W, CONV_CH), lambda bi: (0, 0)), pl.BlockSpec((1, CONV_CH), lambda bi: (0, 0))],
        out_specs=pl.BlockSpec((None, t, CONV_CH), lambda bi: (bi, 0, 0)),
        out_shape=jax.ShapeDtypeStruct((b, t, CONV_CH), F32),
        compiler_params=pltpu.CompilerParams(dimension_semantics=("parallel",), vmem_limit_bytes=VMEM_LIMIT),
        name="ssd_conv",
    )(proj3, proj3, proj3, conv_w.astype(F32), conv_b.astype(F32).reshape(1, CONV_CH))


SSD_PAIRS = HC // 2


def _ssd_direction(rev, xbc_ref, dtc_ref, dtr_ref, bias_c, a_c, bias_r, a_r, st_s, d, y_ref):
    xbc = xbc_ref[...]
    dt_c = jax.nn.softplus(dtc_ref[...] + bias_c)
    la_c = dt_c * a_c
    la_r = jax.nn.softplus(dtr_ref[...] + bias_r) * a_r
    t_io = lax.broadcasted_iota(jnp.int32, (CHUNK, CHUNK), 0)
    s_io = lax.broadcasted_iota(jnp.int32, (CHUNK, CHUNK), 1)
    causal = (s_io >= t_io) if rev else (s_io <= t_io)
    tri = jnp.where(causal, 1.0, 0.0).astype(BF16)
    tri_t = jnp.where((t_io >= s_io) if rev else (t_io <= s_io), 1.0, 0.0).astype(BF16)
    cum_c = sum(jnp.dot(tri, part, preferred_element_type=F32) for part in _split3_bf16(la_c))
    cum_r = sum(jnp.dot(part, tri_t, preferred_element_type=F32) for part in _split3_bf16(la_r))
    end = 0 if rev else CHUNK - 1
    lane_lo = lax.broadcasted_iota(jnp.int32, (CHUNK, LANES), 1) < PC
    row_lo = lax.broadcasted_iota(jnp.int32, (LANES, NC), 0) < PC
    cbs = []
    for g in range(GC):
        bg = xbc[:, GROUP_W + g * NC:GROUP_W + (g + 1) * NC].astype(BF16)
        cg = xbc[:, GROUP_W + GC * NC + g * NC:GROUP_W + GC * NC + (g + 1) * NC].astype(BF16)
        cbs.append((bg, cg, _dot_nt(cg, bg)))
    outs = []
    for pr in range(SSD_PAIRS):
        ha, hb = 2 * pr, 2 * pr + 1
        bg, cg, cb = cbs[ha // (HC // GC)]
        xdt = xbc[:, pr * LANES:(pr + 1) * LANES] * jnp.where(lane_lo, dt_c[:, ha:ha + 1], dt_c[:, hb:hb + 1])
        ys = []
        for hx in (ha, hb):
            decay = jnp.exp(jnp.where(causal, cum_c[:, hx:hx + 1] - cum_r[hx:hx + 1, :], 0.0))
            m = (cb * jnp.where(causal, decay, 0.0)).astype(BF16)
            ys.append(jnp.dot(m, xdt.astype(BF16), preferred_element_type=F32))
        y = jnp.where(lane_lo, ys[0], ys[1])
        st = st_s[d, pr]
        e_cum = jnp.where(lane_lo, jnp.exp(cum_c[:, ha:ha + 1]), jnp.exp(cum_c[:, hb:hb + 1]))
        y = y + _dot_nt(cg, st.astype(BF16)) * e_cum
        outs.append(y)
        last_a, last_b = cum_c[end:end + 1, ha:ha + 1], cum_c[end:end + 1, hb:hb + 1]
        w = jnp.where(lane_lo, jnp.exp(last_a - cum_c[:, ha:ha + 1]), jnp.exp(last_b - cum_c[:, hb:hb + 1]))
        e_last = jnp.where(row_lo, jnp.exp(last_a), jnp.exp(last_b))
        st_s[d, pr] = e_last * st + _dot_tn((xdt * w).astype(BF16), bg)
    y_ref[...] = jnp.concatenate(outs, axis=-1)


def _ssd_body(xf_ref, xb_ref, dcf_ref, dcb_ref, drf_ref, drb_ref, bc_ref, ac_ref, br_ref, ar_ref, s0_ref,
              yf_ref, yb_ref, sfin_ref, st_s):
    i = pl.program_id(1)

    @pl.when(i == 0)
    def _():
        st_s[...] = s0_ref[...]

    _ssd_direction(False, xf_ref, dcf_ref, drf_ref, bc_ref[0], ac_ref[0], br_ref[0], ar_ref[0], st_s, 0, yf_ref)
    _ssd_direction(True, xb_ref, dcb_ref, drb_ref, bc_ref[1], ac_ref[1], br_ref[1], ar_ref[1], st_s, 1, yb_ref)

    @pl.when(i == pl.num_programs(1) - 1)
    def _():
        sfin_ref[...] = st_s[...]


def ssd_scan(xbc, dt_f, dt_b, dt_bias, a_log, s0):
    b, t, _ = xbc.shape
    n = t // CHUNK
    a = -jnp.exp(a_log.astype(F32))
    bias = dt_bias.astype(F32)
    dcf, dcb = dt_f.reshape(b, n, CHUNK, HC), dt_b.reshape(b, n, CHUNK, HC)
    drf, drb = dcf.transpose(0, 1, 3, 2), dcb.transpose(0, 1, 3, 2)
    x_f = pl.BlockSpec((None, CHUNK, CONV_CH), lambda bi, i: (bi, i, 0))
    x_b = pl.BlockSpec((None, CHUNK, CONV_CH), lambda bi, i: (bi, n - 1 - i, 0))
    y_f = pl.BlockSpec((None, CHUNK, GROUP_W), lambda bi, i: (bi, i, 0))
    y_b = pl.BlockSpec((None, CHUNK, GROUP_W), lambda bi, i: (bi, n - 1 - i, 0))
    c_f = pl.BlockSpec((None, None, CHUNK, HC), lambda bi, i: (bi, i, 0, 0))
    c_b = pl.BlockSpec((None, None, CHUNK, HC), lambda bi, i: (bi, n - 1 - i, 0, 0))
    r_f = pl.BlockSpec((None, None, HC, CHUNK), lambda bi, i: (bi, i, 0, 0))
    r_b = pl.BlockSpec((None, None, HC, CHUNK), lambda bi, i: (bi, n - 1 - i, 0, 0))
    p_c = pl.BlockSpec((2, 1, HC), lambda bi, i: (0, 0, 0))
    p_r = pl.BlockSpec((2, HC, 1), lambda bi, i: (0, 0, 0))
    st = pl.BlockSpec((None, 2, SSD_PAIRS, LANES, NC), lambda bi, i: (bi, 0, 0, 0, 0))
    y_fwd, y_bwd, s_fin = pl.pallas_call(
        _ssd_body,
        grid=(b, n),
        in_specs=[x_f, x_b, c_f, c_b, r_f, r_b, p_c, p_c, p_r, p_r, st],
        out_specs=[y_f, y_b, st],
        out_shape=[jax.ShapeDtypeStruct((b, t, GROUP_W), F32), jax.ShapeDtypeStruct((b, t, GROUP_W), F32),
                   jax.ShapeDtypeStruct((b, 2, SSD_PAIRS, LANES, NC), F32)],
        scratch_shapes=[pltpu.VMEM((2, SSD_PAIRS, LANES, NC), F32)],
        compiler_params=pltpu.CompilerParams(dimension_semantics=("parallel", "arbitrary"),
                                             vmem_limit_bytes=VMEM_LIMIT),
        name="ssd_scan",
    )(xbc, xbc, dcf, dcb, drf, drb, bias.reshape(2, 1, HC), a.reshape(2, 1, HC), bias.reshape(2, HC, 1),
      a.reshape(2, HC, 1), s0.astype(F32).reshape(b, 2, SSD_PAIRS, LANES, NC))
    return y_fwd, y_bwd, s_fin.reshape(b, 2, HC, PC, NC)


def _ssd_combine_body(x_ref, yf_ref, yb_ref, z_ref, d_ref, w_ref, o_ref):
    y = (d_ref[...] * x_ref[...] + yf_ref[...] + yb_ref[...]) * jax.nn.silu(z_ref[...])
    o_ref[...] = _rms_rows(y, w_ref[...])


def ssd_combine(xbc2, y_f, y_b, proj2, d_skip, norm_w):
    n = xbc2.shape[0]
    rows = 256
    blk = pl.BlockSpec((rows, GROUP_W), lambda i: (i, 0))
    par = pl.BlockSpec((1, GROUP_W), lambda i: (0, 0))
    return pl.pallas_call(
        _ssd_combine_body,
        grid=(n // rows,),
        in_specs=[blk, blk, blk, pl.BlockSpec((rows, GROUP_W), lambda i: (i, COL_SZ)), par, par],
        out_specs=blk,
        out_shape=jax.ShapeDtypeStruct((n, GROUP_W), F32),
        compiler_params=pltpu.CompilerParams(dimension_semantics=("parallel",)),
        name="ssd_combine",
    )(xbc2, y_f.reshape(n, GROUP_W), y_b.reshape(n, GROUP_W), proj2,
      jnp.repeat(d_skip.astype(F32), PC).reshape(1, GROUP_W), norm_w.astype(F32).reshape(1, GROUP_W))


def ssd_mixer(proj2, b, t, p, init_state):
    proj3 = proj2.reshape(b, t, PROJ_W)
    xbc = ssd_conv(proj3, p['ssd_conv_w'], p['ssd_conv_b'])
    dt0 = COL_SMALL * LANES + SMALL_DT
    dt_f, dt_b = proj3[:, :, dt0:dt0 + HC], proj3[:, :, dt0 + HC:dt0 + 2 * HC]
    s0 = jnp.zeros((b, 2, HC, PC, NC), F32) if init_state is None else init_state
    y_f, y_b, s_fin = ssd_scan(xbc, dt_f, dt_b, p['ssd_dt_bias'], p['ssd_a_log'], s0)
    return ssd_combine(xbc.reshape(b * t, CONV_CH), y_f, y_b, proj2, p['ssd_d'], p['ssd_norm_w']), s_fin


def _peer_finish_body(ot_ref, x_ref, g_ref, o_ref):
    o_ref[...] = x_ref[...] + g_ref[...] * ot_ref[...].T


def peer_finish(out_t, tok0, x, mod, rows_per_mod):
    n, d = x.shape
    tm = 256
    return pl.pallas_call(
        _peer_finish_body,
        grid=(n // tm,),
        in_specs=[pl.BlockSpec((d, tm), lambda i: (0, tok0 // tm + i)), pl.BlockSpec((tm, d), lambda i: (i, 0)),
                  pl.BlockSpec((None, 1, d), lambda i: ((i * tm) // rows_per_mod, 0, MOD_G2))],
        out_specs=pl.BlockSpec((tm, d), lambda i: (i, 0)),
        out_shape=jax.ShapeDtypeStruct((n, d), F32),
        compiler_params=pltpu.CompilerParams(dimension_semantics=("parallel",), vmem_limit_bytes=VMEM_LIMIT),
        name="peer_finish",
    )(out_t, x, mod)


def trunk_layer(x2, b, t, mod, p, cache):
    rows_per_mod = (b * t) // mod.shape[0]
    proj2 = in_proj(x2, p['norm1_w'], mod, p['w_in_b'], rows_per_mod)
    proj3 = proj2.reshape(b, t, PROJ_W)
    latent = cache is not None
    s0 = cache[0].astype(F32) if latent else jnp.zeros((b, 2, HA, DKA, DVA), F32)
    o_f, o_b, st_a = hgrn_scan(proj3, p['lb'], s0)
    o_a = hgrn_combine(o_f, o_b, proj2, p['hgrn_norm_w'])
    o_c, st_c = ssd_mixer(proj2, b, t, p, cache[1] if latent else None)
    na_q, na_k, na_v = (proj3, GROUP_W, COL_NQ), (proj3, GROUP_W, COL_NK), (proj3, GROUP_W, COL_NV)
    if latent:
        o_m, _, _ = mla_mixer(proj2, b, t, p, (cache[2], cache[3]))
        o_d = natten_latent(proj3, p['na_rpb'], cache[4], cache[5])
        new_state = None
    else:
        o_m, ckv, krope = mla_mixer(proj2, b, t, p, None)
        o_d = attention([na_q], [([na_k], na_v)], HD, ((DHD, False),), DHD, DHD ** -0.5)
        k_na = proj3[:, :, COL_NK * GROUP_W:(COL_NK + 1) * GROUP_W].reshape(b, t, HD, DHD)
        v_na = proj3[:, :, COL_NV * GROUP_W:(COL_NV + 1) * GROUP_W].reshape(b, t, HD, DHD)
        new_state = (st_a, st_c, ckv, krope, k_na, v_na)
    n = b * t
    x2, h2 = out_proj([o_a, o_m.reshape(n, GROUP_W), o_c, o_d.reshape(n, GROUP_W)], p['w_out_b'], x2, mod,
                      p['norm2_w'], rows_per_mod)
    return x2, h2, new_state


def _final_norm_body(x_ref, w_ref, o_ref):
    x = x_ref[...]
    y = x * lax.rsqrt(jnp.mean(x * x, axis=-1, keepdims=True) + EPS)
    o_ref[...] = y * w_ref[...]


def final_rmsnorm(x, w):
    b, t, d = x.shape
    rows = 512
    out = pl.pallas_call(
        _final_norm_body,
        grid=(b * t // rows,),
        in_specs=[pl.BlockSpec((rows, d), lambda i: (i, 0)), pl.BlockSpec((1, d), lambda i: (0, 0))],
        out_specs=pl.BlockSpec((rows, d), lambda i: (i, 0)),
        out_shape=jax.ShapeDtypeStruct((b * t, d), x.dtype),
        name="final_rmsnorm",
    )(x.reshape(b * t, d), w.reshape(1, d))
    return out.reshape(b, t, d)


def kernel(x_prompt, x_sample, c, state_hgrn, state_ssd, cache_mla_ckv, cache_mla_krope, cache_na_k,
           cache_na_v, c_ctx, w_ada, b_ada, norm1_w, norm2_w, w_in, w_out, hgrn_lb_logits, hgrn_norm_w,
           mla_q_norm_w, mla_w_q_up, mla_kv_norm_w, mla_w_kv_up, ssd_conv_w, ssd_conv_b, ssd_dt_bias,
           ssd_a_log, ssd_d, ssd_norm_w, na_rpb, peer_w_q, peer_sub_keys, peer_u, peer_v, final_norm_w):
    lb_soft = jax.nn.softmax(hgrn_lb_logits.astype(F32), axis=0)
    lb_all = jnp.cumsum(lb_soft, axis=0) - lb_soft[0]
    stacked = {'w_ada': w_ada, 'b_ada': b_ada, 'norm1_w': norm1_w, 'norm2_w': norm2_w, 'w_in': w_in,
               'w_out': w_out, 'hgrn_norm_w': hgrn_norm_w, 'mla_q_norm_w': mla_q_norm_w,
               'mla_w_q_up': mla_w_q_up, 'mla_kv_norm_w': mla_kv_norm_w, 'mla_w_kv_up': mla_w_kv_up,
               'ssd_conv_w': ssd_conv_w, 'ssd_conv_b': ssd_conv_b, 'ssd_dt_bias': ssd_dt_bias,
               'ssd_a_log': ssd_a_log, 'ssd_d': ssd_d, 'ssd_norm_w': ssd_norm_w, 'na_rpb': na_rpb,
               'peer_w_q': peer_w_q, 'peer_sub_keys': peer_sub_keys, 'peer_u': peer_u, 'peer_v': peer_v}

    ub_all, vt_all = peer_prep_tables(peer_u, peer_v)
    n_ctx, n_lat = BATCH * SEQ, DEC_BATCH * DEC_SEQ
    cond = jnp.concatenate([c_ctx[None, :], c, jnp.zeros((8 - 1 - DEC_BATCH, D_MODEL), F32)], axis=0)

    xp, xs = x_prompt.reshape(n_ctx, D_MODEL), x_sample.reshape(n_lat, D_MODEL)
    ctx_states = []
    for l in range(DEPTH):
        p = {name: arr[l] for name, arr in stacked.items()}
        p['lb'] = lb_all[l]
        p['w_in_b'] = in_proj_weight(w_in[l])
        p['w_out_b'] = w_out[l].astype(BF16)
        p['mla_wq_b'], p['mla_wkv_b'] = mla_weights(mla_w_q_up[l], mla_w_kv_up[l])
        mod = adaln(cond, w_ada[l], b_ada[l])
        mod_p, mod_s = mod[0:1, None, :], mod[1:1 + DEC_BATCH, None, :]
        xp, h2p, st = trunk_layer(xp, BATCH, SEQ, mod_p, p, None)
        ctx_states.append(st)
        cache_l = (state_hgrn[:, l], state_ssd[:, l], cache_mla_ckv[:, l], cache_mla_krope[:, l],
                   cache_na_k[:, l], cache_na_v[:, l])
        xs, h2s, _ = trunk_layer(xs, DEC_BATCH, DEC_SEQ, mod_s, p, cache_l)
        out_t = peer_ffn(jnp.concatenate([h2p, h2s], axis=0), peer_w_q[l].T.astype(BF16),
                         peer_sub_keys[l].astype(BF16), ub_all[l], vt_all[l])
        xp = peer_finish(out_t, 0, xp, mod_p, n_ctx)
        xs = peer_finish(out_t, n_ctx, xs, mod_s, DEC_SEQ)

    y_prompt = final_rmsnorm(xp.reshape(BATCH, SEQ, D_MODEL), final_norm_w)
    y_sample = final_rmsnorm(xs.reshape(DEC_BATCH, DEC_SEQ, D_MODEL), final_norm_w)
    new_state_hgrn = jnp.stack([s[0] for s in ctx_states], axis=1)
    new_state_ssd = jnp.stack([s[1] for s in ctx_states], axis=1)
    new_cache_mla_ckv = jnp.stack([s[2] for s in ctx_states], axis=1)
    new_cache_mla_krope = jnp.stack([s[3] for s in ctx_states], axis=1)
    new_cache_na_k = jnp.stack([s[4] for s in ctx_states], axis=1)
    new_cache_na_v = jnp.stack([s[5] for s in ctx_states], axis=1)
    return (y_prompt, y_sample, new_state_hgrn, new_state_ssd, new_cache_mla_ckv, new_cache_mla_krope,
            new_cache_na_k, new_cache_na_v)
```

```python
import functools
import math
import jax
import jax.numpy as jnp
from jax import lax
import numpy as np
from jax.experimental import pallas as pl
from jax.experimental.pallas import tpu as pltpu

D_MODEL = 2048
BATCH = 32
SEQ = 256
DEPTH = 2
DEC_BATCH = 4
DEC_SEQ = 1024
PAST_LEN = 256

GRID_W = 64
EPS = 1e-6
ROPE_THETA = 10000.0
Q_BLOCK = 128
CHUNK = 64
N_MIXERS = 4
GROUP_W = D_MODEL // N_MIXERS
D_MIX = N_MIXERS * GROUP_W
HA = 4
DKA = GROUP_W // HA
DVA = GROUP_W // HA
LB_FLOOR = 1e-30
HB = 4
Q_RANK = D_MODEL // 4
KV_RANK = D_MODEL // 8
NOPE = 128
ROPE = 64
VB = GROUP_W // HB
MLA_SCALE = (NOPE + ROPE) ** -0.5
HC = 8
PC = GROUP_W // HC
NC = 128
GC = 2
CONV_W = 3
CONV_CH = GROUP_W + 2 * GC * NC
HD = 8
DHD = GROUP_W // HD
WIN_R = 8
WIN_C = 16
N_KEYS = 128
N_EXPERTS = N_KEYS * N_KEYS
PEER_HEADS = 8
PEER_QDIM = 256
PEER_TOPK = 16
PEER_TOKEN_BLOCK = 128

IN_SPLITS = (HA * DKA, HA * DKA, HA * DKA, HA * DVA, HA * DVA,
             Q_RANK, KV_RANK, ROPE,
             GROUP_W, GROUP_W, GC * NC, GC * NC, HC, HC,
             GROUP_W, GROUP_W, GROUP_W)
IN_W = sum(IN_SPLITS)
F32 = jnp.float32


def rmsnorm(x, w):
    xf = x.astype(F32)
    y = xf * lax.rsqrt(jnp.mean(xf * xf, axis=-1, keepdims=True) + EPS)
    return (y * w.astype(F32)).astype(x.dtype)


def split_cols(a, sizes):
    offs = []
    acc = 0
    for s in sizes[:-1]:
        acc += s
        offs.append(acc)
    return jnp.split(a, offs, axis=-1)


def flip_t(a):
    return jnp.flip(a, axis=1)


def conv_centred(x, w, b):
    ch = x.shape[-1]
    k = w.shape[0]
    y = lax.conv_general_dilated(x, w[:, None, :].astype(x.dtype), window_strides=(1,),
                                 padding=[(k // 2, k // 2)], dimension_numbers=('NWC', 'WIO', 'NWC'),
                                 feature_group_count=ch)
    return y + b.astype(x.dtype)


def rope_2d(x):
    t_len = x.shape[1]
    t = jnp.arange(t_len)
    half = ROPE // 2
    inv = 1.0 / (ROPE_THETA ** (jnp.arange(0, half, 2, dtype=F32) / half))
    rows = (t // GRID_W).astype(F32)
    cols = (t % GRID_W).astype(F32)

    def rot(xa, pos):
        ang = (pos[:, None] * inv[None, :]).reshape((t_len,) + (1,) * (x.ndim - 3) + (inv.shape[0],))
        cs, sn = jnp.cos(ang).astype(x.dtype), jnp.sin(ang).astype(x.dtype)
        x1, x2 = jnp.split(xa, 2, axis=-1)
        return jnp.concatenate([x1 * cs - x2 * sn, x2 * cs + x1 * sn], axis=-1)
    return jnp.concatenate([rot(x[..., :half], rows), rot(x[..., half:], cols)], axis=-1)


def block_attention(q, k, v, scale):
    b, tq, h, dq = q.shape
    nb = tq // Q_BLOCK
    qb = q.reshape(b, nb, Q_BLOCK, h, dq).transpose(1, 0, 2, 3, 4)

    def one(qi):
        s = jnp.einsum('bqhd,bkhd->bhqk', qi, k).astype(F32) * scale
        pr = jax.nn.softmax(s, axis=-1).astype(v.dtype)
        return jnp.einsum('bhqk,bkhd->bqhd', pr, v)
    o = lax.map(one, qb)
    return o.transpose(1, 0, 2, 3, 4).reshape(b, tq, h, v.shape[-1])


def masked_exp(mask, diff):
    return jnp.where(mask, jnp.exp(jnp.where(mask, diff, 0.0)), 0.0)


def gla_chunk_scan(q, k, v, logf, s0):
    b, t, h, dk = q.shape
    dv = v.shape[-1]
    n = t // CHUNK

    def chunks(a):
        return a.reshape(b, n, CHUNK, h, a.shape[-1]).transpose(1, 0, 3, 2, 4)
    cum = jnp.cumsum(chunks(logf), axis=3)
    mask = jnp.tril(jnp.ones((CHUNK, CHUNK), dtype=bool))[:, :, None]

    def step(state, inp):
        qi, ki, vi, ci = inp
        decay = masked_exp(mask, ci[:, :, :, None, :] - ci[:, :, None, :, :])
        att = jnp.einsum('bhtk,bhsk,bhtsk->bhts', qi, ki, decay)
        o = jnp.einsum('bhts,bhsv->bhtv', att, vi)
        o = o + jnp.einsum('bhtk,bhkv->bhtv', qi * jnp.exp(ci), state)
        last = ci[:, :, -1]
        state = jnp.exp(last)[..., None] * state + jnp.einsum(
            'bhsk,bhsv->bhkv', ki * jnp.exp(last[:, :, None, :] - ci), vi)
        return state, o
    s_fin, o = lax.scan(step, s0, (chunks(q), chunks(k), chunks(v), cum))
    return o.transpose(1, 0, 3, 2, 4).reshape(b, t, h, dv), s_fin


def ssd_chunk_scan(x, dt, a, bm, cm, s0):
    b, t, h, p = x.shape
    g, n_st = bm.shape[2], bm.shape[3]
    r = h // g
    n = t // CHUNK
    xdt = (x * dt[..., None]).reshape(b, n, CHUNK, g, r, p).transpose(1, 0, 2, 3, 4, 5)
    la = (dt * a).reshape(b, n, CHUNK, g, r).transpose(1, 0, 2, 3, 4)
    bc = bm.reshape(b, n, CHUNK, g, n_st).transpose(1, 0, 2, 3, 4)
    cc = cm.reshape(b, n, CHUNK, g, n_st).transpose(1, 0, 2, 3, 4)
    mask = jnp.tril(jnp.ones((CHUNK, CHUNK), dtype=bool))[None, :, :, None, None]

    def step(state, inp):
        xi, ai, bi, ci = inp
        cum = jnp.cumsum(ai, axis=1)
        lmat = masked_exp(mask, cum[:, :, None] - cum[:, None, :])
        cb = jnp.einsum('btgn,bsgn->btsg', ci, bi)
        y = jnp.einsum('btsg,btsgr,bsgrp->btgrp', cb, lmat, xi)
        y = y + jnp.einsum('btgn,bgrpn->btgrp', ci, state) * jnp.exp(cum)[..., None]
        last = cum[:, -1]
        w = jnp.exp(last[:, None] - cum)
        state = jnp.exp(last)[..., None, None] * state + jnp.einsum('bsgn,bsgr,bsgrp->bgrpn', bi, w, xi)
        return state, y
    s_fin, y = lax.scan(step, s0.reshape(b, g, r, p, n_st), (xdt, la, bc, cc))
    return y.transpose(1, 0, 2, 3, 4, 5).reshape(b, t, h, p), s_fin.reshape(b, h, p, n_st)


HG_SUB = 16
HG_NSUB = CHUNK // HG_SUB


def _split3_bf16(x):
    hi = x.astype(BF16)
    r1 = x - hi.astype(F32)
    mid = r1.astype(BF16)
    lo = (r1 - mid.astype(F32)).astype(BF16)
    return hi, mid, lo


def _hgrn_direction(rev, q_ref, f_ref, v_ref, la, lc, om, st_s, d, o_ref):
    fx = f_ref[...]
    logf = jnp.logaddexp(la, lc + jax.nn.log_sigmoid(fx))
    kk = om * jax.nn.sigmoid(-fx)
    qq = jax.nn.silu(q_ref[...]) * (DKA ** -0.5)
    vv = v_ref[...]
    t_io = lax.broadcasted_iota(jnp.int32, (CHUNK, CHUNK), 0)
    s_io = lax.broadcasted_iota(jnp.int32, (CHUNK, CHUNK), 1)
    tri = jnp.where((s_io >= t_io) if rev else (s_io <= t_io), 1.0, 0.0).astype(BF16)
    c = sum(jnp.dot(tri, part, preferred_element_type=F32) for part in _split3_bf16(logf))
    row = lax.broadcasted_iota(jnp.int32, (CHUNK, 1), 0)
    sub_row = lax.broadcasted_iota(jnp.int32, (HG_SUB, 1), 0)
    lane = lax.broadcasted_iota(jnp.int32, (HG_SUB, CHUNK), 1)
    outs = []
    for h in range(HA):
        cs = slice(h * DKA, (h + 1) * DKA)
        ch, qh, kh, vh = c[:, cs], qq[:, cs], kk[:, cs], vv[:, cs]
        st = st_s[d, h]
        o = _dot_nt((qh * jnp.exp(ch)).astype(BF16), st.astype(BF16))
        att = jnp.zeros((CHUNK, CHUNK), F32)
        for i in range(HG_NSUB):
            if rev:
                if i == HG_NSUB - 1:
                    continue
                c_ref = ch[(i + 1) * HG_SUB:(i + 1) * HG_SUB + 1]
                k_side = row >= (i + 1) * HG_SUB
            else:
                if i == 0:
                    continue
                c_ref = ch[i * HG_SUB - 1:i * HG_SUB]
                k_side = row < i * HG_SUB
            q_side = (row >= i * HG_SUB) & (row < (i + 1) * HG_SUB)
            qs = jnp.where(q_side, qh * jnp.exp(jnp.where(q_side, ch - c_ref, 0.0)), 0.0)
            ks = jnp.where(k_side, kh * jnp.exp(jnp.where(k_side, c_ref - ch, 0.0)), 0.0)
            att = att + _dot_nt(qs.astype(BF16), ks.astype(BF16))
        strips = []
        for i in range(HG_NSUB):
            blk = slice(i * HG_SUB, (i + 1) * HG_SUB)
            cb, qb, kb = ch[blk], qh[blk], kh[blk]
            strip = jnp.zeros((HG_SUB, CHUNK), F32)
            for s in range(HG_SUB):
                causal = (sub_row <= s) if rev else (sub_row >= s)
                w = jnp.exp(jnp.where(causal, cb - cb[s:s + 1], 0.0))
                col = jnp.sum(jnp.where(causal, w * qb * kb[s:s + 1], 0.0), axis=-1, keepdims=True)
                strip = jnp.where(lane == i * HG_SUB + s, col, strip)
            strips.append(strip)
        att = att + jnp.concatenate(strips, axis=0)
        o = o + jnp.dot(att.astype(BF16), vh.astype(BF16), preferred_element_type=F32)
        outs.append(o)
        c_end = ch[0:1] if rev else ch[CHUNK - 1:CHUNK]
        kd = kh * jnp.exp(c_end - ch)
        st_s[d, h] = st * jnp.exp(c_end) + _dot_tn(vh.astype(BF16), kd.astype(BF16))
    o_ref[...] = jnp.concatenate(outs, axis=-1)


def _hgrn_body(qf_ref, ff_ref, vf_ref, qb_ref, fb_ref, vb_ref, la_ref, lc_ref, om_ref, s0_ref,
               of_ref, ob_ref, sfin_ref, st_s):
    i = pl.program_id(1)

    @pl.when(i == 0)
    def _():
        for d in range(2):
            for h in range(HA):
                st_s[d, h] = s0_ref[d, h].T

    _hgrn_direction(False, qf_ref, ff_ref, vf_ref, la_ref[0:1], lc_ref[0:1], om_ref[0:1], st_s, 0, of_ref)
    _hgrn_direction(True, qb_ref, fb_ref, vb_ref, la_ref[1:2], lc_ref[1:2], om_ref[1:2], st_s, 1, ob_ref)

    @pl.when(i == pl.num_programs(1) - 1)
    def _():
        for d in range(2):
            for h in range(HA):
                sfin_ref[d, h] = st_s[d, h].T


def hgrn_scan(proj3, lb, s0):
    b, t, _ = proj3.shape
    w = HA * DKA
    n = t // CHUNK
    lb = lb.astype(F32)
    la = jnp.log(jnp.maximum(lb, LB_FLOOR))
    lc = jnp.log1p(-lb)
    om = 1.0 - lb

    def fwd_col(c):
        return pl.BlockSpec((None, CHUNK, w), lambda bi, i: (bi, i, c))

    def bwd_col(c):
        return pl.BlockSpec((None, CHUNK, w), lambda bi, i: (bi, n - 1 - i, c))
    fwd, bwd = fwd_col(0), bwd_col(0)
    par = pl.BlockSpec((2, w), lambda bi, i: (0, 0))
    st = pl.BlockSpec((None, 2, HA, DKA, DVA), lambda bi, i: (bi, 0, 0, 0, 0))
    q, f_fwd, f_bwd, v = proj3, proj3, proj3, proj3
    return pl.pallas_call(
        _hgrn_body,
        grid=(b, n),
        in_specs=[fwd_col(COL_HQ), fwd_col(COL_HFF), fwd_col(COL_HI), bwd_col(COL_HQ), bwd_col(COL_HFB),
                  bwd_col(COL_HI), par, par, par, st],
        out_specs=[fwd, bwd, st],
        out_shape=[jax.ShapeDtypeStruct((b, t, w), F32), jax.ShapeDtypeStruct((b, t, w), F32),
                   jax.ShapeDtypeStruct((b, 2, HA, DKA, DVA), F32)],
        scratch_shapes=[pltpu.VMEM((2, HA, DVA, DKA), F32)],
        compiler_params=pltpu.CompilerParams(dimension_semantics=("parallel", "arbitrary"),
                                             vmem_limit_bytes=VMEM_LIMIT),
        name="hgrn_scan",
    )(q, f_fwd, v, q, f_bwd, v, la, lc, om, s0)


def _hgrn_combine_body(of_ref, ob_ref, g_ref, w_ref, o_ref):
    y = of_ref[...] + ob_ref[...]
    g = g_ref[...]
    outs = []
    for h in range(HA):
        cs = slice(h * DVA, (h + 1) * DVA)
        yh = y[:, cs]
        yn = yh * lax.rsqrt(jnp.mean(yh * yh, axis=-1, keepdims=True) + EPS)
        outs.append(yn * w_ref[:, cs] * jax.nn.silu(g[:, cs]))
    o_ref[...] = jnp.concatenate(outs, axis=-1)


def hgrn_combine(o_f, o_b, proj2, norm_w):
    b, t, w = o_f.shape
    rows = 256
    blk = pl.BlockSpec((rows, w), lambda i: (i, 0))
    return pl.pallas_call(
        _hgrn_combine_body,
        grid=(b * t // rows,),
        in_specs=[blk, blk, pl.BlockSpec((rows, w), lambda i: (i, COL_HG)), pl.BlockSpec((1, w), lambda i: (0, 0))],
        out_specs=blk,
        out_shape=jax.ShapeDtypeStruct((b * t, w), F32),
        compiler_params=pltpu.CompilerParams(dimension_semantics=("parallel",)),
        name="hgrn_combine",
    )(o_f.reshape(b * t, w), o_b.reshape(b * t, w), proj2, norm_w.astype(F32).reshape(1, w))


def ssd_mixer(parts, conv_w, conv_b, dt_bias, a_log, d_skip, norm_w, init_state):
    x, z, b_in, c_in, dt_fwd, dt_bwd = parts
    b, t, _ = x.shape
    xbc = jax.nn.silu(conv_centred(jnp.concatenate([x, b_in, c_in], axis=-1), conv_w, conv_b))
    xs, bs, cs = split_cols(xbc, (GROUP_W, GC * NC, GC * NC))
    xh = xs.astype(F32).reshape(b, t, HC, PC)
    bg = bs.astype(F32).reshape(b, t, GC, NC)
    cg = cs.astype(F32).reshape(b, t, GC, NC)
    a = -jnp.exp(a_log.astype(F32))
    y = d_skip.astype(F32)[:, None] * xh
    finals = []
    for d, dt_raw in enumerate((dt_fwd, dt_bwd)):
        dt = jax.nn.softplus(dt_raw.astype(F32) + dt_bias[d].astype(F32))
        if init_state is None:
            s0 = jnp.zeros((b, HC, PC, NC), F32)
        else:
            s0 = init_state[:, d].astype(F32)
        if d == 0:
            yd, sf = ssd_chunk_scan(xh, dt, a[d], bg, cg, s0)
        else:
            yd, sf = ssd_chunk_scan(flip_t(xh), flip_t(dt), a[d], flip_t(bg), flip_t(cg), s0)
            yd = flip_t(yd)
        y = y + yd
        finals.append(sf)
    y = y.reshape(b, t, GROUP_W) * jax.nn.silu(z.astype(F32))
    return rmsnorm(y, norm_w).astype(x.dtype), jnp.stack(finals, axis=1)


def mla_project(parts, p):
    cq, ckv, krope = parts
    b, t, _ = cq.shape
    q = (rmsnorm(cq, p['mla_q_norm_w']) @ p['mla_w_q_up']).reshape(b, t, HB, NOPE + ROPE)
    return q[..., :NOPE], q[..., NOPE:], rmsnorm(ckv, p['mla_kv_norm_w']), krope


def mla_keys_values(ckv, krope, w_kv_up):
    b, t, _ = ckv.shape
    kv = (ckv @ w_kv_up).reshape(b, t, HB, NOPE + VB)
    k = jnp.concatenate([kv[..., :NOPE], jnp.broadcast_to(krope[:, :, None, :], (b, t, HB, ROPE))], axis=-1)
    return k, kv[..., NOPE:]


def mla_context(parts, p):
    q_nope, q_rope, ckv, krope = mla_project(parts, p)
    k, v = mla_keys_values(ckv, krope, p['mla_w_kv_up'])
    o = block_attention(jnp.concatenate([q_nope, q_rope], axis=-1), k, v, MLA_SCALE)
    b, t = o.shape[:2]
    return o.reshape(b, t, HB * VB), ckv, krope


def mla_latent(parts, p, ckv_ctx, krope_ctx):
    q_nope, q_rope, ckv, krope = mla_project(parts, p)
    q = jnp.concatenate([q_nope, rope_2d(q_rope)], axis=-1)
    k_lat, v_lat = mla_keys_values(ckv, rope_2d(krope), p['mla_w_kv_up'])
    k_ctx, v_ctx = mla_keys_values(ckv_ctx.astype(ckv.dtype), krope_ctx.astype(krope.dtype), p['mla_w_kv_up'])
    o = block_attention(q, jnp.concatenate([k_lat, k_ctx], axis=1), jnp.concatenate([v_lat, v_ctx], axis=1),
                        MLA_SCALE)
    b, t = o.shape[:2]
    return o.reshape(b, t, HB * VB)


def natten_context(parts):
    b, t, _ = parts[0].shape
    q, k, v = [a.reshape(b, t, HD, DHD) for a in parts]
    o = block_attention(q, k, v, DHD ** -0.5)
    return o.reshape(b, t, GROUP_W), k, v


BF16 = jnp.bfloat16
LANES = 128
VMEM_LIMIT = 56 * 1024 * 1024
NEG_BIG = -1e30

NA_ROWS = DEC_SEQ // GRID_W
NA_WR = min(WIN_R, NA_ROWS)
NA_WIN = NA_WR * GRID_W


def _dot_nt(a, b):
    return lax.dot_general(a, b, (((1,), (1,)), ((), ())), preferred_element_type=F32)


def _dot_tn(a, b):
    return lax.dot_general(a, b, (((0,), (0,)), ((), ())), preferred_element_type=F32)


def natten_bias_table(rpb):
    cols = jnp.arange(GRID_W)
    start = jnp.clip(cols - WIN_C // 2, 0, GRID_W - WIN_C)
    in_win = (cols[None, :] >= start[:, None]) & (cols[None, :] < start[:, None] + WIN_C)
    c_off = jnp.clip(cols[None, :] - cols[:, None] + (WIN_C - 1), 0, 2 * WIN_C - 2)
    r_off = jnp.arange(NA_WR)[:, None] - (NA_WR - 1) + jnp.arange(NA_WR)[None, :] + (WIN_R - 1)
    tab = rpb.astype(F32)[:, r_off][:, :, :, c_off]
    tab = jnp.where(in_win[None, None, None], tab, NEG_BIG)
    return tab.transpose(0, 1, 3, 2, 4).reshape(HD, NA_WR, GRID_W, NA_WIN)


def _natten_body(q_ref, k_ref, v_ref, kc_ref, vc_ref, bias_ref, o_ref):
    r = pl.program_id(1)
    rs = jnp.clip(r - NA_WR // 2, 0, NA_ROWS - NA_WR)
    win = pl.ds(pl.multiple_of(rs * GRID_W, GRID_W), NA_WIN)
    q = q_ref[...] * (DHD ** -0.5)
    outs = []
    for h in range(HD):
        cs = slice(h * DHD, (h + 1) * DHD)
        qh = q[:, cs].astype(BF16)
        s_loc = _dot_nt(qh, k_ref[win, cs].astype(BF16)) + bias_ref[h]
        s_ctx = _dot_nt(qh, kc_ref[:, cs].astype(BF16))
        m = jnp.maximum(jnp.max(s_loc, axis=-1, keepdims=True), jnp.max(s_ctx, axis=-1, keepdims=True))
        p_loc = jnp.exp(s_loc - m)
        p_ctx = jnp.exp(s_ctx - m)
        inv = 1.0 / (jnp.sum(p_loc, axis=-1, keepdims=True) + jnp.sum(p_ctx, axis=-1, keepdims=True))
        outs.append(jnp.dot((p_loc * inv).astype(BF16), v_ref[win, cs].astype(BF16), preferred_element_type=F32)
                    + jnp.dot((p_ctx * inv).astype(BF16), vc_ref[:, cs].astype(BF16),
                              preferred_element_type=F32))
    o_ref[...] = jnp.concatenate(outs, axis=-1)


def natten_latent(proj3, rpb, k_ctx, v_ctx):
    b, t, _ = proj3.shape
    w = GROUP_W
    assert t == DEC_SEQ
    n_ctx = k_ctx.shape[1]
    bias = natten_bias_table(rpb)
    q = k = v = proj3

    def delta_idx(r):
        return jnp.clip(r - NA_WR // 2, 0, NA_ROWS - NA_WR) - r + (NA_WR - 1)
    ctx = pl.BlockSpec((None, n_ctx, w), lambda bi, r: (bi, 0, 0))
    return pl.pallas_call(
        _natten_body,
        grid=(b, NA_ROWS),
        in_specs=[pl.BlockSpec((None, GRID_W, w), lambda bi, r: (bi, r, COL_NQ)),
                  pl.BlockSpec((None, t, w), lambda bi, r: (bi, 0, COL_NK)),
                  pl.BlockSpec((None, t, w), lambda bi, r: (bi, 0, COL_NV)), ctx, ctx,
                  pl.BlockSpec((HD, None, GRID_W, NA_WIN), lambda bi, r: (0, delta_idx(r), 0, 0))],
        out_specs=pl.BlockSpec((None, GRID_W, w), lambda bi, r: (bi, r, 0)),
        out_shape=jax.ShapeDtypeStruct((b, t, w), F32),
        compiler_params=pltpu.CompilerParams(dimension_semantics=("parallel", "arbitrary"),
                                             vmem_limit_bytes=VMEM_LIMIT),
        name="natten_latent",
    )(q, k, v, k_ctx.reshape(b, n_ctx, w), v_ctx.reshape(b, n_ctx, w), bias)


PEER_KDIM = PEER_QDIM // 2
PEER_HP = PEER_HEADS * 2
PEER_PREP_TE = 512
PEER_SCORE_TM = 512
PEER_SELECT_TM = 256
PEER_TM = 512
PEER_TE = 1024
PEER_ROWS = PEER_TE // N_KEYS
PEER_CAND = [(a, b) for a in range(PEER_TOPK) for b in range(PEER_TOPK) if (a + 1) * (b + 1) <= PEER_TOPK]


def _peer_prep_body(u_ref, v_ref, ub_ref, vt_ref):
    ub_ref[...] = u_ref[...].astype(BF16)
    vt_ref[...] = v_ref[...].T.astype(BF16)


def peer_prep_tables(peer_u, peer_v):
    depth, n_e, d = peer_u.shape
    te = PEER_PREP_TE
    return pl.pallas_call(
        _peer_prep_body,
        grid=(depth, n_e // te),
        in_specs=[pl.BlockSpec((None, te, d), lambda l, j: (l, j, 0)),
                  pl.BlockSpec((None, te, d), lambda l, j: (l, j, 0))],
        out_specs=[pl.BlockSpec((None, te, d), lambda l, j: (l, j, 0)),
                   pl.BlockSpec((None, d, te), lambda l, j: (l, 0, j))],
        out_shape=[jax.ShapeDtypeStruct((depth, n_e, d), BF16), jax.ShapeDtypeStruct((depth, d, n_e), BF16)],
        compiler_params=pltpu.CompilerParams(dimension_semantics=("parallel", "parallel"),
                                             vmem_limit_bytes=VMEM_LIMIT),
        name="peer_prep_tables",
    )(peer_u, peer_v)


def _peer_score_body(ha_ref, hb_ref, wqt_ref, sk_ref, xt_ref, st_ref, *, tiles_a):
    h = jnp.where(pl.program_id(0) < tiles_a, ha_ref[...], hb_ref[...])
    xt = h.T.astype(BF16)
    xt_ref[...] = xt
    qt = jnp.dot(wqt_ref[...], xt, preferred_element_type=F32).astype(BF16)
    for hp in range(PEER_HP):
        st_ref[hp] = jnp.dot(sk_ref[hp % 2], qt[hp * PEER_KDIM:(hp + 1) * PEER_KDIM, :],
                             preferred_element_type=F32)


def peer_scores(h_a, h_b, wq_t, sub_keys):
    n_a, d = h_a.shape
    n = n_a + h_b.shape[0]
    tm = PEER_SCORE_TM
    tiles_a = n_a // tm
    return pl.pallas_call(
        functools.partial(_peer_score_body, tiles_a=tiles_a),
        grid=(n // tm,),
        in_specs=[pl.BlockSpec((tm, d), lambda i: (jnp.minimum(i, tiles_a - 1), 0)),
                  pl.BlockSpec((tm, d), lambda i: (jnp.maximum(i - tiles_a, 0), 0)),
                  pl.BlockSpec(wq_t.shape, lambda i: (0, 0)),
                  pl.BlockSpec(sub_keys.shape, lambda i: (0, 0, 0))],
        out_specs=[pl.BlockSpec((d, tm), lambda i: (0, i)),
                   pl.BlockSpec((PEER_HP, N_KEYS, tm), lambda i: (0, 0, i))],
        out_shape=[jax.ShapeDtypeStruct((d, n), BF16), jax.ShapeDtypeStruct((PEER_HP, N_KEYS, n), F32)],
        compiler_params=pltpu.CompilerParams(dimension_semantics=("parallel",), vmem_limit_bytes=VMEM_LIMIT),
        name="peer_scores",
    )(h_a, h_b, wq_t, sub_keys)


def _peer_select_body(st_ref, e1_ref, cnt1_ref, rank2_ref, e2_ref, rank1_s, vtop_s, cnt_s, zinv_s):
    tm = st_ref.shape[-1]
    kio = lax.broadcasted_iota(jnp.int32, (N_KEYS, LANES), 0).astype(F32)
    neg = jnp.float32(-jnp.inf)

    def group(g, carry):
        ln = pl.ds(pl.multiple_of(g * LANES, LANES), LANES)

        for hp in range(PEER_HP):
            h, part = hp // 2, hp % 2

            def extract(it, sr):
                s, rank = sr
                m = jnp.max(s, axis=0, keepdims=True)
                first = jnp.min(jnp.where(s == m, kio, float(N_KEYS)), axis=0, keepdims=True)
                sel = kio == first
                vtop_s[part, it, h:h + 1, ln] = m
                return jnp.where(sel, neg, s), jnp.where(sel, it, rank)
            _, rank = lax.fori_loop(0, PEER_TOPK, extract,
                                    (st_ref[hp, :, ln], jnp.full((N_KEYS, LANES), PEER_TOPK, jnp.int32)))
            rank = rank.astype(F32)
            if part == 0:
                rank1_s[h, :, ln] = rank
            else:
                rank2_ref[h, :, ln] = rank

        v1 = [vtop_s[0, a, :, ln] for a in range(PEER_TOPK)]
        v2 = [vtop_s[1, b, :, ln] for b in range(PEER_TOPK)]
        sums = [v1[a] + v2[b] for a, b in PEER_CAND]
        n_c = len(PEER_CAND)
        beaten = [jnp.zeros((PEER_HEADS, LANES), F32) for _ in range(n_c)]
        for i in range(n_c):
            for j in range(i + 1, n_c):
                ge = sums[i] >= sums[j]
                beaten[j] = beaten[j] + jnp.where(ge, 1.0, 0.0)
                beaten[i] = beaten[i] + jnp.where(ge, 0.0, 1.0)
        z = jnp.zeros((PEER_HEADS, LANES), F32)
        cnt = [jnp.zeros((PEER_HEADS, LANES), F32) for _ in range(PEER_TOPK)]
        for i, (a, b) in enumerate(PEER_CAND):
            keep = beaten[i] < float(PEER_TOPK)
            z = z + jnp.where(keep, jnp.exp(sums[i] - sums[0]), 0.0)
            cnt[a] = cnt[a] + jnp.where(keep, 1.0, 0.0)
        for a in range(PEER_TOPK):
            cnt_s[a, :, ln] = cnt[a]
        zinv_s[:, ln] = 1.0 / z

        for h in range(PEER_HEADS):
            rank1 = rank1_s[h, :, ln]
            s1 = st_ref[2 * h, :, ln]
            e1 = jnp.exp(s1 - vtop_s[0, 0, h:h + 1, ln]) * zinv_s[h:h + 1, ln]
            e1_ref[h, :, ln] = jnp.where(rank1 < float(PEER_TOPK), e1, 0.0)
            c1 = jnp.zeros((N_KEYS, LANES), F32)
            for a in range(PEER_TOPK):
                c1 = c1 + jnp.where(rank1 == float(a), cnt_s[a, h:h + 1, ln], 0.0)
            cnt1_ref[h, :, ln] = c1
            e2_ref[h, :, ln] = jnp.exp(st_ref[2 * h + 1, :, ln] - vtop_s[1, 0, h:h + 1, ln])
        return carry
    lax.fori_loop(0, tm // LANES, group, 0)


def peer_select(st):
    _, _, n = st.shape
    tm = PEER_SELECT_TM
    spec = pl.BlockSpec((PEER_HEADS, N_KEYS, tm), lambda i: (0, 0, i))
    shp = jax.ShapeDtypeStruct((PEER_HEADS, N_KEYS, n), F32)
    return pl.pallas_call(
        _peer_select_body,
        grid=(n // tm,),
        in_specs=[pl.BlockSpec((PEER_HP, N_KEYS, tm), lambda i: (0, 0, i))],
        out_specs=[spec, spec, spec, spec],
        out_shape=[shp, shp, shp, shp],
        scratch_shapes=[pltpu.VMEM((PEER_HEADS, N_KEYS, tm), F32),
                        pltpu.VMEM((2, PEER_TOPK, PEER_HEADS, tm), F32),
                        pltpu.VMEM((PEER_TOPK, PEER_HEADS, tm), F32),
                        pltpu.VMEM((PEER_HEADS, tm), F32)],
        compiler_params=pltpu.CompilerParams(dimension_semantics=("parallel",), vmem_limit_bytes=VMEM_LIMIT),
        name="peer_select",
    )(st)


def _gelu_tanh(x):
    return 0.5 * x * (1.0 + jnp.tanh(math.sqrt(2.0 / math.pi) * (x + 0.044715 * (x * x * x))))


PEER_SUB_ROWS = 4
PEER_GATE_ROWS = 32


def _peer_expert_body(u_ref, vt_ref, xt_ref, e1_ref, cnt1_ref, rank2_ref, e2_ref, o_ref, st_s, at_s):
    j = pl.program_id(1)
    tm = xt_ref.shape[-1]

    @pl.when(j == 0)
    def _():
        o_ref[...] = jnp.zeros_like(o_ref)

    for c in range(PEER_ROWS // PEER_SUB_ROWS):
        sub = slice(c * PEER_SUB_ROWS * N_KEYS, (c + 1) * PEER_SUB_ROWS * N_KEYS)
        st_s[sub, :] = jnp.dot(u_ref[sub, :], xt_ref[...], preferred_element_type=F32)
        sub_rows = range(c * PEER_SUB_ROWS, (c + 1) * PEER_SUB_ROWS)
        n_blk = N_KEYS // PEER_GATE_ROWS

        def gate_block(idx, carry):
            ln = pl.ds(pl.multiple_of((idx // n_blk) * LANES, LANES), LANES)
            col0 = pl.multiple_of((idx % n_blk) * PEER_GATE_ROWS, PEER_GATE_ROWS)
            cols = pl.ds(col0, PEER_GATE_ROWS)
            gate = {r: jnp.zeros((PEER_GATE_ROWS, LANES), F32) for r in sub_rows}
            for h in range(PEER_HEADS):
                rank2 = rank2_ref[h, cols, ln]
                e2_bits = pltpu.bitcast(e2_ref[h, cols, ln], jnp.int32)
                for r in sub_rows:
                    keep = lax.shift_right_arithmetic(
                        pltpu.bitcast(rank2 - cnt1_ref[h, r:r + 1, ln], jnp.int32), 31)
                    gate[r] = gate[r] + pltpu.bitcast(e2_bits & keep, F32) * e1_ref[h, r:r + 1, ln]
            for r in sub_rows:
                rows = pl.ds(r * N_KEYS + col0, PEER_GATE_ROWS)
                at_s[rows, ln] = (_gelu_tanh(st_s[rows, ln]) * gate[r]).astype(BF16)
            return carry
        lax.fori_loop(0, (tm // LANES) * n_blk, gate_block, 0)
        o_ref[...] += jnp.dot(vt_ref[:, sub], at_s[sub, :], preferred_element_type=F32)


def peer_experts(ub, vt, xt, e1, cnt1, rank2, e2):
    n_e, d = ub.shape
    n = xt.shape[1]
    tm, te = PEER_TM, PEER_TE
    row_spec = pl.BlockSpec((PEER_HEADS, PEER_ROWS, tm), lambda i, j: (0, j, i))
    col_spec = pl.BlockSpec((PEER_HEADS, N_KEYS, tm), lambda i, j: (0, 0, i))
    return pl.pallas_call(
        _peer_expert_body,
        grid=(n // tm, n_e // te),
        in_specs=[pl.BlockSpec((te, d), lambda i, j: (j, 0)),
                  pl.BlockSpec((d, te), lambda i, j: (0, j)),
                  pl.BlockSpec((d, tm), lambda i, j: (0, i)),
                  row_spec, row_spec, col_spec, col_spec],
        out_specs=pl.BlockSpec((d, tm), lambda i, j: (0, i)),
        out_shape=jax.ShapeDtypeStruct((d, n), F32),
        scratch_shapes=[pltpu.VMEM((te, tm), F32), pltpu.VMEM((te, tm), BF16)],
        compiler_params=pltpu.CompilerParams(dimension_semantics=("parallel", "arbitrary"),
                                             vmem_limit_bytes=VMEM_LIMIT),
        name="peer_experts",
    )(ub, vt, xt, e1, cnt1, rank2, e2)


def peer_ffn(h_a, h_b, wq_t, sub_keys_b, ub, vt):
    xt, st = peer_scores(h_a, h_b, wq_t, sub_keys_b)
    e1, cnt1, rank2, e2 = peer_select(st)
    return peer_experts(ub, vt, xt, e1, cnt1, rank2, e2)


PROJ_ORDER = (0, 1, 2, 3, 4, 5, 8, 9, 14, 15, 16, 6, 10, 11, 7, 12, 13)
PROJ_W = 6528
COL_HQ, COL_HFF, COL_HFB, COL_HI, COL_HG, COL_CQ, COL_SX, COL_SZ, COL_NQ, COL_NK, COL_NV = range(11)
COL_CKV, COL_SB, COL_SC = 22, 23, 24
COL_SMALL = 50
SMALL_DT = ROPE
PROJ_TM = 512
PROJ_TN = 2176
MOD_SH1, MOD_SC1, MOD_G1, MOD_SH2, MOD_SC2, MOD_G2 = range(6)


def in_proj_weight(w_in):
    offs = np.cumsum((0,) + IN_SPLITS)
    cols = [w_in[:, offs[k]:offs[k + 1]] for k in PROJ_ORDER]
    cols.append(jnp.zeros((w_in.shape[0], PROJ_W - IN_W), w_in.dtype))
    return jnp.concatenate(cols, axis=1).astype(BF16)


def _adaln_body(c_ref, w_ref, b_ref, o_ref):
    a = jax.nn.silu(c_ref[...]).astype(BF16)
    o_ref[...] = jnp.dot(a, w_ref[...].astype(BF16), preferred_element_type=F32) + b_ref[...]


def adaln(cond, w_ada, b_ada):
    r, d = cond.shape
    n = w_ada.shape[1]
    tn = 1024
    return pl.pallas_call(
        _adaln_body,
        grid=(n // tn,),
        in_specs=[pl.BlockSpec((r, d), lambda j: (0, 0)), pl.BlockSpec((d, tn), lambda j: (0, j)),
                  pl.BlockSpec((1, tn), lambda j: (0, j))],
        out_specs=pl.BlockSpec((r, tn), lambda j: (0, j)),
        out_shape=jax.ShapeDtypeStruct((r, n), F32),
        compiler_params=pltpu.CompilerParams(dimension_semantics=("parallel",), vmem_limit_bytes=VMEM_LIMIT),
        name="adaln",
    )(cond, w_ada, b_ada.reshape(1, n))


def _rms_rows(x, w):
    return x * lax.rsqrt(jnp.mean(x * x, axis=-1, keepdims=True) + EPS) * w


def _in_proj_body(x_ref, nw_ref, sc_ref, sh_ref, w_ref, o_ref, h_s):
    @pl.when(pl.program_id(1) == 0)
    def _():
        h_s[...] = (_rms_rows(x_ref[...], nw_ref[...]) * (1.0 + sc_ref[...]) + sh_ref[...]).astype(BF16)
    o_ref[...] = jnp.dot(h_s[...], w_ref[...], preferred_element_type=F32)


def in_proj(x, norm_w, mod, w_perm, rows_per_mod):
    n, d = x.shape
    tm, tn = PROJ_TM, PROJ_TN

    def mod_spec(part):
        return pl.BlockSpec((None, 1, d), lambda i, j: ((i * tm) // rows_per_mod, 0, part))
    return pl.pallas_call(
        _in_proj_body,
        grid=(n // tm, PROJ_W // tn),
        in_specs=[pl.BlockSpec((tm, d), lambda i, j: (i, 0)), pl.BlockSpec((1, d), lambda i, j: (0, 0)),
                  mod_spec(MOD_SC1), mod_spec(MOD_SH1), pl.BlockSpec((d, tn), lambda i, j: (0, j))],
        out_specs=pl.BlockSpec((tm, tn), lambda i, j: (i, j)),
        out_shape=jax.ShapeDtypeStruct((n, PROJ_W), F32),
        scratch_shapes=[pltpu.VMEM((tm, d), BF16)],
        compiler_params=pltpu.CompilerParams(dimension_semantics=("parallel", "arbitrary"),
                                             vmem_limit_bytes=VMEM_LIMIT),
        name="in_proj",
    )(x, norm_w.reshape(1, d), mod, mod, w_perm)


def _out_proj_body(oa_ref, ob_ref, oc_ref, od_ref, w_ref, x_ref, g_ref, sc_ref, sh_ref, nw_ref, xo_ref, h_ref):
    mix = None
    for k, o_ref in enumerate((oa_ref, ob_ref, oc_ref, od_ref)):
        t = jnp.dot(o_ref[...].astype(BF16), w_ref[k * GROUP_W:(k + 1) * GROUP_W, :], preferred_element_type=F32)
        mix = t if mix is None else mix + t
    x = x_ref[...] + g_ref[...] * mix
    xo_ref[...] = x
    h_ref[...] = _rms_rows(x, nw_ref[...]) * (1.0 + sc_ref[...]) + sh_ref[...]


def out_proj(o_parts, w_out_b, x, mod, norm2_w, rows_per_mod):
    n, d = x.shape
    tm = 256

    def mod_spec(part):
        return pl.BlockSpec((None, 1, d), lambda i: ((i * tm) // rows_per_mod, 0, part))
    part = pl.BlockSpec((tm, GROUP_W), lambda i: (i, 0))
    row = pl.BlockSpec((tm, d), lambda i: (i, 0))
    return pl.pallas_call(
        _out_proj_body,
        grid=(n // tm,),
        in_specs=[part, part, part, part, pl.BlockSpec((D_MIX, d), lambda i: (0, 0)), row,
                  mod_spec(MOD_G1), mod_spec(MOD_SC2), mod_spec(MOD_SH2), pl.BlockSpec((1, d), lambda i: (0, 0))],
        out_specs=[row, row],
        out_shape=[jax.ShapeDtypeStruct((n, d), F32), jax.ShapeDtypeStruct((n, d), F32)],
        compiler_params=pltpu.CompilerParams(dimension_semantics=("parallel",), vmem_limit_bytes=VMEM_LIMIT),
        name="out_proj",
    )(*o_parts, w_out_b, x, mod, mod, mod, norm2_w.reshape(1, d))


def _norm_matmul_body(x_ref, nw_ref, w_ref, y_ref, xn_ref):
    xn = _rms_rows(x_ref[...], nw_ref[...])
    xn_ref[...] = xn
    y_ref[...] = jnp.dot(xn.astype(BF16), w_ref[...], preferred_element_type=F32)


def _matmul_body(x_ref, w_ref, y_ref):
    y_ref[...] = jnp.dot(x_ref[...].astype(BF16), w_ref[...], preferred_element_type=F32)


def norm_matmul(x, col, k, norm_w, w_b):
    n = x.shape[0]
    n_out = w_b.shape[1]
    tm = min(512, n)
    x_spec = pl.BlockSpec((tm, k), lambda i: (i, col))
    w_spec = pl.BlockSpec((k, n_out), lambda i: (0, 0))
    y_spec = pl.BlockSpec((tm, n_out), lambda i: (i, 0))
    params = pltpu.CompilerParams(dimension_semantics=("parallel",), vmem_limit_bytes=VMEM_LIMIT)
    if norm_w is None:
        return pl.pallas_call(_matmul_body, grid=(n // tm,), in_specs=[x_spec, w_spec], out_specs=y_spec,
                              out_shape=jax.ShapeDtypeStruct((n, n_out), F32), compiler_params=params,
                              name="matmul")(x, w_b)
    return pl.pallas_call(
        _norm_matmul_body, grid=(n // tm,),
        in_specs=[x_spec, pl.BlockSpec((1, k), lambda i: (0, 0)), w_spec],
        out_specs=[y_spec, pl.BlockSpec((tm, k), lambda i: (i, 0))],
        out_shape=[jax.ShapeDtypeStruct((n, n_out), F32), jax.ShapeDtypeStruct((n, k), F32)],
        compiler_params=params, name="norm_matmul")(x, norm_w.reshape(1, k), w_b)


ATTN_TQ = Q_BLOCK


def _attn_body(*refs, n_heads, parts, dv, scale, n_seg):
    n_p = len(parts)
    q_refs = refs[:n_p]
    pos = n_p
    segs = []
    for _ in range(n_seg):
        segs.append((refs[pos:pos + n_p], refs[pos + n_p]))
        pos += n_p + 1
    o_ref = refs[pos]
    outs = []
    for h in range(n_heads):
        scores = []
        for k_refs, _ in segs:
            s = None
            for q_ref, k_ref, (d, shared) in zip(q_refs, k_refs, parts):
                qh = q_ref[:, h * d:(h + 1) * d].astype(BF16)
                kh = (k_ref[:, 0:d] if shared else k_ref[:, h * d:(h + 1) * d]).astype(BF16)
                t = _dot_nt(qh, kh)
                s = t if s is None else s + t
            scores.append(s * scale)
        m = None
        for s in scores:
            ms = jnp.max(s, axis=-1, keepdims=True)
            m = ms if m is None else jnp.maximum(m, ms)
        probs = [jnp.exp(s - m) for s in scores]
        denom = None
        for p in probs:
            ps = jnp.sum(p, axis=-1, keepdims=True)
            denom = ps if denom is None else denom + ps
        inv = 1.0 / denom
        o = None
        for p, (_, v_ref) in zip(probs, segs):
            t = jnp.dot((p * inv).astype(BF16), v_ref[:, h * dv:(h + 1) * dv].astype(BF16),
                        preferred_element_type=F32)
            o = t if o is None else o + t
        outs.append(o)
    o_ref[...] = jnp.concatenate(outs, axis=-1)


def attention(q_parts, segments, n_heads, parts, dv, scale):
    b, tq, _ = q_parts[0][0].shape
    args, specs = [], []
    for arr, w, c in q_parts:
        args.append(arr)
        specs.append(pl.BlockSpec((None, ATTN_TQ, w), lambda bi, i, c=c: (bi, i, c)))
    for k_parts, v in segments:
        for arr, w, c in list(k_parts) + [v]:
            args.append(arr)
            specs.append(pl.BlockSpec((None, arr.shape[1], w), lambda bi, i, c=c: (bi, 0, c)))
    body = functools.partial(_attn_body, n_heads=n_heads, parts=parts, dv=dv, scale=scale, n_seg=len(segments))
    return pl.pallas_call(
        body,
        grid=(b, tq // ATTN_TQ),
        in_specs=specs,
        out_specs=pl.BlockSpec((None, ATTN_TQ, n_heads * dv), lambda bi, i: (bi, i, 0)),
        out_shape=jax.ShapeDtypeStruct((b, tq, n_heads * dv), F32),
        compiler_params=pltpu.CompilerParams(dimension_semantics=("parallel", "arbitrary"),
                                             vmem_limit_bytes=VMEM_LIMIT),
        name="attention",
    )(*args)


def mla_weights(w_q_up, w_kv_up):
    wq = w_q_up.reshape(Q_RANK, HB, NOPE + ROPE)
    wq = jnp.concatenate([wq[:, :, :NOPE].reshape(Q_RANK, HB * NOPE), wq[:, :, NOPE:].reshape(Q_RANK, HB * ROPE)], 1)
    wkv = w_kv_up.reshape(KV_RANK, HB, NOPE + VB)
    wkv = jnp.concatenate([wkv[:, :, :NOPE].reshape(KV_RANK, HB * NOPE), wkv[:, :, NOPE:].reshape(KV_RANK, HB * VB)], 1)
    return wq.astype(BF16), wkv.astype(BF16)


def rope_tables(t_len):
    t = jnp.arange(t_len)
    half = ROPE // 2
    inv = 1.0 / (ROPE_THETA ** (jnp.arange(0, half, 2, dtype=F32) / half))
    a_r = (t // GRID_W).astype(F32)[:, None] * inv[None, :]
    a_c = (t % GRID_W).astype(F32)[:, None] * inv[None, :]
    cos = jnp.concatenate([jnp.cos(a_r), jnp.cos(a_r), jnp.cos(a_c), jnp.cos(a_c)], axis=1)
    sin = jnp.concatenate([-jnp.sin(a_r), jnp.sin(a_r), -jnp.sin(a_c), jnp.sin(a_c)], axis=1)
    return cos, sin


def _rope_body(q_ref, small_ref, cos_ref, sin_ref, qo_ref, ko_ref):
    quarter = ROPE // 4
    r_io = lax.broadcasted_iota(jnp.int32, (ROPE, ROPE), 0)
    c_io = lax.broadcasted_iota(jnp.int32, (ROPE, ROPE), 1)
    partner = jnp.where((c_io // quarter) % 2 == 0, c_io + quarter, c_io - quarter)
    swap = jnp.where(r_io == partner, 1.0, 0.0).astype(BF16)
    cos, sin = cos_ref[...], sin_ref[...]

    def rot(x):
        xs = sum(jnp.dot(part, swap, preferred_element_type=F32) for part in _split3_bf16(x))
        return x * cos + xs * sin
    qo_ref[...] = jnp.concatenate([rot(q_ref[:, h * ROPE:(h + 1) * ROPE]) for h in range(HB)], axis=-1)
    ko_ref[...] = rot(small_ref[:, 0:ROPE])


def rope_rotate(q, proj3):
    b, t, _ = q.shape
    cos, sin = rope_tables(t)
    tb = 256
    tab = pl.BlockSpec((tb, ROPE), lambda bi, i: (i, 0))
    return pl.pallas_call(
        _rope_body,
        grid=(b, t // tb),
        in_specs=[pl.BlockSpec((None, tb, HB * ROPE), lambda bi, i: (bi, i, HB * NOPE // (HB * ROPE))),
                  pl.BlockSpec((None, tb, LANES), lambda bi, i: (bi, i, COL_SMALL)), tab, tab],
        out_specs=[pl.BlockSpec((None, tb, HB * ROPE), lambda bi, i: (bi, i, 0)),
                   pl.BlockSpec((None, tb, ROPE), lambda bi, i: (bi, i, 0))],
        out_shape=[jax.ShapeDtypeStruct((b, t, HB * ROPE), F32), jax.ShapeDtypeStruct((b, t, ROPE), F32)],
        compiler_params=pltpu.CompilerParams(dimension_semantics=("parallel", "parallel")),
        name="rope_rotate",
    )(q, proj3, cos, sin)


MLA_PARTS = ((NOPE, False), (ROPE, True))


def mla_mixer(proj2, b, t, p, cache):
    wq_b, wkv_b = p['mla_wq_b'], p['mla_wkv_b']
    q, _ = norm_matmul(proj2, COL_CQ, Q_RANK, p['mla_q_norm_w'], wq_b)
    kv, ckv_n = norm_matmul(proj2, COL_CKV, KV_RANK, p['mla_kv_norm_w'], wkv_b)
    q3, kv3, proj3 = q.reshape(b, t, -1), kv.reshape(b, t, -1), proj2.reshape(b, t, PROJ_W)
    if cache is None:
        o = attention([(q3, HB * NOPE, 0), (q3, HB * ROPE, 2)],
                      [([(kv3, HB * NOPE, 0), (proj3, LANES, COL_SMALL)], (kv3, HB * VB, 1))],
                      HB, MLA_PARTS, VB, MLA_SCALE)
        return o, ckv_n.reshape(b, t, KV_RANK), proj3[:, :, COL_SMALL * LANES:COL_SMALL * LANES + ROPE]
    ckv_ctx, krope_ctx = cache
    n_ctx = ckv_ctx.shape[1]
    kv_ctx = norm_matmul(ckv_ctx.reshape(b * n_ctx, KV_RANK).astype(F32), 0, KV_RANK, None, wkv_b)
    kv_ctx = kv_ctx.reshape(b, n_ctx, -1)
    q_rot, k_rot = rope_rotate(q3, proj3)
    o = attention([(q3, HB * NOPE, 0), (q_rot, HB * ROPE, 0)],
                  [([(kv3, HB * NOPE, 0), (k_rot, ROPE, 0)], (kv3, HB * VB, 1)),
                   ([(kv_ctx, HB * NOPE, 0), (krope_ctx.astype(F32), ROPE, 0)], (kv_ctx, HB * VB, 1))],
                  HB, MLA_PARTS, VB, MLA_SCALE)
    return o, None, None


def _ssd_conv_body(x_ref, b_ref, c_ref, w_ref, bias_ref, o_ref):
    t = x_ref.shape[0]
    xbc = jnp.concatenate([x_ref[...], b_ref[...], c_ref[...]], axis=-1)
    row = lax.broadcasted_iota(jnp.int32, (t, 1), 0)
    prev = jnp.where(row == 0, 0.0, pltpu.roll(xbc, 1, axis=0))
    nxt = jnp.where(row == t - 1, 0.0, pltpu.roll(xbc, t - 1, axis=0))
    o_ref[...] = jax.nn.silu(prev * w_ref[0:1] + xbc * w_ref[1:2] + nxt * w_ref[2:3] + bias_ref[...])


def ssd_conv(proj3, conv_w, conv_b):
    b, t, _ = proj3.shape
    return pl.pallas_call(
        _ssd_conv_body,
        grid=(b,),
        in_specs=[pl.BlockSpec((None, t, GROUP_W), lambda bi: (bi, 0, COL_SX)),
                  pl.BlockSpec((None, t, GC * NC), lambda bi: (bi, 0, COL_SB)),
                  pl.BlockSpec((None, t, GC * NC), lambda bi: (bi, 0, COL_SC)),
                  pl.BlockSpec((CONV_W, CONV_CH), lambda bi: (0, 0)), pl.BlockSpec((1, CONV_CH), lambda bi: (0, 0))],
        out_specs=pl.BlockSpec((None, t, CONV_CH), lambda bi: (bi, 0, 0)),
        out_shape=jax.ShapeDtypeStruct((b, t, CONV_CH), F32),
        compiler_params=pltpu.CompilerParams(dimension_semantics=("parallel",), vmem_limit_bytes=VMEM_LIMIT),
        name="ssd_conv",
    )(proj3, proj3, proj3, conv_w.astype(F32), conv_b.astype(F32).reshape(1, CONV_CH))


SSD_PAIRS = HC // 2


def _ssd_direction(rev, xbc_ref, dtc_ref, dtr_ref, bias_c, a_c, bias_r, a_r, st_s, d, y_ref):
    xbc = xbc_ref[...]
    dt_c = jax.nn.softplus(dtc_ref[...] + bias_c)
    la_c = dt_c * a_c
    la_r = jax.nn.softplus(dtr_ref[...] + bias_r) * a_r
    t_io = lax.broadcasted_iota(jnp.int32, (CHUNK, CHUNK), 0)
    s_io = lax.broadcasted_iota(jnp.int32, (CHUNK, CHUNK), 1)
    causal = (s_io >= t_io) if rev else (s_io <= t_io)
    tri = jnp.where(causal, 1.0, 0.0).astype(BF16)
    tri_t = jnp.where((t_io >= s_io) if rev else (t_io <= s_io), 1.0, 0.0).astype(BF16)
    cum_c = sum(jnp.dot(tri, part, preferred_element_type=F32) for part in _split3_bf16(la_c))
    cum_r = sum(jnp.dot(part, tri_t, preferred_element_type=F32) for part in _split3_bf16(la_r))
    end = 0 if rev else CHUNK - 1
    lane_lo = lax.broadcasted_iota(jnp.int32, (CHUNK, LANES), 1) < PC
    row_lo = lax.broadcasted_iota(jnp.int32, (LANES, NC), 0) < PC
    cbs = []
    for g in range(GC):
        bg = xbc[:, GROUP_W + g * NC:GROUP_W + (g + 1) * NC].astype(BF16)
        cg = xbc[:, GROUP_W + GC * NC + g * NC:GROUP_W + GC * NC + (g + 1) * NC].astype(BF16)
        cbs.append((bg, cg, _dot_nt(cg, bg)))
    outs = []
    for pr in range(SSD_PAIRS):
        ha, hb = 2 * pr, 2 * pr + 1
        bg, cg, cb = cbs[ha // (HC // GC)]
        xdt = xbc[:, pr * LANES:(pr + 1) * LANES] * jnp.where(lane_lo, dt_c[:, ha:ha + 1], dt_c[:, hb:hb + 1])
        ys = []
        for hx in (ha, hb):
            decay = jnp.exp(jnp.where(causal, cum_c[:, hx:hx + 1] - cum_r[hx:hx + 1, :], 0.0))
            m = (cb * jnp.where(causal, decay, 0.0)).astype(BF16)
            ys.append(jnp.dot(m, xdt.astype(BF16), preferred_element_type=F32))
        y = jnp.where(lane_lo, ys[0], ys[1])
        st = st_s[d, pr]
        e_cum = jnp.where(lane_lo, jnp.exp(cum_c[:, ha:ha + 1]), jnp.exp(cum_c[:, hb:hb + 1]))
        y = y + _dot_nt(cg, st.astype(BF16)) * e_cum
        outs.append(y)
        last_a, last_b = cum_c[end:end + 1, ha:ha + 1], cum_c[end:end + 1, hb:hb + 1]
        w = jnp.where(lane_lo, jnp.exp(last_a - cum_c[:, ha:ha + 1]), jnp.exp(last_b - cum_c[:, hb:hb + 1]))
        e_last = jnp.where(row_lo, jnp.exp(last_a), jnp.exp(last_b))
        st_s[d, pr] = e_last * st + _dot_tn((xdt * w).astype(BF16), bg)
    y_ref[...] = jnp.concatenate(outs, axis=-1)


def _ssd_body(xf_ref, xb_ref, dcf_ref, dcb_ref, drf_ref, drb_ref, bc_ref, ac_ref, br_ref, ar_ref, s0_ref,
              yf_ref, yb_ref, sfin_ref, st_s):
    i = pl.program_id(1)

    @pl.when(i == 0)
    def _():
        st_s[...] = s0_ref[...]

    _ssd_direction(False, xf_ref, dcf_ref, drf_ref, bc_ref[0], ac_ref[0], br_ref[0], ar_ref[0], st_s, 0, yf_ref)
    _ssd_direction(True, xb_ref, dcb_ref, drb_ref, bc_ref[1], ac_ref[1], br_ref[1], ar_ref[1], st_s, 1, yb_ref)

    @pl.when(i == pl.num_programs(1) - 1)
    def _():
        sfin_ref[...] = st_s[...]


def ssd_scan(xbc, dt_f, dt_b, dt_bias, a_log, s0):
    b, t, _ = xbc.shape
    n = t // CHUNK
    a = -jnp.exp(a_log.astype(F32))
    bias = dt_bias.astype(F32)
    dcf, dcb = dt_f.reshape(b, n, CHUNK, HC), dt_b.reshape(b, n, CHUNK, HC)
    drf, drb = dcf.transpose(0, 1, 3, 2), dcb.transpose(0, 1, 3, 2)
    x_f = pl.BlockSpec((None, CHUNK, CONV_CH), lambda bi, i: (bi, i, 0))
    x_b = pl.BlockSpec((None, CHUNK, CONV_CH), lambda bi, i: (bi, n - 1 - i, 0))
    y_f = pl.BlockSpec((None, CHUNK, GROUP_W), lambda bi, i: (bi, i, 0))
    y_b = pl.BlockSpec((None, CHUNK, GROUP_W), lambda bi, i: (bi, n - 1 - i, 0))
    c_f = pl.BlockSpec((None, None, CHUNK, HC), lambda bi, i: (bi, i, 0, 0))
    c_b = pl.BlockSpec((None, None, CHUNK, HC), lambda bi, i: (bi, n - 1 - i, 0, 0))
    r_f = pl.BlockSpec((None, None, HC, CHUNK), lambda bi, i: (bi, i, 0, 0))
    r_b = pl.BlockSpec((None, None, HC, CHUNK), lambda bi, i: (bi, n - 1 - i, 0, 0))
    p_c = pl.BlockSpec((2, 1, HC), lambda bi, i: (0, 0, 0))
    p_r = pl.BlockSpec((2, HC, 1), lambda bi, i: (0, 0, 0))
    st = pl.BlockSpec((None, 2, SSD_PAIRS, LANES, NC), lambda bi, i: (bi, 0, 0, 0, 0))
    y_fwd, y_bwd, s_fin = pl.pallas_call(
        _ssd_body,
        grid=(b, n),
        in_specs=[x_f, x_b, c_f, c_b, r_f, r_b, p_c, p_c, p_r, p_r, st],
        out_specs=[y_f, y_b, st],
        out_shape=[jax.ShapeDtypeStruct((b, t, GROUP_W), F32), jax.ShapeDtypeStruct((b, t, GROUP_W), F32),
                   jax.ShapeDtypeStruct((b, 2, SSD_PAIRS, LANES, NC), F32)],
        scratch_shapes=[pltpu.VMEM((2, SSD_PAIRS, LANES, NC), F32)],
        compiler_params=pltpu.CompilerParams(dimension_semantics=("parallel", "arbitrary"),
                                             vmem_limit_bytes=VMEM_LIMIT),
        name="ssd_scan",
    )(xbc, xbc, dcf, dcb, drf, drb, bias.reshape(2, 1, HC), a.reshape(2, 1, HC), bias.reshape(2, HC, 1),
      a.reshape(2, HC, 1), s0.astype(F32).reshape(b, 2, SSD_PAIRS, LANES, NC))
    return y_fwd, y_bwd, s_fin.reshape(b, 2, HC, PC, NC)


def _ssd_combine_body(x_ref, yf_ref, yb_ref, z_ref, d_ref, w_ref, o_ref):
    y = (d_ref[...] * x_ref[...] + yf_ref[...] + yb_ref[...]) * jax.nn.silu(z_ref[...])
    o_ref[...] = _rms_rows(y, w_ref[...])


def ssd_combine(xbc2, y_f, y_b, proj2, d_skip, norm_w):
    n = xbc2.shape[0]
    rows = 256
    blk = pl.BlockSpec((rows, GROUP_W), lambda i: (i, 0))
    par = pl.BlockSpec((1, GROUP_W), lambda i: (0, 0))
    return pl.pallas_call(
        _ssd_combine_body,
        grid=(n // rows,),
        in_specs=[blk, blk, blk, pl.BlockSpec((rows, GROUP_W), lambda i: (i, COL_SZ)), par, par],
        out_specs=blk,
        out_shape=jax.ShapeDtypeStruct((n, GROUP_W), F32),
        compiler_params=pltpu.CompilerParams(dimension_semantics=("parallel",)),
        name="ssd_combine",
    )(xbc2, y_f.reshape(n, GROUP_W), y_b.reshape(n, GROUP_W), proj2,
      jnp.repeat(d_skip.astype(F32), PC).reshape(1, GROUP_W), norm_w.astype(F32).reshape(1, GROUP_W))


def ssd_mixer(proj2, b, t, p, init_state):
    proj3 = proj2.reshape(b, t, PROJ_W)
    xbc = ssd_conv(proj3, p['ssd_conv_w'], p['ssd_conv_b'])
    dt0 = COL_SMALL * LANES + SMALL_DT
    dt_f, dt_b = proj3[:, :, dt0:dt0 + HC], proj3[:, :, dt0 + HC:dt0 + 2 * HC]
    s0 = jnp.zeros((b, 2, HC, PC, NC), F32) if init_state is None else init_state
    y_f, y_b, s_fin = ssd_scan(xbc, dt_f, dt_b, p['ssd_dt_bias'], p['ssd_a_log'], s0)
    return ssd_combine(xbc.reshape(b * t, CONV_CH), y_f, y_b, proj2, p['ssd_d'], p['ssd_norm_w']), s_fin


def _peer_finish_body(ot_ref, x_ref, g_ref, nw_ref, o_ref, *, final_norm):
    x = x_ref[...] + g_ref[...] * ot_ref[...].T
    o_ref[...] = _rms_rows(x, nw_ref[...]) if final_norm else x


def peer_finish(out_t, tok0, x, mod, rows_per_mod, final_norm_w=None):
    n, d = x.shape
    tm = 256
    norm_w = jnp.ones((1, d), F32) if final_norm_w is None else final_norm_w.astype(F32).reshape(1, d)
    return pl.pallas_call(
        functools.partial(_peer_finish_body, final_norm=final_norm_w is not None),
        grid=(n // tm,),
        in_specs=[pl.BlockSpec((d, tm), lambda i: (0, tok0 // tm + i)), pl.BlockSpec((tm, d), lambda i: (i, 0)),
                  pl.BlockSpec((None, 1, d), lambda i: ((i * tm) // rows_per_mod, 0, MOD_G2)),
                  pl.BlockSpec((1, d), lambda i: (0, 0))],
        out_specs=pl.BlockSpec((tm, d), lambda i: (i, 0)),
        out_shape=jax.ShapeDtypeStruct((n, d), F32),
        compiler_params=pltpu.CompilerParams(dimension_semantics=("parallel",), vmem_limit_bytes=VMEM_LIMIT),
        name="peer_finish",
    )(out_t, x, mod, norm_w)


def trunk_layer(x2, b, t, mod, p, cache):
    rows_per_mod = (b * t) // mod.shape[0]
    proj2 = in_proj(x2, p['norm1_w'], mod, p['w_in_b'], rows_per_mod)
    proj3 = proj2.reshape(b, t, PROJ_W)
    latent = cache is not None
    s0 = cache[0].astype(F32) if latent else jnp.zeros((b, 2, HA, DKA, DVA), F32)
    o_f, o_b, st_a = hgrn_scan(proj3, p['lb'], s0)
    o_a = hgrn_combine(o_f, o_b, proj2, p['hgrn_norm_w'])
    o_c, st_c = ssd_mixer(proj2, b, t, p, cache[1] if latent else None)
    na_q, na_k, na_v = (proj3, GROUP_W, COL_NQ), (proj3, GROUP_W, COL_NK), (proj3, GROUP_W, COL_NV)
    if latent:
        o_m, _, _ = mla_mixer(proj2, b, t, p, (cache[2], cache[3]))
        o_d = natten_latent(proj3, p['na_rpb'], cache[4], cache[5])
        new_state = None
    else:
        o_m, ckv, krope = mla_mixer(proj2, b, t, p, None)
        o_d = attention([na_q], [([na_k], na_v)], HD, ((DHD, False),), DHD, DHD ** -0.5)
        k_na = proj3[:, :, COL_NK * GROUP_W:(COL_NK + 1) * GROUP_W].reshape(b, t, HD, DHD)
        v_na = proj3[:, :, COL_NV * GROUP_W:(COL_NV + 1) * GROUP_W].reshape(b, t, HD, DHD)
        new_state = (st_a, st_c, ckv, krope, k_na, v_na)
    n = b * t
    x2, h2 = out_proj([o_a, o_m.reshape(n, GROUP_W), o_c, o_d.reshape(n, GROUP_W)], p['w_out_b'], x2, mod,
                      p['norm2_w'], rows_per_mod)
    return x2, h2, new_state


def _final_norm_body(x_ref, w_ref, o_ref):
    x = x_ref[...]
    y = x * lax.rsqrt(jnp.mean(x * x, axis=-1, keepdims=True) + EPS)
    o_ref[...] = y * w_ref[...]


def final_rmsnorm(x, w):
    b, t, d = x.shape
    rows = 512
    out = pl.pallas_call(
        _final_norm_body,
        grid=(b * t // rows,),
        in_specs=[pl.BlockSpec((rows, d), lambda i: (i, 0)), pl.BlockSpec((1, d), lambda i: (0, 0))],
        out_specs=pl.BlockSpec((rows, d), lambda i: (i, 0)),
        out_shape=jax.ShapeDtypeStruct((b * t, d), x.dtype),
        name="final_rmsnorm",
    )(x.reshape(b * t, d), w.reshape(1, d))
    return out.reshape(b, t, d)


def kernel(x_prompt, x_sample, c, state_hgrn, state_ssd, cache_mla_ckv, cache_mla_krope, cache_na_k,
           cache_na_v, c_ctx, w_ada, b_ada, norm1_w, norm2_w, w_in, w_out, hgrn_lb_logits, hgrn_norm_w,
           mla_q_norm_w, mla_w_q_up, mla_kv_norm_w, mla_w_kv_up, ssd_conv_w, ssd_conv_b, ssd_dt_bias,
           ssd_a_log, ssd_d, ssd_norm_w, na_rpb, peer_w_q, peer_sub_keys, peer_u, peer_v, final_norm_w):
    lb_soft = jax.nn.softmax(hgrn_lb_logits.astype(F32), axis=0)
    lb_all = jnp.cumsum(lb_soft, axis=0) - lb_soft[0]
    stacked = {'w_ada': w_ada, 'b_ada': b_ada, 'norm1_w': norm1_w, 'norm2_w': norm2_w, 'w_in': w_in,
               'w_out': w_out, 'hgrn_norm_w': hgrn_norm_w, 'mla_q_norm_w': mla_q_norm_w,
               'mla_w_q_up': mla_w_q_up, 'mla_kv_norm_w': mla_kv_norm_w, 'mla_w_kv_up': mla_w_kv_up,
               'ssd_conv_w': ssd_conv_w, 'ssd_conv_b': ssd_conv_b, 'ssd_dt_bias': ssd_dt_bias,
               'ssd_a_log': ssd_a_log, 'ssd_d': ssd_d, 'ssd_norm_w': ssd_norm_w, 'na_rpb': na_rpb,
               'peer_w_q': peer_w_q, 'peer_sub_keys': peer_sub_keys, 'peer_u': peer_u, 'peer_v': peer_v}

    ub_all, vt_all = peer_prep_tables(peer_u, peer_v)
    n_ctx, n_lat = BATCH * SEQ, DEC_BATCH * DEC_SEQ
    cond = jnp.concatenate([c_ctx[None, :], c, jnp.zeros((8 - 1 - DEC_BATCH, D_MODEL), F32)], axis=0)

    xp, xs = x_prompt.reshape(n_ctx, D_MODEL), x_sample.reshape(n_lat, D_MODEL)
    ctx_states = []
    for l in range(DEPTH):
        p = {name: arr[l] for name, arr in stacked.items()}
        p['lb'] = lb_all[l]
        p['w_in_b'] = in_proj_weight(w_in[l])
        p['w_out_b'] = w_out[l].astype(BF16)
        p['mla_wq_b'], p['mla_wkv_b'] = mla_weights(mla_w_q_up[l], mla_w_kv_up[l])
        mod = adaln(cond, w_ada[l], b_ada[l])
        mod_p, mod_s = mod[0:1, None, :], mod[1:1 + DEC_BATCH, None, :]
        xp, h2p, st = trunk_layer(xp, BATCH, SEQ, mod_p, p, None)
        ctx_states.append(st)
        cache_l = (state_hgrn[:, l], state_ssd[:, l], cache_mla_ckv[:, l], cache_mla_krope[:, l],
                   cache_na_k[:, l], cache_na_v[:, l])
        xs, h2s, _ = trunk_layer(xs, DEC_BATCH, DEC_SEQ, mod_s, p, cache_l)
        out_t = peer_ffn(h2p, h2s, peer_w_q[l].T.astype(BF16), peer_sub_keys[l].astype(BF16), ub_all[l], vt_all[l])
        last_w = final_norm_w if l == DEPTH - 1 else None
        xp = peer_finish(out_t, 0, xp, mod_p, n_ctx, last_w)
        xs = peer_finish(out_t, n_ctx, xs, mod_s, DEC_SEQ, last_w)

    y_prompt = xp.reshape(BATCH, SEQ, D_MODEL)
    y_sample = xs.reshape(DEC_BATCH, DEC_SEQ, D_MODEL)
    new_state_hgrn = jnp.stack([s[0] for s in ctx_states], axis=1)
    new_state_ssd = jnp.stack([s[1] for s in ctx_states], axis=1)
    new_cache_mla_ckv = jnp.stack([s[2] for s in ctx_states], axis=1)
    new_cache_mla_krope = jnp.stack([s[3] for s in ctx_states], axis=1)
    new_cache_na_k = jnp.stack([s[4] for s in ctx_states], axis=1)
    new_cache_na_v = jnp.stack([s[5] for s in ctx_states], axis=1)
    return (y_prompt, y_sample, new_state_hgrn, new_state_ssd, new_cache_mla_ckv, new_cache_mla_krope,
            new_cache_na_k, new_cache_na_v)
```

```python
import functools
import math
import jax
import jax.numpy as jnp
from jax import lax
import numpy as np
from jax.experimental import pallas as pl
from jax.experimental.pallas import tpu as pltpu

D_MODEL = 2048
BATCH = 32
SEQ = 256
DEPTH = 2
DEC_BATCH = 4
DEC_SEQ = 1024
PAST_LEN = 256

GRID_W = 64
EPS = 1e-6
ROPE_THETA = 10000.0
Q_BLOCK = 128
CHUNK = 64
N_MIXERS = 4
GROUP_W = D_MODEL // N_MIXERS
D_MIX = N_MIXERS * GROUP_W
HA = 4
DKA = GROUP_W // HA
DVA = GROUP_W // HA
LB_FLOOR = 1e-30
HB = 4
Q_RANK = D_MODEL // 4
KV_RANK = D_MODEL // 8
NOPE = 128
ROPE = 64
VB = GROUP_W // HB
MLA_SCALE = (NOPE + ROPE) ** -0.5
HC = 8
PC = GROUP_W // HC
NC = 128
GC = 2
CONV_W = 3
CONV_CH = GROUP_W + 2 * GC * NC
HD = 8
DHD = GROUP_W // HD
WIN_R = 8
WIN_C = 16
N_KEYS = 128
N_EXPERTS = N_KEYS * N_KEYS
PEER_HEADS = 8
PEER_QDIM = 256
PEER_TOPK = 16
PEER_TOKEN_BLOCK = 128

IN_SPLITS = (HA * DKA, HA * DKA, HA * DKA, HA * DVA, HA * DVA,
             Q_RANK, KV_RANK, ROPE,
             GROUP_W, GROUP_W, GC * NC, GC * NC, HC, HC,
             GROUP_W, GROUP_W, GROUP_W)
IN_W = sum(IN_SPLITS)
F32 = jnp.float32


def rmsnorm(x, w):
    xf = x.astype(F32)
    y = xf * lax.rsqrt(jnp.mean(xf * xf, axis=-1, keepdims=True) + EPS)
    return (y * w.astype(F32)).astype(x.dtype)


def split_cols(a, sizes):
    offs = []
    acc = 0
    for s in sizes[:-1]:
        acc += s
        offs.append(acc)
    return jnp.split(a, offs, axis=-1)


def flip_t(a):
    return jnp.flip(a, axis=1)


def conv_centred(x, w, b):
    ch = x.shape[-1]
    k = w.shape[0]
    y = lax.conv_general_dilated(x, w[:, None, :].astype(x.dtype), window_strides=(1,),
                                 padding=[(k // 2, k // 2)], dimension_numbers=('NWC', 'WIO', 'NWC'),
                                 feature_group_count=ch)
    return y + b.astype(x.dtype)


def rope_2d(x):
    t_len = x.shape[1]
    t = jnp.arange(t_len)
    half = ROPE // 2
    inv = 1.0 / (ROPE_THETA ** (jnp.arange(0, half, 2, dtype=F32) / half))
    rows = (t // GRID_W).astype(F32)
    cols = (t % GRID_W).astype(F32)

    def rot(xa, pos):
        ang = (pos[:, None] * inv[None, :]).reshape((t_len,) + (1,) * (x.ndim - 3) + (inv.shape[0],))
        cs, sn = jnp.cos(ang).astype(x.dtype), jnp.sin(ang).astype(x.dtype)
        x1, x2 = jnp.split(xa, 2, axis=-1)
        return jnp.concatenate([x1 * cs - x2 * sn, x2 * cs + x1 * sn], axis=-1)
    return jnp.concatenate([rot(x[..., :half], rows), rot(x[..., half:], cols)], axis=-1)


def block_attention(q, k, v, scale):
    b, tq, h, dq = q.shape
    nb = tq // Q_BLOCK
    qb = q.reshape(b, nb, Q_BLOCK, h, dq).transpose(1, 0, 2, 3, 4)

    def one(qi):
        s = jnp.einsum('bqhd,bkhd->bhqk', qi, k).astype(F32) * scale
        pr = jax.nn.softmax(s, axis=-1).astype(v.dtype)
        return jnp.einsum('bhqk,bkhd->bqhd', pr, v)
    o = lax.map(one, qb)
    return o.transpose(1, 0, 2, 3, 4).reshape(b, tq, h, v.shape[-1])


def masked_exp(mask, diff):
    return jnp.where(mask, jnp.exp(jnp.where(mask, diff, 0.0)), 0.0)


def gla_chunk_scan(q, k, v, logf, s0):
    b, t, h, dk = q.shape
    dv = v.shape[-1]
    n = t // CHUNK

    def chunks(a):
        return a.reshape(b, n, CHUNK, h, a.shape[-1]).transpose(1, 0, 3, 2, 4)
    cum = jnp.cumsum(chunks(logf), axis=3)
    mask = jnp.tril(jnp.ones((CHUNK, CHUNK), dtype=bool))[:, :, None]

    def step(state, inp):
        qi, ki, vi, ci = inp
        decay = masked_exp(mask, ci[:, :, :, None, :] - ci[:, :, None, :, :])
        att = jnp.einsum('bhtk,bhsk,bhtsk->bhts', qi, ki, decay)
        o = jnp.einsum('bhts,bhsv->bhtv', att, vi)
        o = o + jnp.einsum('bhtk,bhkv->bhtv', qi * jnp.exp(ci), state)
        last = ci[:, :, -1]
        state = jnp.exp(last)[..., None] * state + jnp.einsum(
            'bhsk,bhsv->bhkv', ki * jnp.exp(last[:, :, None, :] - ci), vi)
        return state, o
    s_fin, o = lax.scan(step, s0, (chunks(q), chunks(k), chunks(v), cum))
    return o.transpose(1, 0, 3, 2, 4).reshape(b, t, h, dv), s_fin


def ssd_chunk_scan(x, dt, a, bm, cm, s0):
    b, t, h, p = x.shape
    g, n_st = bm.shape[2], bm.shape[3]
    r = h // g
    n = t // CHUNK
    xdt = (x * dt[..., None]).reshape(b, n, CHUNK, g, r, p).transpose(1, 0, 2, 3, 4, 5)
    la = (dt * a).reshape(b, n, CHUNK, g, r).transpose(1, 0, 2, 3, 4)
    bc = bm.reshape(b, n, CHUNK, g, n_st).transpose(1, 0, 2, 3, 4)
    cc = cm.reshape(b, n, CHUNK, g, n_st).transpose(1, 0, 2, 3, 4)
    mask = jnp.tril(jnp.ones((CHUNK, CHUNK), dtype=bool))[None, :, :, None, None]

    def step(state, inp):
        xi, ai, bi, ci = inp
        cum = jnp.cumsum(ai, axis=1)
        lmat = masked_exp(mask, cum[:, :, None] - cum[:, None, :])
        cb = jnp.einsum('btgn,bsgn->btsg', ci, bi)
        y = jnp.einsum('btsg,btsgr,bsgrp->btgrp', cb, lmat, xi)
        y = y + jnp.einsum('btgn,bgrpn->btgrp', ci, state) * jnp.exp(cum)[..., None]
        last = cum[:, -1]
        w = jnp.exp(last[:, None] - cum)
        state = jnp.exp(last)[..., None, None] * state + jnp.einsum('bsgn,bsgr,bsgrp->bgrpn', bi, w, xi)
        return state, y
    s_fin, y = lax.scan(step, s0.reshape(b, g, r, p, n_st), (xdt, la, bc, cc))
    return y.transpose(1, 0, 2, 3, 4, 5).reshape(b, t, h, p), s_fin.reshape(b, h, p, n_st)


HG_SUB = 16
HG_NSUB = CHUNK // HG_SUB


def _split3_bf16(x):
    hi = x.astype(BF16)
    r1 = x - hi.astype(F32)
    mid = r1.astype(BF16)
    lo = (r1 - mid.astype(F32)).astype(BF16)
    return hi, mid, lo


def _hgrn_direction(rev, q_ref, f_ref, v_ref, la, lc, om, st_s, d, o_ref):
    fx = f_ref[...]
    logf = jnp.logaddexp(la, lc + jax.nn.log_sigmoid(fx))
    kk = om * jax.nn.sigmoid(-fx)
    qq = jax.nn.silu(q_ref[...]) * (DKA ** -0.5)
    vv = v_ref[...]
    t_io = lax.broadcasted_iota(jnp.int32, (CHUNK, CHUNK), 0)
    s_io = lax.broadcasted_iota(jnp.int32, (CHUNK, CHUNK), 1)
    tri = jnp.where((s_io >= t_io) if rev else (s_io <= t_io), 1.0, 0.0).astype(BF16)
    c = sum(jnp.dot(tri, part, preferred_element_type=F32) for part in _split3_bf16(logf))
    row = lax.broadcasted_iota(jnp.int32, (CHUNK, 1), 0)
    sub_row = lax.broadcasted_iota(jnp.int32, (HG_SUB, 1), 0)
    lane = lax.broadcasted_iota(jnp.int32, (HG_SUB, CHUNK), 1)
    outs = []
    for h in range(HA):
        cs = slice(h * DKA, (h + 1) * DKA)
        ch, qh, kh, vh = c[:, cs], qq[:, cs], kk[:, cs], vv[:, cs]
        st = st_s[d, h]
        o = _dot_nt((qh * jnp.exp(ch)).astype(BF16), st.astype(BF16))
        att = jnp.zeros((CHUNK, CHUNK), F32)
        for i in range(HG_NSUB):
            if rev:
                if i == HG_NSUB - 1:
                    continue
                c_ref = ch[(i + 1) * HG_SUB:(i + 1) * HG_SUB + 1]
                k_side = row >= (i + 1) * HG_SUB
            else:
                if i == 0:
                    continue
                c_ref = ch[i * HG_SUB - 1:i * HG_SUB]
                k_side = row < i * HG_SUB
            q_side = (row >= i * HG_SUB) & (row < (i + 1) * HG_SUB)
            qs = jnp.where(q_side, qh * jnp.exp(jnp.where(q_side, ch - c_ref, 0.0)), 0.0)
            ks = jnp.where(k_side, kh * jnp.exp(jnp.where(k_side, c_ref - ch, 0.0)), 0.0)
            att = att + _dot_nt(qs.astype(BF16), ks.astype(BF16))
        strips = []
        for i in range(HG_NSUB):
            blk = slice(i * HG_SUB, (i + 1) * HG_SUB)
            cb, qb, kb = ch[blk], qh[blk], kh[blk]
            strip = jnp.zeros((HG_SUB, CHUNK), F32)
            for s in range(HG_SUB):
                causal = (sub_row <= s) if rev else (sub_row >= s)
                w = jnp.exp(jnp.where(causal, cb - cb[s:s + 1], 0.0))
                col = jnp.sum(jnp.where(causal, w * qb * kb[s:s + 1], 0.0), axis=-1, keepdims=True)
                strip = jnp.where(lane == i * HG_SUB + s, col, strip)
            strips.append(strip)
        att = att + jnp.concatenate(strips, axis=0)
        o = o + jnp.dot(att.astype(BF16), vh.astype(BF16), preferred_element_type=F32)
        outs.append(o)
        c_end = ch[0:1] if rev else ch[CHUNK - 1:CHUNK]
        kd = kh * jnp.exp(c_end - ch)
        st_s[d, h] = st * jnp.exp(c_end) + _dot_tn(vh.astype(BF16), kd.astype(BF16))
    o_ref[...] = jnp.concatenate(outs, axis=-1)


def _hgrn_body(qf_ref, ff_ref, vf_ref, qb_ref, fb_ref, vb_ref, la_ref, lc_ref, om_ref, s0_ref,
               of_ref, ob_ref, sfin_ref, st_s):
    i = pl.program_id(1)

    @pl.when(i == 0)
    def _():
        for d in range(2):
            for h in range(HA):
                st_s[d, h] = s0_ref[d, h].T

    _hgrn_direction(False, qf_ref, ff_ref, vf_ref, la_ref[0:1], lc_ref[0:1], om_ref[0:1], st_s, 0, of_ref)
    _hgrn_direction(True, qb_ref, fb_ref, vb_ref, la_ref[1:2], lc_ref[1:2], om_ref[1:2], st_s, 1, ob_ref)

    @pl.when(i == pl.num_programs(1) - 1)
    def _():
        for d in range(2):
            for h in range(HA):
                sfin_ref[d, h] = st_s[d, h].T


def hgrn_scan(proj3, lb, s0):
    b, t, _ = proj3.shape
    w = HA * DKA
    n = t // CHUNK
    lb = lb.astype(F32)
    la = jnp.log(jnp.maximum(lb, LB_FLOOR))
    lc = jnp.log1p(-lb)
    om = 1.0 - lb

    def fwd_col(c):
        return pl.BlockSpec((None, CHUNK, w), lambda bi, i: (bi, i, c))

    def bwd_col(c):
        return pl.BlockSpec((None, CHUNK, w), lambda bi, i: (bi, n - 1 - i, c))
    fwd, bwd = fwd_col(0), bwd_col(0)
    par = pl.BlockSpec((2, w), lambda bi, i: (0, 0))
    st = pl.BlockSpec((None, 2, HA, DKA, DVA), lambda bi, i: (bi, 0, 0, 0, 0))
    q, f_fwd, f_bwd, v = proj3, proj3, proj3, proj3
    return pl.pallas_call(
        _hgrn_body,
        grid=(b, n),
        in_specs=[fwd_col(COL_HQ), fwd_col(COL_HFF), fwd_col(COL_HI), bwd_col(COL_HQ), bwd_col(COL_HFB),
                  bwd_col(COL_HI), par, par, par, st],
        out_specs=[fwd, bwd, st],
        out_shape=[jax.ShapeDtypeStruct((b, t, w), F32), jax.ShapeDtypeStruct((b, t, w), F32),
                   jax.ShapeDtypeStruct((b, 2, HA, DKA, DVA), F32)],
        scratch_shapes=[pltpu.VMEM((2, HA, DVA, DKA), F32)],
        compiler_params=pltpu.CompilerParams(dimension_semantics=("parallel", "arbitrary"),
                                             vmem_limit_bytes=VMEM_LIMIT),
        name="hgrn_scan",
    )(q, f_fwd, v, q, f_bwd, v, la, lc, om, s0)


def _hgrn_combine_body(of_ref, ob_ref, g_ref, w_ref, o_ref):
    y = of_ref[...] + ob_ref[...]
    g = g_ref[...]
    outs = []
    for h in range(HA):
        cs = slice(h * DVA, (h + 1) * DVA)
        yh = y[:, cs]
        yn = yh * lax.rsqrt(jnp.mean(yh * yh, axis=-1, keepdims=True) + EPS)
        outs.append(yn * w_ref[:, cs] * jax.nn.silu(g[:, cs]))
    o_ref[...] = jnp.concatenate(outs, axis=-1)


def hgrn_combine(o_f, o_b, proj2, norm_w):
    b, t, w = o_f.shape
    rows = 256
    blk = pl.BlockSpec((rows, w), lambda i: (i, 0))
    return pl.pallas_call(
        _hgrn_combine_body,
        grid=(b * t // rows,),
        in_specs=[blk, blk, pl.BlockSpec((rows, w), lambda i: (i, COL_HG)), pl.BlockSpec((1, w), lambda i: (0, 0))],
        out_specs=blk,
        out_shape=jax.ShapeDtypeStruct((b * t, w), F32),
        compiler_params=pltpu.CompilerParams(dimension_semantics=("parallel",)),
        name="hgrn_combine",
    )(o_f.reshape(b * t, w), o_b.reshape(b * t, w), proj2, norm_w.astype(F32).reshape(1, w))


def ssd_mixer(parts, conv_w, conv_b, dt_bias, a_log, d_skip, norm_w, init_state):
    x, z, b_in, c_in, dt_fwd, dt_bwd = parts
    b, t, _ = x.shape
    xbc = jax.nn.silu(conv_centred(jnp.concatenate([x, b_in, c_in], axis=-1), conv_w, conv_b))
    xs, bs, cs = split_cols(xbc, (GROUP_W, GC * NC, GC * NC))
    xh = xs.astype(F32).reshape(b, t, HC, PC)
    bg = bs.astype(F32).reshape(b, t, GC, NC)
    cg = cs.astype(F32).reshape(b, t, GC, NC)
    a = -jnp.exp(a_log.astype(F32))
    y = d_skip.astype(F32)[:, None] * xh
    finals = []
    for d, dt_raw in enumerate((dt_fwd, dt_bwd)):
        dt = jax.nn.softplus(dt_raw.astype(F32) + dt_bias[d].astype(F32))
        if init_state is None:
            s0 = jnp.zeros((b, HC, PC, NC), F32)
        else:
            s0 = init_state[:, d].astype(F32)
        if d == 0:
            yd, sf = ssd_chunk_scan(xh, dt, a[d], bg, cg, s0)
        else:
            yd, sf = ssd_chunk_scan(flip_t(xh), flip_t(dt), a[d], flip_t(bg), flip_t(cg), s0)
            yd = flip_t(yd)
        y = y + yd
        finals.append(sf)
    y = y.reshape(b, t, GROUP_W) * jax.nn.silu(z.astype(F32))
    return rmsnorm(y, norm_w).astype(x.dtype), jnp.stack(finals, axis=1)


def mla_project(parts, p):
    cq, ckv, krope = parts
    b, t, _ = cq.shape
    q = (rmsnorm(cq, p['mla_q_norm_w']) @ p['mla_w_q_up']).reshape(b, t, HB, NOPE + ROPE)
    return q[..., :NOPE], q[..., NOPE:], rmsnorm(ckv, p['mla_kv_norm_w']), krope


def mla_keys_values(ckv, krope, w_kv_up):
    b, t, _ = ckv.shape
    kv = (ckv @ w_kv_up).reshape(b, t, HB, NOPE + VB)
    k = jnp.concatenate([kv[..., :NOPE], jnp.broadcast_to(krope[:, :, None, :], (b, t, HB, ROPE))], axis=-1)
    return k, kv[..., NOPE:]


def mla_context(parts, p):
    q_nope, q_rope, ckv, krope = mla_project(parts, p)
    k, v = mla_keys_values(ckv, krope, p['mla_w_kv_up'])
    o = block_attention(jnp.concatenate([q_nope, q_rope], axis=-1), k, v, MLA_SCALE)
    b, t = o.shape[:2]
    return o.reshape(b, t, HB * VB), ckv, krope


def mla_latent(parts, p, ckv_ctx, krope_ctx):
    q_nope, q_rope, ckv, krope = mla_project(parts, p)
    q = jnp.concatenate([q_nope, rope_2d(q_rope)], axis=-1)
    k_lat, v_lat = mla_keys_values(ckv, rope_2d(krope), p['mla_w_kv_up'])
    k_ctx, v_ctx = mla_keys_values(ckv_ctx.astype(ckv.dtype), krope_ctx.astype(krope.dtype), p['mla_w_kv_up'])
    o = block_attention(q, jnp.concatenate([k_lat, k_ctx], axis=1), jnp.concatenate([v_lat, v_ctx], axis=1),
                        MLA_SCALE)
    b, t = o.shape[:2]
    return o.reshape(b, t, HB * VB)


def natten_context(parts):
    b, t, _ = parts[0].shape
    q, k, v = [a.reshape(b, t, HD, DHD) for a in parts]
    o = block_attention(q, k, v, DHD ** -0.5)
    return o.reshape(b, t, GROUP_W), k, v


BF16 = jnp.bfloat16
LANES = 128
VMEM_LIMIT = 56 * 1024 * 1024
NEG_BIG = -1e30

NA_ROWS = DEC_SEQ // GRID_W
NA_WR = min(WIN_R, NA_ROWS)
NA_WIN = NA_WR * GRID_W


def _dot_nt(a, b):
    return lax.dot_general(a, b, (((1,), (1,)), ((), ())), preferred_element_type=F32)


def _dot_tn(a, b):
    return lax.dot_general(a, b, (((0,), (0,)), ((), ())), preferred_element_type=F32)


def natten_bias_table(rpb):
    cols = jnp.arange(GRID_W)
    start = jnp.clip(cols - WIN_C // 2, 0, GRID_W - WIN_C)
    in_win = (cols[None, :] >= start[:, None]) & (cols[None, :] < start[:, None] + WIN_C)
    c_off = jnp.clip(cols[None, :] - cols[:, None] + (WIN_C - 1), 0, 2 * WIN_C - 2)
    r_off = jnp.arange(NA_WR)[:, None] - (NA_WR - 1) + jnp.arange(NA_WR)[None, :] + (WIN_R - 1)
    tab = rpb.astype(F32)[:, r_off][:, :, :, c_off]
    tab = jnp.where(in_win[None, None, None], tab, NEG_BIG)
    return tab.transpose(0, 1, 3, 2, 4).reshape(HD, NA_WR, GRID_W, NA_WIN)


def _natten_body(q_ref, k_ref, v_ref, kc_ref, vc_ref, bias_ref, o_ref):
    r = pl.program_id(1)
    rs = jnp.clip(r - NA_WR // 2, 0, NA_ROWS - NA_WR)
    win = pl.ds(pl.multiple_of(rs * GRID_W, GRID_W), NA_WIN)
    q = q_ref[...] * (DHD ** -0.5)
    outs = []
    for h in range(HD):
        cs = slice(h * DHD, (h + 1) * DHD)
        qh = q[:, cs].astype(BF16)
        s_loc = _dot_nt(qh, k_ref[win, cs].astype(BF16)) + bias_ref[h]
        s_ctx = _dot_nt(qh, kc_ref[:, cs].astype(BF16))
        m = jnp.maximum(jnp.max(s_loc, axis=-1, keepdims=True), jnp.max(s_ctx, axis=-1, keepdims=True))
        p_loc = jnp.exp(s_loc - m)
        p_ctx = jnp.exp(s_ctx - m)
        inv = 1.0 / (jnp.sum(p_loc, axis=-1, keepdims=True) + jnp.sum(p_ctx, axis=-1, keepdims=True))
        outs.append(jnp.dot((p_loc * inv).astype(BF16), v_ref[win, cs].astype(BF16), preferred_element_type=F32)
                    + jnp.dot((p_ctx * inv).astype(BF16), vc_ref[:, cs].astype(BF16),
                              preferred_element_type=F32))
    o_ref[...] = jnp.concatenate(outs, axis=-1)


def natten_latent(proj3, rpb, k_ctx, v_ctx):
    b, t, _ = proj3.shape
    w = GROUP_W
    assert t == DEC_SEQ
    n_ctx = k_ctx.shape[1]
    bias = natten_bias_table(rpb)
    q = k = v = proj3

    def delta_idx(r):
        return jnp.clip(r - NA_WR // 2, 0, NA_ROWS - NA_WR) - r + (NA_WR - 1)
    ctx = pl.BlockSpec((None, n_ctx, w), lambda bi, r: (bi, 0, 0))
    return pl.pallas_call(
        _natten_body,
        grid=(b, NA_ROWS),
        in_specs=[pl.BlockSpec((None, GRID_W, w), lambda bi, r: (bi, r, COL_NQ)),
                  pl.BlockSpec((None, t, w), lambda bi, r: (bi, 0, COL_NK)),
                  pl.BlockSpec((None, t, w), lambda bi, r: (bi, 0, COL_NV)), ctx, ctx,
                  pl.BlockSpec((HD, None, GRID_W, NA_WIN), lambda bi, r: (0, delta_idx(r), 0, 0))],
        out_specs=pl.BlockSpec((None, GRID_W, w), lambda bi, r: (bi, r, 0)),
        out_shape=jax.ShapeDtypeStruct((b, t, w), F32),
        compiler_params=pltpu.CompilerParams(dimension_semantics=("parallel", "arbitrary"),
                                             vmem_limit_bytes=VMEM_LIMIT),
        name="natten_latent",
    )(q, k, v, k_ctx.reshape(b, n_ctx, w), v_ctx.reshape(b, n_ctx, w), bias)


PEER_KDIM = PEER_QDIM // 2
PEER_HP = PEER_HEADS * 2
PEER_PREP_TE = 512
PEER_SCORE_TM = 512
PEER_SELECT_TM = 256
PEER_TM = 512
PEER_TE = 1024
PEER_ROWS = PEER_TE // N_KEYS
PEER_CAND = [(a, b) for a in range(PEER_TOPK) for b in range(PEER_TOPK) if (a + 1) * (b + 1) <= PEER_TOPK]


def _peer_prep_body(u_ref, v_ref, ub_ref, vt_ref):
    ub_ref[...] = u_ref[...].astype(BF16)
    vt_ref[...] = v_ref[...].T.astype(BF16)


def peer_prep_tables(peer_u, peer_v):
    depth, n_e, d = peer_u.shape
    te = PEER_PREP_TE
    return pl.pallas_call(
        _peer_prep_body,
        grid=(depth, n_e // te),
        in_specs=[pl.BlockSpec((None, te, d), lambda l, j: (l, j, 0)),
                  pl.BlockSpec((None, te, d), lambda l, j: (l, j, 0))],
        out_specs=[pl.BlockSpec((None, te, d), lambda l, j: (l, j, 0)),
                   pl.BlockSpec((None, d, te), lambda l, j: (l, 0, j))],
        out_shape=[jax.ShapeDtypeStruct((depth, n_e, d), BF16), jax.ShapeDtypeStruct((depth, d, n_e), BF16)],
        compiler_params=pltpu.CompilerParams(dimension_semantics=("parallel", "parallel"),
                                             vmem_limit_bytes=VMEM_LIMIT),
        name="peer_prep_tables",
    )(peer_u, peer_v)


def _peer_score_body(ha_ref, hb_ref, wqt_ref, sk_ref, xt_ref, st_ref, *, tiles_a):
    h = jnp.where(pl.program_id(0) < tiles_a, ha_ref[...], hb_ref[...])
    xt = h.T.astype(BF16)
    xt_ref[...] = xt
    qt = jnp.dot(wqt_ref[...], xt, preferred_element_type=F32).astype(BF16)
    for hp in range(PEER_HP):
        st_ref[hp] = jnp.dot(sk_ref[hp % 2], qt[hp * PEER_KDIM:(hp + 1) * PEER_KDIM, :],
                             preferred_element_type=F32)


def peer_scores(h_a, h_b, wq_t, sub_keys):
    n_a, d = h_a.shape
    n = n_a + h_b.shape[0]
    tm = PEER_SCORE_TM
    tiles_a = n_a // tm
    return pl.pallas_call(
        functools.partial(_peer_score_body, tiles_a=tiles_a),
        grid=(n // tm,),
        in_specs=[pl.BlockSpec((tm, d), lambda i: (jnp.minimum(i, tiles_a - 1), 0)),
                  pl.BlockSpec((tm, d), lambda i: (jnp.maximum(i - tiles_a, 0), 0)),
                  pl.BlockSpec(wq_t.shape, lambda i: (0, 0)),
                  pl.BlockSpec(sub_keys.shape, lambda i: (0, 0, 0))],
        out_specs=[pl.BlockSpec((d, tm), lambda i: (0, i)),
                   pl.BlockSpec((PEER_HP, N_KEYS, tm), lambda i: (0, 0, i))],
        out_shape=[jax.ShapeDtypeStruct((d, n), BF16), jax.ShapeDtypeStruct((PEER_HP, N_KEYS, n), F32)],
        compiler_params=pltpu.CompilerParams(dimension_semantics=("parallel",), vmem_limit_bytes=VMEM_LIMIT),
        name="peer_scores",
    )(h_a, h_b, wq_t, sub_keys)


SUBLANES = 8
assert N_KEYS // SUBLANES == PEER_TOPK


def _bitonic_merge_desc(xs):
    xs = list(xs)
    j = len(xs) // 2
    while j >= 1:
        for i in range(len(xs)):
            if i & j == 0:
                xs[i], xs[i | j] = jnp.maximum(xs[i], xs[i | j]), jnp.minimum(xs[i], xs[i | j])
        j //= 2
    return xs


def _sort_desc(xs):
    xs = list(xs)
    n = len(xs)
    k = 2
    while k <= n:
        j = k // 2
        while j >= 1:
            for i in range(n):
                if i & j == 0:
                    hi, lo = jnp.maximum(xs[i], xs[i | j]), jnp.minimum(xs[i], xs[i | j])
                    xs[i], xs[i | j] = (hi, lo) if (i & k) == 0 else (lo, hi)
            j //= 2
        k *= 2
    return xs


def _peer_select_body(st_ref, e1_ref, cnt1_ref, rank2_ref, e2_ref, rank1_s, vtop_s, cnt_s, zinv_s):
    tm = st_ref.shape[-1]
    kio = lax.broadcasted_iota(jnp.int32, (N_KEYS, LANES), 0).astype(F32)
    neg = jnp.float32(-jnp.inf)

    def group(g, carry):
        ln = pl.ds(pl.multiple_of(g * LANES, LANES), LANES)

        def store_rank(hp, rank):
            if hp % 2 == 0:
                rank1_s[hp // 2, :, ln] = rank
            else:
                rank2_ref[hp // 2, :, ln] = rank

        ambiguous = jnp.zeros((SUBLANES, LANES), F32)
        for hp in range(PEER_HP):
            h, part = hp // 2, hp % 2
            tiles = [st_ref[hp, v * SUBLANES:(v + 1) * SUBLANES, ln] for v in range(N_KEYS // SUBLANES)]
            top = _sort_desc(list(tiles))
            for shift in (SUBLANES // 2, SUBLANES // 4, SUBLANES // 8):
                merged = [jnp.maximum(top[v], pltpu.roll(top[PEER_TOPK - 1 - v], shift, axis=0))
                          for v in range(PEER_TOPK)]
                top = _bitonic_merge_desc(merged)
            gap = top[0] - top[1]
            for a in range(1, PEER_TOPK - 1):
                gap = jnp.minimum(gap, top[a] - top[a + 1])
            n_sel = jnp.zeros((SUBLANES, LANES), jnp.int32)
            ranks = []
            for t in tiles:
                below = jnp.zeros((SUBLANES, LANES), jnp.int32)
                for a in range(PEER_TOPK):
                    below = below + lax.shift_right_arithmetic(pltpu.bitcast(t - top[a], jnp.int32), 31)
                ranks.append((-below).astype(F32))
                n_sel = n_sel - lax.shift_right_arithmetic(below + (PEER_TOPK - 1), 31) - 1
            n_sel = -n_sel
            for shift in (SUBLANES // 2, SUBLANES // 4, SUBLANES // 8):
                n_sel = n_sel + pltpu.roll(n_sel, shift, axis=0)
            ambiguous = ambiguous + jnp.where(gap == 0.0, 1.0, 0.0) + jnp.abs((n_sel - PEER_TOPK).astype(F32))
            for a in range(PEER_TOPK):
                vtop_s[part, a, h:h + 1, ln] = top[a][0:1, :]
            store_rank(hp, jnp.concatenate(ranks, axis=0))

        @pl.when(jnp.max(ambiguous) > 0.0)
        def _():
            for hp in range(PEER_HP):
                h, part = hp // 2, hp % 2

                def extract(it, sr):
                    s, rank = sr
                    m = jnp.max(s, axis=0, keepdims=True)
                    first = jnp.min(jnp.where(s == m, kio, float(N_KEYS)), axis=0, keepdims=True)
                    sel = kio == first
                    vtop_s[part, it, h:h + 1, ln] = m
                    return jnp.where(sel, neg, s), jnp.where(sel, it, rank)
                _, rank = lax.fori_loop(0, PEER_TOPK, extract,
                                        (st_ref[hp, :, ln], jnp.full((N_KEYS, LANES), PEER_TOPK, jnp.int32)))
                store_rank(hp, rank.astype(F32))

        v1 = [vtop_s[0, a, :, ln] for a in range(PEER_TOPK)]
        v2 = [vtop_s[1, b, :, ln] for b in range(PEER_TOPK)]
        sums = [v1[a] + v2[b] for a, b in PEER_CAND]
        n_c = len(PEER_CAND)
        beaten = [jnp.zeros((PEER_HEADS, LANES), F32) for _ in range(n_c)]
        for i in range(n_c):
            for j in range(i + 1, n_c):
                ge = sums[i] >= sums[j]
                beaten[j] = beaten[j] + jnp.where(ge, 1.0, 0.0)
                beaten[i] = beaten[i] + jnp.where(ge, 0.0, 1.0)
        z = jnp.zeros((PEER_HEADS, LANES), F32)
        cnt = [jnp.zeros((PEER_HEADS, LANES), F32) for _ in range(PEER_TOPK)]
        for i, (a, b) in enumerate(PEER_CAND):
            keep = beaten[i] < float(PEER_TOPK)
            z = z + jnp.where(keep, jnp.exp(sums[i] - sums[0]), 0.0)
            cnt[a] = cnt[a] + jnp.where(keep, 1.0, 0.0)
        for a in range(PEER_TOPK):
            cnt_s[a, :, ln] = cnt[a]
        zinv_s[:, ln] = 1.0 / z

        for h in range(PEER_HEADS):
            rank1 = rank1_s[h, :, ln]
            s1 = st_ref[2 * h, :, ln]
            e1 = jnp.exp(s1 - vtop_s[0, 0, h:h + 1, ln]) * zinv_s[h:h + 1, ln]
            e1_ref[h, :, ln] = jnp.where(rank1 < float(PEER_TOPK), e1, 0.0)
            c1 = jnp.zeros((N_KEYS, LANES), F32)
            for a in range(PEER_TOPK):
                c1 = c1 + jnp.where(rank1 == float(a), cnt_s[a, h:h + 1, ln], 0.0)
            cnt1_ref[h, :, ln] = c1
            e2_ref[h, :, ln] = jnp.exp(st_ref[2 * h + 1, :, ln] - vtop_s[1, 0, h:h + 1, ln])
        return carry
    lax.fori_loop(0, tm // LANES, group, 0)


def peer_select(st):
    _, _, n = st.shape
    tm = PEER_SELECT_TM
    spec = pl.BlockSpec((PEER_HEADS, N_KEYS, tm), lambda i: (0, 0, i))
    shp = jax.ShapeDtypeStruct((PEER_HEADS, N_KEYS, n), F32)
    return pl.pallas_call(
        _peer_select_body,
        grid=(n // tm,),
        in_specs=[pl.BlockSpec((PEER_HP, N_KEYS, tm), lambda i: (0, 0, i))],
        out_specs=[spec, spec, spec, spec],
        out_shape=[shp, shp, shp, shp],
        scratch_shapes=[pltpu.VMEM((PEER_HEADS, N_KEYS, tm), F32),
                        pltpu.VMEM((2, PEER_TOPK, PEER_HEADS, tm), F32),
                        pltpu.VMEM((PEER_TOPK, PEER_HEADS, tm), F32),
                        pltpu.VMEM((PEER_HEADS, tm), F32)],
        compiler_params=pltpu.CompilerParams(dimension_semantics=("parallel",), vmem_limit_bytes=VMEM_LIMIT),
        name="peer_select",
    )(st)


def _gelu_tanh(x):
    return 0.5 * x * (1.0 + jnp.tanh(math.sqrt(2.0 / math.pi) * (x + 0.044715 * (x * x * x))))


PEER_SUB_ROWS = 4
PEER_GATE_ROWS = 32


def _peer_expert_body(u_ref, vt_ref, xt_ref, e1_ref, cnt1_ref, rank2_ref, e2_ref, o_ref, st_s, at_s):
    j = pl.program_id(1)
    tm = xt_ref.shape[-1]

    @pl.when(j == 0)
    def _():
        o_ref[...] = jnp.zeros_like(o_ref)

    for c in range(PEER_ROWS // PEER_SUB_ROWS):
        sub = slice(c * PEER_SUB_ROWS * N_KEYS, (c + 1) * PEER_SUB_ROWS * N_KEYS)
        st_s[sub, :] = jnp.dot(u_ref[sub, :], xt_ref[...], preferred_element_type=F32)
        sub_rows = range(c * PEER_SUB_ROWS, (c + 1) * PEER_SUB_ROWS)
        n_blk = N_KEYS // PEER_GATE_ROWS

        def gate_block(idx, carry):
            ln = pl.ds(pl.multiple_of((idx // n_blk) * LANES, LANES), LANES)
            col0 = pl.multiple_of((idx % n_blk) * PEER_GATE_ROWS, PEER_GATE_ROWS)
            cols = pl.ds(col0, PEER_GATE_ROWS)
            gate = {r: jnp.zeros((PEER_GATE_ROWS, LANES), F32) for r in sub_rows}
            for h in range(PEER_HEADS):
                rank2 = rank2_ref[h, cols, ln]
                e2_bits = pltpu.bitcast(e2_ref[h, cols, ln], jnp.int32)
                for r in sub_rows:
                    keep = lax.shift_right_arithmetic(
                        pltpu.bitcast(rank2 - cnt1_ref[h, r:r + 1, ln], jnp.int32), 31)
                    gate[r] = gate[r] + pltpu.bitcast(e2_bits & keep, F32) * e1_ref[h, r:r + 1, ln]
            for r in sub_rows:
                rows = pl.ds(r * N_KEYS + col0, PEER_GATE_ROWS)
                at_s[rows, ln] = (_gelu_tanh(st_s[rows, ln]) * gate[r]).astype(BF16)
            return carry
        lax.fori_loop(0, (tm // LANES) * n_blk, gate_block, 0)
        o_ref[...] += jnp.dot(vt_ref[:, sub], at_s[sub, :], preferred_element_type=F32)


def peer_experts(ub, vt, xt, e1, cnt1, rank2, e2):
    n_e, d = ub.shape
    n = xt.shape[1]
    tm, te = PEER_TM, PEER_TE
    row_spec = pl.BlockSpec((PEER_HEADS, PEER_ROWS, tm), lambda i, j: (0, j, i))
    col_spec = pl.BlockSpec((PEER_HEADS, N_KEYS, tm), lambda i, j: (0, 0, i))
    return pl.pallas_call(
        _peer_expert_body,
        grid=(n // tm, n_e // te),
        in_specs=[pl.BlockSpec((te, d), lambda i, j: (j, 0)),
                  pl.BlockSpec((d, te), lambda i, j: (0, j)),
                  pl.BlockSpec((d, tm), lambda i, j: (0, i)),
                  row_spec, row_spec, col_spec, col_spec],
        out_specs=pl.BlockSpec((d, tm), lambda i, j: (0, i)),
        out_shape=jax.ShapeDtypeStruct((d, n), F32),
        scratch_shapes=[pltpu.VMEM((te, tm), F32), pltpu.VMEM((te, tm), BF16)],
        compiler_params=pltpu.CompilerParams(dimension_semantics=("parallel", "arbitrary"),
                                             vmem_limit_bytes=VMEM_LIMIT),
        name="peer_experts",
    )(ub, vt, xt, e1, cnt1, rank2, e2)


def peer_ffn(h_a, h_b, wq_t, sub_keys_b, ub, vt):
    xt, st = peer_scores(h_a, h_b, wq_t, sub_keys_b)
    e1, cnt1, rank2, e2 = peer_select(st)
    return peer_experts(ub, vt, xt, e1, cnt1, rank2, e2)


PROJ_ORDER = (0, 1, 2, 3, 4, 5, 8, 9, 14, 15, 16, 6, 10, 11, 7, 12, 13)
PROJ_W = 6528
COL_HQ, COL_HFF, COL_HFB, COL_HI, COL_HG, COL_CQ, COL_SX, COL_SZ, COL_NQ, COL_NK, COL_NV = range(11)
COL_CKV, COL_SB, COL_SC = 22, 23, 24
COL_SMALL = 50
SMALL_DT = ROPE
PROJ_TM = 512
PROJ_TN = 2176
MOD_SH1, MOD_SC1, MOD_G1, MOD_SH2, MOD_SC2, MOD_G2 = range(6)


def in_proj_weight(w_in):
    offs = np.cumsum((0,) + IN_SPLITS)
    cols = [w_in[:, offs[k]:offs[k + 1]] for k in PROJ_ORDER]
    cols.append(jnp.zeros((w_in.shape[0], PROJ_W - IN_W), w_in.dtype))
    return jnp.concatenate(cols, axis=1).astype(BF16)


def _adaln_body(c_ref, w_ref, b_ref, o_ref):
    a = jax.nn.silu(c_ref[...]).astype(BF16)
    o_ref[...] = jnp.dot(a, w_ref[...].astype(BF16), preferred_element_type=F32) + b_ref[...]


def adaln(cond, w_ada, b_ada):
    r, d = cond.shape
    n = w_ada.shape[1]
    tn = 1024
    return pl.pallas_call(
        _adaln_body,
        grid=(n // tn,),
        in_specs=[pl.BlockSpec((r, d), lambda j: (0, 0)), pl.BlockSpec((d, tn), lambda j: (0, j)),
                  pl.BlockSpec((1, tn), lambda j: (0, j))],
        out_specs=pl.BlockSpec((r, tn), lambda j: (0, j)),
        out_shape=jax.ShapeDtypeStruct((r, n), F32),
        compiler_params=pltpu.CompilerParams(dimension_semantics=("parallel",), vmem_limit_bytes=VMEM_LIMIT),
        name="adaln",
    )(cond, w_ada, b_ada.reshape(1, n))


def _rms_rows(x, w):
    return x * lax.rsqrt(jnp.mean(x * x, axis=-1, keepdims=True) + EPS) * w


def _in_proj_body(x_ref, nw_ref, sc_ref, sh_ref, w_ref, o_ref, h_s):
    @pl.when(pl.program_id(1) == 0)
    def _():
        h_s[...] = (_rms_rows(x_ref[...], nw_ref[...]) * (1.0 + sc_ref[...]) + sh_ref[...]).astype(BF16)
    o_ref[...] = jnp.dot(h_s[...], w_ref[...], preferred_element_type=F32)


def in_proj(x, norm_w, mod, w_perm, rows_per_mod):
    n, d = x.shape
    tm, tn = PROJ_TM, PROJ_TN

    def mod_spec(part):
        return pl.BlockSpec((None, 1, d), lambda i, j: ((i * tm) // rows_per_mod, 0, part))
    return pl.pallas_call(
        _in_proj_body,
        grid=(n // tm, PROJ_W // tn),
        in_specs=[pl.BlockSpec((tm, d), lambda i, j: (i, 0)), pl.BlockSpec((1, d), lambda i, j: (0, 0)),
                  mod_spec(MOD_SC1), mod_spec(MOD_SH1), pl.BlockSpec((d, tn), lambda i, j: (0, j))],
        out_specs=pl.BlockSpec((tm, tn), lambda i, j: (i, j)),
        out_shape=jax.ShapeDtypeStruct((n, PROJ_W), F32),
        scratch_shapes=[pltpu.VMEM((tm, d), BF16)],
        compiler_params=pltpu.CompilerParams(dimension_semantics=("parallel", "arbitrary"),
                                             vmem_limit_bytes=VMEM_LIMIT),
        name="in_proj",
    )(x, norm_w.reshape(1, d), mod, mod, w_perm)


def _out_proj_body(oa_ref, ob_ref, oc_ref, od_ref, w_ref, x_ref, g_ref, sc_ref, sh_ref, nw_ref, xo_ref, h_ref):
    mix = None
    for k, o_ref in enumerate((oa_ref, ob_ref, oc_ref, od_ref)):
        t = jnp.dot(o_ref[...].astype(BF16), w_ref[k * GROUP_W:(k + 1) * GROUP_W, :], preferred_element_type=F32)
        mix = t if mix is None else mix + t
    x = x_ref[...] + g_ref[...] * mix
    xo_ref[...] = x
    h_ref[...] = _rms_rows(x, nw_ref[...]) * (1.0 + sc_ref[...]) + sh_ref[...]


def out_proj(o_parts, w_out_b, x, mod, norm2_w, rows_per_mod):
    n, d = x.shape
    tm = 256

    def mod_spec(part):
        return pl.BlockSpec((None, 1, d), lambda i: ((i * tm) // rows_per_mod, 0, part))
    part = pl.BlockSpec((tm, GROUP_W), lambda i: (i, 0))
    row = pl.BlockSpec((tm, d), lambda i: (i, 0))
    return pl.pallas_call(
        _out_proj_body,
        grid=(n // tm,),
        in_specs=[part, part, part, part, pl.BlockSpec((D_MIX, d), lambda i: (0, 0)), row,
                  mod_spec(MOD_G1), mod_spec(MOD_SC2), mod_spec(MOD_SH2), pl.BlockSpec((1, d), lambda i: (0, 0))],
        out_specs=[row, row],
        out_shape=[jax.ShapeDtypeStruct((n, d), F32), jax.ShapeDtypeStruct((n, d), F32)],
        compiler_params=pltpu.CompilerParams(dimension_semantics=("parallel",), vmem_limit_bytes=VMEM_LIMIT),
        name="out_proj",
    )(*o_parts, w_out_b, x, mod, mod, mod, norm2_w.reshape(1, d))


def _norm_matmul_body(x_ref, nw_ref, w_ref, y_ref, xn_ref):
    xn = _rms_rows(x_ref[...], nw_ref[...])
    xn_ref[...] = xn
    y_ref[...] = jnp.dot(xn.astype(BF16), w_ref[...], preferred_element_type=F32)


def _matmul_body(x_ref, w_ref, y_ref):
    y_ref[...] = jnp.dot(x_ref[...].astype(BF16), w_ref[...], preferred_element_type=F32)


def norm_matmul(x, col, k, norm_w, w_b):
    n = x.shape[0]
    n_out = w_b.shape[1]
    tm = min(512, n)
    x_spec = pl.BlockSpec((tm, k), lambda i: (i, col))
    w_spec = pl.BlockSpec((k, n_out), lambda i: (0, 0))
    y_spec = pl.BlockSpec((tm, n_out), lambda i: (i, 0))
    params = pltpu.CompilerParams(dimension_semantics=("parallel",), vmem_limit_bytes=VMEM_LIMIT)
    if norm_w is None:
        return pl.pallas_call(_matmul_body, grid=(n // tm,), in_specs=[x_spec, w_spec], out_specs=y_spec,
                              out_shape=jax.ShapeDtypeStruct((n, n_out), F32), compiler_params=params,
                              name="matmul")(x, w_b)
    return pl.pallas_call(
        _norm_matmul_body, grid=(n // tm,),
        in_specs=[x_spec, pl.BlockSpec((1, k), lambda i: (0, 0)), w_spec],
        out_specs=[y_spec, pl.BlockSpec((tm, k), lambda i: (i, 0))],
        out_shape=[jax.ShapeDtypeStruct((n, n_out), F32), jax.ShapeDtypeStruct((n, k), F32)],
        compiler_params=params, name="norm_matmul")(x, norm_w.reshape(1, k), w_b)


ATTN_TQ = Q_BLOCK


def _attn_body(*refs, n_heads, parts, dv, scale, n_seg):
    n_p = len(parts)
    q_refs = refs[:n_p]
    pos = n_p
    segs = []
    for _ in range(n_seg):
        segs.append((refs[pos:pos + n_p], refs[pos + n_p]))
        pos += n_p + 1
    o_ref = refs[pos]
    outs = []
    for h in range(n_heads):
        scores = []
        for k_refs, _ in segs:
            s = None
            for q_ref, k_ref, (d, shared) in zip(q_refs, k_refs, parts):
                qh = q_ref[:, h * d:(h + 1) * d].astype(BF16)
                kh = (k_ref[:, 0:d] if shared else k_ref[:, h * d:(h + 1) * d]).astype(BF16)
                t = _dot_nt(qh, kh)
                s = t if s is None else s + t
            scores.append(s * scale)
        m = None
        for s in scores:
            ms = jnp.max(s, axis=-1, keepdims=True)
            m = ms if m is None else jnp.maximum(m, ms)
        probs = [jnp.exp(s - m) for s in scores]
        denom = None
        for p in probs:
            ps = jnp.sum(p, axis=-1, keepdims=True)
            denom = ps if denom is None else denom + ps
        inv = 1.0 / denom
        o = None
        for p, (_, v_ref) in zip(probs, segs):
            t = jnp.dot((p * inv).astype(BF16), v_ref[:, h * dv:(h + 1) * dv].astype(BF16),
                        preferred_element_type=F32)
            o = t if o is None else o + t
        outs.append(o)
    o_ref[...] = jnp.concatenate(outs, axis=-1)


def attention(q_parts, segments, n_heads, parts, dv, scale):
    b, tq, _ = q_parts[0][0].shape
    args, specs = [], []
    for arr, w, c in q_parts:
        args.append(arr)
        specs.append(pl.BlockSpec((None, ATTN_TQ, w), lambda bi, i, c=c: (bi, i, c)))
    for k_parts, v in segments:
        for arr, w, c in list(k_parts) + [v]:
            args.append(arr)
            specs.append(pl.BlockSpec((None, arr.shape[1], w), lambda bi, i, c=c: (bi, 0, c)))
    body = functools.partial(_attn_body, n_heads=n_heads, parts=parts, dv=dv, scale=scale, n_seg=len(segments))
    return pl.pallas_call(
        body,
        grid=(b, tq // ATTN_TQ),
        in_specs=specs,
        out_specs=pl.BlockSpec((None, ATTN_TQ, n_heads * dv), lambda bi, i: (bi, i, 0)),
        out_shape=jax.ShapeDtypeStruct((b, tq, n_heads * dv), F32),
        compiler_params=pltpu.CompilerParams(dimension_semantics=("parallel", "arbitrary"),
                                             vmem_limit_bytes=VMEM_LIMIT),
        name="attention",
    )(*args)


def mla_weights(w_q_up, w_kv_up):
    wq = w_q_up.reshape(Q_RANK, HB, NOPE + ROPE)
    wq = jnp.concatenate([wq[:, :, :NOPE].reshape(Q_RANK, HB * NOPE), wq[:, :, NOPE:].reshape(Q_RANK, HB * ROPE)], 1)
    wkv = w_kv_up.reshape(KV_RANK, HB, NOPE + VB)
    wkv = jnp.concatenate([wkv[:, :, :NOPE].reshape(KV_RANK, HB * NOPE), wkv[:, :, NOPE:].reshape(KV_RANK, HB * VB)], 1)
    return wq.astype(BF16), wkv.astype(BF16)


def rope_tables(t_len):
    t = jnp.arange(t_len)
    half = ROPE // 2
    inv = 1.0 / (ROPE_THETA ** (jnp.arange(0, half, 2, dtype=F32) / half))
    a_r = (t // GRID_W).astype(F32)[:, None] * inv[None, :]
    a_c = (t % GRID_W).astype(F32)[:, None] * inv[None, :]
    cos = jnp.concatenate([jnp.cos(a_r), jnp.cos(a_r), jnp.cos(a_c), jnp.cos(a_c)], axis=1)
    sin = jnp.concatenate([-jnp.sin(a_r), jnp.sin(a_r), -jnp.sin(a_c), jnp.sin(a_c)], axis=1)
    return cos, sin


def _rope_body(q_ref, small_ref, cos_ref, sin_ref, qo_ref, ko_ref):
    quarter = ROPE // 4
    r_io = lax.broadcasted_iota(jnp.int32, (ROPE, ROPE), 0)
    c_io = lax.broadcasted_iota(jnp.int32, (ROPE, ROPE), 1)
    partner = jnp.where((c_io // quarter) % 2 == 0, c_io + quarter, c_io - quarter)
    swap = jnp.where(r_io == partner, 1.0, 0.0).astype(BF16)
    cos, sin = cos_ref[...], sin_ref[...]

    def rot(x):
        xs = sum(jnp.dot(part, swap, preferred_element_type=F32) for part in _split3_bf16(x))
        return x * cos + xs * sin
    qo_ref[...] = jnp.concatenate([rot(q_ref[:, h * ROPE:(h + 1) * ROPE]) for h in range(HB)], axis=-1)
    ko_ref[...] = rot(small_ref[:, 0:ROPE])


def rope_rotate(q, proj3):
    b, t, _ = q.shape
    cos, sin = rope_tables(t)
    tb = 256
    tab = pl.BlockSpec((tb, ROPE), lambda bi, i: (i, 0))
    return pl.pallas_call(
        _rope_body,
        grid=(b, t // tb),
        in_specs=[pl.BlockSpec((None, tb, HB * ROPE), lambda bi, i: (bi, i, HB * NOPE // (HB * ROPE))),
                  pl.BlockSpec((None, tb, LANES), lambda bi, i: (bi, i, COL_SMALL)), tab, tab],
        out_specs=[pl.BlockSpec((None, tb, HB * ROPE), lambda bi, i: (bi, i, 0)),
                   pl.BlockSpec((None, tb, ROPE), lambda bi, i: (bi, i, 0))],
        out_shape=[jax.ShapeDtypeStruct((b, t, HB * ROPE), F32), jax.ShapeDtypeStruct((b, t, ROPE), F32)],
        compiler_params=pltpu.CompilerParams(dimension_semantics=("parallel", "parallel")),
        name="rope_rotate",
    )(q, proj3, cos, sin)


MLA_PARTS = ((NOPE, False), (ROPE, True))


def mla_mixer(proj2, b, t, p, cache):
    wq_b, wkv_b = p['mla_wq_b'], p['mla_wkv_b']
    q, _ = norm_matmul(proj2, COL_CQ, Q_RANK, p['mla_q_norm_w'], wq_b)
    kv, ckv_n = norm_matmul(proj2, COL_CKV, KV_RANK, p['mla_kv_norm_w'], wkv_b)
    q3, kv3, proj3 = q.reshape(b, t, -1), kv.reshape(b, t, -1), proj2.reshape(b, t, PROJ_W)
    if cache is None:
        o = attention([(q3, HB * NOPE, 0), (q3, HB * ROPE, 2)],
                      [([(kv3, HB * NOPE, 0), (proj3, LANES, COL_SMALL)], (kv3, HB * VB, 1))],
                      HB, MLA_PARTS, VB, MLA_SCALE)
        return o, ckv_n.reshape(b, t, KV_RANK), proj3[:, :, COL_SMALL * LANES:COL_SMALL * LANES + ROPE]
    ckv_ctx, krope_ctx = cache
    n_ctx = ckv_ctx.shape[1]
    kv_ctx = norm_matmul(ckv_ctx.reshape(b * n_ctx, KV_RANK).astype(F32), 0, KV_RANK, None, wkv_b)
    kv_ctx = kv_ctx.reshape(b, n_ctx, -1)
    q_rot, k_rot = rope_rotate(q3, proj3)
    o = attention([(q3, HB * NOPE, 0), (q_rot, HB * ROPE, 0)],
                  [([(kv3, HB * NOPE, 0), (k_rot, ROPE, 0)], (kv3, HB * VB, 1)),
                   ([(kv_ctx, HB * NOPE, 0), (krope_ctx.astype(F32), ROPE, 0)], (kv_ctx, HB * VB, 1))],
                  HB, MLA_PARTS, VB, MLA_SCALE)
    return o, None, None


def _ssd_conv_body(x_ref, b_ref, c_ref, w_ref, bias_ref, o_ref):
    t = x_ref.shape[0]
    xbc = jnp.concatenate([x_ref[...], b_ref[...], c_ref[...]], axis=-1)
    row = lax.broadcasted_iota(jnp.int32, (t, 1), 0)
    prev = jnp.where(row == 0, 0.0, pltpu.roll(xbc, 1, axis=0))
    nxt = jnp.where(row == t - 1, 0.0, pltpu.roll(xbc, t - 1, axis=0))
    o_ref[...] = jax.nn.silu(prev * w_ref[0:1] + xbc * w_ref[1:2] + nxt * w_ref[2:3] + bias_ref[...])


def ssd_conv(proj3, conv_w, conv_b):
    b, t, _ = proj3.shape
    return pl.pallas_call(
        _ssd_conv_body,
        grid=(b,),
        in_specs=[pl.BlockSpec((None, t, GROUP_W), lambda bi: (bi, 0, COL_SX)),
                  pl.BlockSpec((None, t, GC * NC), lambda bi: (bi, 0, COL_SB)),
                  pl.BlockSpec((None, t, GC * NC), lambda bi: (bi, 0, COL_SC)),
                  pl.BlockSpec((CONV_W, CONV_CH), lambda bi: (0, 0)), pl.BlockSpec((1, CONV_CH), lambda bi: (0, 0))],
        out_specs=pl.BlockSpec((None, t, CONV_CH), lambda bi: (bi, 0, 0)),
        out_shape=jax.ShapeDtypeStruct((b, t, CONV_CH), F32),
        compiler_params=pltpu.CompilerParams(dimension_semantics=("parallel",), vmem_limit_bytes=VMEM_LIMIT),
        name="ssd_conv",
    )(proj3, proj3, proj3, conv_w.astype(F32), conv_b.astype(F32).reshape(1, CONV_CH))


SSD_PAIRS = HC // 2


def _ssd_direction(rev, xbc_ref, dtc_ref, dtr_ref, bias_c, a_c, bias_r, a_r, st_s, d, y_ref):
    xbc = xbc_ref[...]
    dt_c = jax.nn.softplus(dtc_ref[...] + bias_c)
    la_c = dt_c * a_c
    la_r = jax.nn.softplus(dtr_ref[...] + bias_r) * a_r
    t_io = lax.broadcasted_iota(jnp.int32, (CHUNK, CHUNK), 0)
    s_io = lax.broadcasted_iota(jnp.int32, (CHUNK, CHUNK), 1)
    causal = (s_io >= t_io) if rev else (s_io <= t_io)
    tri = jnp.where(causal, 1.0, 0.0).astype(BF16)
    tri_t = jnp.where((t_io >= s_io) if rev else (t_io <= s_io), 1.0, 0.0).astype(BF16)
    cum_c = sum(jnp.dot(tri, part, preferred_element_type=F32) for part in _split3_bf16(la_c))
    cum_r = sum(jnp.dot(part, tri_t, preferred_element_type=F32) for part in _split3_bf16(la_r))
    end = 0 if rev else CHUNK - 1
    lane_lo = lax.broadcasted_iota(jnp.int32, (CHUNK, LANES), 1) < PC
    row_lo = lax.broadcasted_iota(jnp.int32, (LANES, NC), 0) < PC
    cbs = []
    for g in range(GC):
        bg = xbc[:, GROUP_W + g * NC:GROUP_W + (g + 1) * NC].astype(BF16)
        cg = xbc[:, GROUP_W + GC * NC + g * NC:GROUP_W + GC * NC + (g + 1) * NC].astype(BF16)
        cbs.append((bg, cg, _dot_nt(cg, bg)))
    outs = []
    for pr in range(SSD_PAIRS):
        ha, hb = 2 * pr, 2 * pr + 1
        bg, cg, cb = cbs[ha // (HC // GC)]
        xdt = xbc[:, pr * LANES:(pr + 1) * LANES] * jnp.where(lane_lo, dt_c[:, ha:ha + 1], dt_c[:, hb:hb + 1])
        ys = []
        for hx in (ha, hb):
            decay = jnp.exp(jnp.where(causal, cum_c[:, hx:hx + 1] - cum_r[hx:hx + 1, :], 0.0))
            m = (cb * jnp.where(causal, decay, 0.0)).astype(BF16)
            ys.append(jnp.dot(m, xdt.astype(BF16), preferred_element_type=F32))
        y = jnp.where(lane_lo, ys[0], ys[1])
        st = st_s[d, pr]
        e_cum = jnp.where(lane_lo, jnp.exp(cum_c[:, ha:ha + 1]), jnp.exp(cum_c[:, hb:hb + 1]))
        y = y + _dot_nt(cg, st.astype(BF16)) * e_cum
        outs.append(y)
        last_a, last_b = cum_c[end:end + 1, ha:ha + 1], cum_c[end:end + 1, hb:hb + 1]
        w = jnp.where(lane_lo, jnp.exp(last_a - cum_c[:, ha:ha + 1]), jnp.exp(last_b - cum_c[:, hb:hb + 1]))
        e_last = jnp.where(row_lo, jnp.exp(last_a), jnp.exp(last_b))
        st_s[d, pr] = e_last * st + _dot_tn((xdt * w).astype(BF16), bg)
    y_ref[...] = jnp.concatenate(outs, axis=-1)


def _ssd_body(xf_ref, xb_ref, dcf_ref, dcb_ref, drf_ref, drb_ref, bc_ref, ac_ref, br_ref, ar_ref, s0_ref,
              yf_ref, yb_ref, sfin_ref, st_s):
    i = pl.program_id(1)

    @pl.when(i == 0)
    def _():
        st_s[...] = s0_ref[...]

    _ssd_direction(False, xf_ref, dcf_ref, drf_ref, bc_ref[0], ac_ref[0], br_ref[0], ar_ref[0], st_s, 0, yf_ref)
    _ssd_direction(True, xb_ref, dcb_ref, drb_ref, bc_ref[1], ac_ref[1], br_ref[1], ar_ref[1], st_s, 1, yb_ref)

    @pl.when(i == pl.num_programs(1) - 1)
    def _():
        sfin_ref[...] = st_s[...]


def ssd_scan(xbc, dt_f, dt_b, dt_bias, a_log, s0):
    b, t, _ = xbc.shape
    n = t // CHUNK
    a = -jnp.exp(a_log.astype(F32))
    bias = dt_bias.astype(F32)
    dcf, dcb = dt_f.reshape(b, n, CHUNK, HC), dt_b.reshape(b, n, CHUNK, HC)
    drf, drb = dcf.transpose(0, 1, 3, 2), dcb.transpose(0, 1, 3, 2)
    x_f = pl.BlockSpec((None, CHUNK, CONV_CH), lambda bi, i: (bi, i, 0))
    x_b = pl.BlockSpec((None, CHUNK, CONV_CH), lambda bi, i: (bi, n - 1 - i, 0))
    y_f = pl.BlockSpec((None, CHUNK, GROUP_W), lambda bi, i: (bi, i, 0))
    y_b = pl.BlockSpec((None, CHUNK, GROUP_W), lambda bi, i: (bi, n - 1 - i, 0))
    c_f = pl.BlockSpec((None, None, CHUNK, HC), lambda bi, i: (bi, i, 0, 0))
    c_b = pl.BlockSpec((None, None, CHUNK, HC), lambda bi, i: (bi, n - 1 - i, 0, 0))
    r_f = pl.BlockSpec((None, None, HC, CHUNK), lambda bi, i: (bi, i, 0, 0))
    r_b = pl.BlockSpec((None, None, HC, CHUNK), lambda bi, i: (bi, n - 1 - i, 0, 0))
    p_c = pl.BlockSpec((2, 1, HC), lambda bi, i: (0, 0, 0))
    p_r = pl.BlockSpec((2, HC, 1), lambda bi, i: (0, 0, 0))
    st = pl.BlockSpec((None, 2, SSD_PAIRS, LANES, NC), lambda bi, i: (bi, 0, 0, 0, 0))
    y_fwd, y_bwd, s_fin = pl.pallas_call(
        _ssd_body,
        grid=(b, n),
        in_specs=[x_f, x_b, c_f, c_b, r_f, r_b, p_c, p_c, p_r, p_r, st],
        out_specs=[y_f, y_b, st],
        out_shape=[jax.ShapeDtypeStruct((b, t, GROUP_W), F32), jax.ShapeDtypeStruct((b, t, GROUP_W), F32),
                   jax.ShapeDtypeStruct((b, 2, SSD_PAIRS, LANES, NC), F32)],
        scratch_shapes=[pltpu.VMEM((2, SSD_PAIRS, LANES, NC), F32)],
        compiler_params=pltpu.CompilerParams(dimension_semantics=("parallel", "arbitrary"),
                                             vmem_limit_bytes=VMEM_LIMIT),
        name="ssd_scan",
    )(xbc, xbc, dcf, dcb, drf, drb, bias.reshape(2, 1, HC), a.reshape(2, 1, HC), bias.reshape(2, HC, 1),
      a.reshape(2, HC, 1), s0.astype(F32).reshape(b, 2, SSD_PAIRS, LANES, NC))
    return y_fwd, y_bwd, s_fin.reshape(b, 2, HC, PC, NC)


def _ssd_combine_body(x_ref, yf_ref, yb_ref, z_ref, d_ref, w_ref, o_ref):
    y = (d_ref[...] * x_ref[...] + yf_ref[...] + yb_ref[...]) * jax.nn.silu(z_ref[...])
    o_ref[...] = _rms_rows(y, w_ref[...])


def ssd_combine(xbc2, y_f, y_b, proj2, d_skip, norm_w):
    n = xbc2.shape[0]
    rows = 256
    blk = pl.BlockSpec((rows, GROUP_W), lambda i: (i, 0))
    par = pl.BlockSpec((1, GROUP_W), lambda i: (0, 0))
    return pl.pallas_call(
        _ssd_combine_body,
        grid=(n // rows,),
        in_specs=[blk, blk, blk, pl.BlockSpec((rows, GROUP_W), lambda i: (i, COL_SZ)), par, par],
        out_specs=blk,
        out_shape=jax.ShapeDtypeStruct((n, GROUP_W), F32),
        compiler_params=pltpu.CompilerParams(dimension_semantics=("parallel",)),
        name="ssd_combine",
    )(xbc2, y_f.reshape(n, GROUP_W), y_b.reshape(n, GROUP_W), proj2,
      jnp.repeat(d_skip.astype(F32), PC).reshape(1, GROUP_W), norm_w.astype(F32).reshape(1, GROUP_W))


def ssd_mixer(proj2, b, t, p, init_state):
    proj3 = proj2.reshape(b, t, PROJ_W)
    xbc = ssd_conv(proj3, p['ssd_conv_w'], p['ssd_conv_b'])
    dt0 = COL_SMALL * LANES + SMALL_DT
    dt_f, dt_b = proj3[:, :, dt0:dt0 + HC], proj3[:, :, dt0 + HC:dt0 + 2 * HC]
    s0 = jnp.zeros((b, 2, HC, PC, NC), F32) if init_state is None else init_state
    y_f, y_b, s_fin = ssd_scan(xbc, dt_f, dt_b, p['ssd_dt_bias'], p['ssd_a_log'], s0)
    return ssd_combine(xbc.reshape(b * t, CONV_CH), y_f, y_b, proj2, p['ssd_d'], p['ssd_norm_w']), s_fin


def _peer_finish_body(ot_ref, x_ref, g_ref, nw_ref, o_ref, *, final_norm):
    x = x_ref[...] + g_ref[...] * ot_ref[...].T
    o_ref[...] = _rms_rows(x, nw_ref[...]) if final_norm else x


def peer_finish(out_t, tok0, x, mod, rows_per_mod, final_norm_w=None):
    n, d = x.shape
    tm = 256
    norm_w = jnp.ones((1, d), F32) if final_norm_w is None else final_norm_w.astype(F32).reshape(1, d)
    return pl.pallas_call(
        functools.partial(_peer_finish_body, final_norm=final_norm_w is not None),
        grid=(n // tm,),
        in_specs=[pl.BlockSpec((d, tm), lambda i: (0, tok0 // tm + i)), pl.BlockSpec((tm, d), lambda i: (i, 0)),
                  pl.BlockSpec((None, 1, d), lambda i: ((i * tm) // rows_per_mod, 0, MOD_G2)),
                  pl.BlockSpec((1, d), lambda i: (0, 0))],
        out_specs=pl.BlockSpec((tm, d), lambda i: (i, 0)),
        out_shape=jax.ShapeDtypeStruct((n, d), F32),
        compiler_params=pltpu.CompilerParams(dimension_semantics=("parallel",), vmem_limit_bytes=VMEM_LIMIT),
        name="peer_finish",
    )(out_t, x, mod, norm_w)


def trunk_layer(x2, b, t, mod, p, cache):
    rows_per_mod = (b * t) // mod.shape[0]
    proj2 = in_proj(x2, p['norm1_w'], mod, p['w_in_b'], rows_per_mod)
    proj3 = proj2.reshape(b, t, PROJ_W)
    latent = cache is not None
    s0 = cache[0].astype(F32) if latent else jnp.zeros((b, 2, HA, DKA, DVA), F32)
    o_f, o_b, st_a = hgrn_scan(proj3, p['lb'], s0)
    o_a = hgrn_combine(o_f, o_b, proj2, p['hgrn_norm_w'])
    o_c, st_c = ssd_mixer(proj2, b, t, p, cache[1] if latent else None)
    na_q, na_k, na_v = (proj3, GROUP_W, COL_NQ), (proj3, GROUP_W, COL_NK), (proj3, GROUP_W, COL_NV)
    if latent:
        o_m, _, _ = mla_mixer(proj2, b, t, p, (cache[2], cache[3]))
        o_d = natten_latent(proj3, p['na_rpb'], cache[4], cache[5])
        new_state = None
    else:
        o_m, ckv, krope = mla_mixer(proj2, b, t, p, None)
        o_d = attention([na_q], [([na_k], na_v)], HD, ((DHD, False),), DHD, DHD ** -0.5)
        k_na = proj3[:, :, COL_NK * GROUP_W:(COL_NK + 1) * GROUP_W].reshape(b, t, HD, DHD)
        v_na = proj3[:, :, COL_NV * GROUP_W:(COL_NV + 1) * GROUP_W].reshape(b, t, HD, DHD)
        new_state = (st_a, st_c, ckv, krope, k_na, v_na)
    n = b * t
    x2, h2 = out_proj([o_a, o_m.reshape(n, GROUP_W), o_c, o_d.reshape(n, GROUP_W)], p['w_out_b'], x2, mod,
                      p['norm2_w'], rows_per_mod)
    return x2, h2, new_state


def _final_norm_body(x_ref, w_ref, o_ref):
    x = x_ref[...]
    y = x * lax.rsqrt(jnp.mean(x * x, axis=-1, keepdims=True) + EPS)
    o_ref[...] = y * w_ref[...]


def final_rmsnorm(x, w):
    b, t, d = x.shape
    rows = 512
    out = pl.pallas_call(
        _final_norm_body,
        grid=(b * t // rows,),
        in_specs=[pl.BlockSpec((rows, d), lambda i: (i, 0)), pl.BlockSpec((1, d), lambda i: (0, 0))],
        out_specs=pl.BlockSpec((rows, d), lambda i: (i, 0)),
        out_shape=jax.ShapeDtypeStruct((b * t, d), x.dtype),
        name="final_rmsnorm",
    )(x.reshape(b * t, d), w.reshape(1, d))
    return out.reshape(b, t, d)


def kernel(x_prompt, x_sample, c, state_hgrn, state_ssd, cache_mla_ckv, cache_mla_krope, cache_na_k,
           cache_na_v, c_ctx, w_ada, b_ada, norm1_w, norm2_w, w_in, w_out, hgrn_lb_logits, hgrn_norm_w,
           mla_q_norm_w, mla_w_q_up, mla_kv_norm_w, mla_w_kv_up, ssd_conv_w, ssd_conv_b, ssd_dt_bias,
           ssd_a_log, ssd_d, ssd_norm_w, na_rpb, peer_w_q, peer_sub_keys, peer_u, peer_v, final_norm_w):
    lb_soft = jax.nn.softmax(hgrn_lb_logits.astype(F32), axis=0)
    lb_all = jnp.cumsum(lb_soft, axis=0) - lb_soft[0]
    stacked = {'w_ada': w_ada, 'b_ada': b_ada, 'norm1_w': norm1_w, 'norm2_w': norm2_w, 'w_in': w_in,
               'w_out': w_out, 'hgrn_norm_w': hgrn_norm_w, 'mla_q_norm_w': mla_q_norm_w,
               'mla_w_q_up': mla_w_q_up, 'mla_kv_norm_w': mla_kv_norm_w, 'mla_w_kv_up': mla_w_kv_up,
               'ssd_conv_w': ssd_conv_w, 'ssd_conv_b': ssd_conv_b, 'ssd_dt_bias': ssd_dt_bias,
               'ssd_a_log': ssd_a_log, 'ssd_d': ssd_d, 'ssd_norm_w': ssd_norm_w, 'na_rpb': na_rpb,
               'peer_w_q': peer_w_q, 'peer_sub_keys': peer_sub_keys, 'peer_u': peer_u, 'peer_v': peer_v}

    ub_all, vt_all = peer_prep_tables(peer_u, peer_v)
    n_ctx, n_lat = BATCH * SEQ, DEC_BATCH * DEC_SEQ
    cond = jnp.concatenate([c_ctx[None, :], c, jnp.zeros((8 - 1 - DEC_BATCH, D_MODEL), F32)], axis=0)

    xp, xs = x_prompt.reshape(n_ctx, D_MODEL), x_sample.reshape(n_lat, D_MODEL)
    ctx_states = []
    for l in range(DEPTH):
        p = {name: arr[l] for name, arr in stacked.items()}
        p['lb'] = lb_all[l]
        p['w_in_b'] = in_proj_weight(w_in[l])
        p['w_out_b'] = w_out[l].astype(BF16)
        p['mla_wq_b'], p['mla_wkv_b'] = mla_weights(mla_w_q_up[l], mla_w_kv_up[l])
        mod = adaln(cond, w_ada[l], b_ada[l])
        mod_p, mod_s = mod[0:1, None, :], mod[1:1 + DEC_BATCH, None, :]
        xp, h2p, st = trunk_layer(xp, BATCH, SEQ, mod_p, p, None)
        ctx_states.append(st)
        cache_l = (state_hgrn[:, l], state_ssd[:, l], cache_mla_ckv[:, l], cache_mla_krope[:, l],
                   cache_na_k[:, l], cache_na_v[:, l])
        xs, h2s, _ = trunk_layer(xs, DEC_BATCH, DEC_SEQ, mod_s, p, cache_l)
        out_t = peer_ffn(h2p, h2s, peer_w_q[l].T.astype(BF16), peer_sub_keys[l].astype(BF16), ub_all[l], vt_all[l])
        last_w = final_norm_w if l == DEPTH - 1 else None
        xp = peer_finish(out_t, 0, xp, mod_p, n_ctx, last_w)
        xs = peer_finish(out_t, n_ctx, xs, mod_s, DEC_SEQ, last_w)

    y_prompt = xp.reshape(BATCH, SEQ, D_MODEL)
    y_sample = xs.reshape(DEC_BATCH, DEC_SEQ, D_MODEL)
    new_state_hgrn = jnp.stack([s[0] for s in ctx_states], axis=1)
    new_state_ssd = jnp.stack([s[1] for s in ctx_states], axis=1)
    new_cache_mla_ckv = jnp.stack([s[2] for s in ctx_states], axis=1)
    new_cache_mla_krope = jnp.stack([s[3] for s in ctx_states], axis=1)
    new_cache_na_k = jnp.stack([s[4] for s in ctx_states], axis=1)
    new_cache_na_v = jnp.stack([s[5] for s in ctx_states], axis=1)
    return (y_prompt, y_sample, new_state_hgrn, new_state_ssd, new_cache_mla_ckv, new_cache_mla_krope,
            new_cache_na_k, new_cache_na_v)
```

```python
import functools
import math
import jax
import jax.numpy as jnp
from jax import lax
import numpy as np
from jax.experimental import pallas as pl
from jax.experimental.pallas import tpu as pltpu

D_MODEL = 2048
BATCH = 32
SEQ = 256
DEPTH = 2
DEC_BATCH = 4
DEC_SEQ = 1024
PAST_LEN = 256

GRID_W = 64
EPS = 1e-6
ROPE_THETA = 10000.0
Q_BLOCK = 128
CHUNK = 64
N_MIXERS = 4
GROUP_W = D_MODEL // N_MIXERS
D_MIX = N_MIXERS * GROUP_W
HA = 4
DKA = GROUP_W // HA
DVA = GROUP_W // HA
LB_FLOOR = 1e-30
HB = 4
Q_RANK = D_MODEL // 4
KV_RANK = D_MODEL // 8
NOPE = 128
ROPE = 64
VB = GROUP_W // HB
MLA_SCALE = (NOPE + ROPE) ** -0.5
HC = 8
PC = GROUP_W // HC
NC = 128
GC = 2
CONV_W = 3
CONV_CH = GROUP_W + 2 * GC * NC
HD = 8
DHD = GROUP_W // HD
WIN_R = 8
WIN_C = 16
N_KEYS = 128
N_EXPERTS = N_KEYS * N_KEYS
PEER_HEADS = 8
PEER_QDIM = 256
PEER_TOPK = 16

IN_SPLITS = (HA * DKA, HA * DKA, HA * DKA, HA * DVA, HA * DVA,
             Q_RANK, KV_RANK, ROPE,
             GROUP_W, GROUP_W, GC * NC, GC * NC, HC, HC,
             GROUP_W, GROUP_W, GROUP_W)
IN_W = sum(IN_SPLITS)
F32 = jnp.float32


BF16 = jnp.bfloat16
LANES = 128
VMEM_LIMIT = 56 * 1024 * 1024
NEG_BIG = -1e30


def _dot_nt(a, b):
    return lax.dot_general(a, b, (((1,), (1,)), ((), ())), preferred_element_type=F32)


def _dot_tn(a, b):
    return lax.dot_general(a, b, (((0,), (0,)), ((), ())), preferred_element_type=F32)


HG_SUB = 16
HG_NSUB = CHUNK // HG_SUB


def _split3_bf16(x):
    hi = x.astype(BF16)
    r1 = x - hi.astype(F32)
    mid = r1.astype(BF16)
    lo = (r1 - mid.astype(F32)).astype(BF16)
    return hi, mid, lo


def _hgrn_direction(rev, q_ref, f_ref, v_ref, la, lc, om, st_s, d, o_ref):
    fx = f_ref[...]
    logf = jnp.logaddexp(la, lc + jax.nn.log_sigmoid(fx))
    kk = om * jax.nn.sigmoid(-fx)
    qq = jax.nn.silu(q_ref[...]) * (DKA ** -0.5)
    vv = v_ref[...]
    t_io = lax.broadcasted_iota(jnp.int32, (CHUNK, CHUNK), 0)
    s_io = lax.broadcasted_iota(jnp.int32, (CHUNK, CHUNK), 1)
    tri = jnp.where((s_io >= t_io) if rev else (s_io <= t_io), 1.0, 0.0).astype(BF16)
    c = sum(jnp.dot(tri, part, preferred_element_type=F32) for part in _split3_bf16(logf))
    row = lax.broadcasted_iota(jnp.int32, (CHUNK, 1), 0)
    sub_row = lax.broadcasted_iota(jnp.int32, (HG_SUB, 1), 0)
    lane = lax.broadcasted_iota(jnp.int32, (HG_SUB, CHUNK), 1)
    outs = []
    for h in range(HA):
        cs = slice(h * DKA, (h + 1) * DKA)
        ch, qh, kh, vh = c[:, cs], qq[:, cs], kk[:, cs], vv[:, cs]
        st = st_s[d, h]
        o = _dot_nt((qh * jnp.exp(ch)).astype(BF16), st.astype(BF16))
        att = jnp.zeros((CHUNK, CHUNK), F32)
        for i in range(HG_NSUB):
            if rev:
                if i == HG_NSUB - 1:
                    continue
                c_ref = ch[(i + 1) * HG_SUB:(i + 1) * HG_SUB + 1]
                k_side = row >= (i + 1) * HG_SUB
            else:
                if i == 0:
                    continue
                c_ref = ch[i * HG_SUB - 1:i * HG_SUB]
                k_side = row < i * HG_SUB
            q_side = (row >= i * HG_SUB) & (row < (i + 1) * HG_SUB)
            qs = jnp.where(q_side, qh * jnp.exp(jnp.where(q_side, ch - c_ref, 0.0)), 0.0)
            ks = jnp.where(k_side, kh * jnp.exp(jnp.where(k_side, c_ref - ch, 0.0)), 0.0)
            att = att + _dot_nt(qs.astype(BF16), ks.astype(BF16))
        strips = []
        for i in range(HG_NSUB):
            blk = slice(i * HG_SUB, (i + 1) * HG_SUB)
            cb, qb, kb = ch[blk], qh[blk], kh[blk]
            strip = jnp.zeros((HG_SUB, CHUNK), F32)
            for s in range(HG_SUB):
                causal = (sub_row <= s) if rev else (sub_row >= s)
                w = jnp.exp(jnp.where(causal, cb - cb[s:s + 1], 0.0))
                col = jnp.sum(jnp.where(causal, w * qb * kb[s:s + 1], 0.0), axis=-1, keepdims=True)
                strip = jnp.where(lane == i * HG_SUB + s, col, strip)
            strips.append(strip)
        att = att + jnp.concatenate(strips, axis=0)
        o = o + jnp.dot(att.astype(BF16), vh.astype(BF16), preferred_element_type=F32)
        outs.append(o)
        c_end = ch[0:1] if rev else ch[CHUNK - 1:CHUNK]
        kd = kh * jnp.exp(c_end - ch)
        st_s[d, h] = st * jnp.exp(c_end) + _dot_tn(vh.astype(BF16), kd.astype(BF16))
    o_ref[...] = jnp.concatenate(outs, axis=-1)


def _hgrn_body(qf_ref, ff_ref, vf_ref, qb_ref, fb_ref, vb_ref, la_ref, lc_ref, om_ref, s0_ref,
               of_ref, ob_ref, sfin_ref, st_s):
    i = pl.program_id(1)

    @pl.when(i == 0)
    def _():
        for d in range(2):
            for h in range(HA):
                st_s[d, h] = s0_ref[d, h].T

    _hgrn_direction(False, qf_ref, ff_ref, vf_ref, la_ref[0:1], lc_ref[0:1], om_ref[0:1], st_s, 0, of_ref)
    _hgrn_direction(True, qb_ref, fb_ref, vb_ref, la_ref[1:2], lc_ref[1:2], om_ref[1:2], st_s, 1, ob_ref)

    @pl.when(i == pl.num_programs(1) - 1)
    def _():
        for d in range(2):
            for h in range(HA):
                sfin_ref[d, h] = st_s[d, h].T


def hgrn_scan(proj3, lb, s0):
    b, t, _ = proj3.shape
    w = HA * DKA
    n = t // CHUNK
    lb = lb.astype(F32)
    la = jnp.log(jnp.maximum(lb, LB_FLOOR))
    lc = jnp.log1p(-lb)
    om = 1.0 - lb

    def fwd_col(c):
        return pl.BlockSpec((None, CHUNK, w), lambda bi, i: (bi, i, c))

    def bwd_col(c):
        return pl.BlockSpec((None, CHUNK, w), lambda bi, i: (bi, n - 1 - i, c))
    fwd, bwd = fwd_col(0), bwd_col(0)
    par = pl.BlockSpec((2, w), lambda bi, i: (0, 0))
    st = pl.BlockSpec((None, 2, HA, DKA, DVA), lambda bi, i: (bi, 0, 0, 0, 0))
    q, f_fwd, f_bwd, v = proj3, proj3, proj3, proj3
    return pl.pallas_call(
        _hgrn_body,
        grid=(b, n),
        in_specs=[fwd_col(COL_HQ), fwd_col(COL_HFF), fwd_col(COL_HI), bwd_col(COL_HQ), bwd_col(COL_HFB),
                  bwd_col(COL_HI), par, par, par, st],
        out_specs=[fwd, bwd, st],
        out_shape=[jax.ShapeDtypeStruct((b, t, w), F32), jax.ShapeDtypeStruct((b, t, w), F32),
                   jax.ShapeDtypeStruct((b, 2, HA, DKA, DVA), F32)],
        scratch_shapes=[pltpu.VMEM((2, HA, DVA, DKA), F32)],
        compiler_params=pltpu.CompilerParams(dimension_semantics=("parallel", "arbitrary"),
                                             vmem_limit_bytes=VMEM_LIMIT),
        name="hgrn_scan",
    )(q, f_fwd, v, q, f_bwd, v, la, lc, om, s0)


def _hgrn_combine_body(of_ref, ob_ref, g_ref, w_ref, o_ref):
    y = of_ref[...] + ob_ref[...]
    g = g_ref[...]
    outs = []
    for h in range(HA):
        cs = slice(h * DVA, (h + 1) * DVA)
        yh = y[:, cs]
        yn = yh * lax.rsqrt(jnp.mean(yh * yh, axis=-1, keepdims=True) + EPS)
        outs.append(yn * w_ref[:, cs] * jax.nn.silu(g[:, cs]))
    o_ref[...] = jnp.concatenate(outs, axis=-1)


def hgrn_combine(o_f, o_b, proj2, norm_w):
    b, t, w = o_f.shape
    rows = 256
    blk = pl.BlockSpec((rows, w), lambda i: (i, 0))
    return pl.pallas_call(
        _hgrn_combine_body,
        grid=(b * t // rows,),
        in_specs=[blk, blk, pl.BlockSpec((rows, w), lambda i: (i, COL_HG)), pl.BlockSpec((1, w), lambda i: (0, 0))],
        out_specs=blk,
        out_shape=jax.ShapeDtypeStruct((b * t, w), F32),
        compiler_params=pltpu.CompilerParams(dimension_semantics=("parallel",)),
        name="hgrn_combine",
    )(o_f.reshape(b * t, w), o_b.reshape(b * t, w), proj2, norm_w.astype(F32).reshape(1, w))


NA_ROWS = DEC_SEQ // GRID_W
NA_WR = min(WIN_R, NA_ROWS)
NA_WIN = NA_WR * GRID_W


def natten_bias_table(rpb):
    cols = jnp.arange(GRID_W)
    start = jnp.clip(cols - WIN_C // 2, 0, GRID_W - WIN_C)
    in_win = (cols[None, :] >= start[:, None]) & (cols[None, :] < start[:, None] + WIN_C)
    c_off = jnp.clip(cols[None, :] - cols[:, None] + (WIN_C - 1), 0, 2 * WIN_C - 2)
    r_off = jnp.arange(NA_WR)[:, None] - (NA_WR - 1) + jnp.arange(NA_WR)[None, :] + (WIN_R - 1)
    tab = rpb.astype(F32)[:, r_off][:, :, :, c_off]
    tab = jnp.where(in_win[None, None, None], tab, NEG_BIG)
    return tab.transpose(0, 1, 3, 2, 4).reshape(HD, NA_WR, GRID_W, NA_WIN)


def _natten_body(q_ref, k_ref, v_ref, kc_ref, vc_ref, bias_ref, o_ref):
    r = pl.program_id(1)
    rs = jnp.clip(r - NA_WR // 2, 0, NA_ROWS - NA_WR)
    win = pl.ds(pl.multiple_of(rs * GRID_W, GRID_W), NA_WIN)
    q = q_ref[...] * (DHD ** -0.5)
    outs = []
    for h in range(HD):
        cs = slice(h * DHD, (h + 1) * DHD)
        qh = q[:, cs].astype(BF16)
        s_loc = _dot_nt(qh, k_ref[win, cs].astype(BF16)) + bias_ref[h]
        s_ctx = _dot_nt(qh, kc_ref[:, cs].astype(BF16))
        m = jnp.maximum(jnp.max(s_loc, axis=-1, keepdims=True), jnp.max(s_ctx, axis=-1, keepdims=True))
        p_loc = jnp.exp(s_loc - m)
        p_ctx = jnp.exp(s_ctx - m)
        inv = 1.0 / (jnp.sum(p_loc, axis=-1, keepdims=True) + jnp.sum(p_ctx, axis=-1, keepdims=True))
        outs.append(jnp.dot((p_loc * inv).astype(BF16), v_ref[win, cs].astype(BF16), preferred_element_type=F32)
                    + jnp.dot((p_ctx * inv).astype(BF16), vc_ref[:, cs].astype(BF16),
                              preferred_element_type=F32))
    o_ref[...] = jnp.concatenate(outs, axis=-1)


def natten_latent(proj3, rpb, k_ctx, v_ctx):
    b, t, _ = proj3.shape
    w = GROUP_W
    assert t == DEC_SEQ
    n_ctx = k_ctx.shape[1]
    bias = natten_bias_table(rpb)
    q = k = v = proj3

    def delta_idx(r):
        return jnp.clip(r - NA_WR // 2, 0, NA_ROWS - NA_WR) - r + (NA_WR - 1)
    ctx = pl.BlockSpec((None, n_ctx, w), lambda bi, r: (bi, 0, 0))
    return pl.pallas_call(
        _natten_body,
        grid=(b, NA_ROWS),
        in_specs=[pl.BlockSpec((None, GRID_W, w), lambda bi, r: (bi, r, COL_NQ)),
                  pl.BlockSpec((None, t, w), lambda bi, r: (bi, 0, COL_NK)),
                  pl.BlockSpec((None, t, w), lambda bi, r: (bi, 0, COL_NV)), ctx, ctx,
                  pl.BlockSpec((HD, None, GRID_W, NA_WIN), lambda bi, r: (0, delta_idx(r), 0, 0))],
        out_specs=pl.BlockSpec((None, GRID_W, w), lambda bi, r: (bi, r, 0)),
        out_shape=jax.ShapeDtypeStruct((b, t, w), F32),
        compiler_params=pltpu.CompilerParams(dimension_semantics=("parallel", "arbitrary"),
                                             vmem_limit_bytes=VMEM_LIMIT),
        name="natten_latent",
    )(q, k, v, k_ctx.reshape(b, n_ctx, w), v_ctx.reshape(b, n_ctx, w), bias)


PEER_KDIM = PEER_QDIM // 2
PEER_HP = PEER_HEADS * 2
PEER_PREP_TE = 512
PEER_SCORE_TM = 512
PEER_SELECT_TM = 256
PEER_TM = 512
PEER_TE = 1024
PEER_ROWS = PEER_TE // N_KEYS
PEER_CAND = [(a, b) for a in range(PEER_TOPK) for b in range(PEER_TOPK) if (a + 1) * (b + 1) <= PEER_TOPK]


def _peer_prep_body(u_ref, v_ref, ub_ref, vt_ref):
    ub_ref[...] = u_ref[...].astype(BF16)
    vt_ref[...] = v_ref[...].T.astype(BF16)


def peer_prep_tables(peer_u, peer_v):
    depth, n_e, d = peer_u.shape
    te = PEER_PREP_TE
    return pl.pallas_call(
        _peer_prep_body,
        grid=(depth, n_e // te),
        in_specs=[pl.BlockSpec((None, te, d), lambda l, j: (l, j, 0)),
                  pl.BlockSpec((None, te, d), lambda l, j: (l, j, 0))],
        out_specs=[pl.BlockSpec((None, te, d), lambda l, j: (l, j, 0)),
                   pl.BlockSpec((None, d, te), lambda l, j: (l, 0, j))],
        out_shape=[jax.ShapeDtypeStruct((depth, n_e, d), BF16), jax.ShapeDtypeStruct((depth, d, n_e), BF16)],
        compiler_params=pltpu.CompilerParams(dimension_semantics=("parallel", "parallel"),
                                             vmem_limit_bytes=VMEM_LIMIT),
        name="peer_prep_tables",
    )(peer_u, peer_v)


def _peer_score_body(ha_ref, hb_ref, wqt_ref, sk_ref, xt_ref, st_ref, *, tiles_a):
    h = jnp.where(pl.program_id(0) < tiles_a, ha_ref[...], hb_ref[...])
    xt = h.T.astype(BF16)
    xt_ref[...] = xt
    qt = jnp.dot(wqt_ref[...], xt, preferred_element_type=F32).astype(BF16)
    for hp in range(PEER_HP):
        st_ref[hp] = jnp.dot(sk_ref[hp % 2], qt[hp * PEER_KDIM:(hp + 1) * PEER_KDIM, :],
                             preferred_element_type=F32)


def peer_scores(h_a, h_b, wq_t, sub_keys):
    n_a, d = h_a.shape
    n = n_a + h_b.shape[0]
    tm = PEER_SCORE_TM
    tiles_a = n_a // tm
    return pl.pallas_call(
        functools.partial(_peer_score_body, tiles_a=tiles_a),
        grid=(n // tm,),
        in_specs=[pl.BlockSpec((tm, d), lambda i: (jnp.minimum(i, tiles_a - 1), 0)),
                  pl.BlockSpec((tm, d), lambda i: (jnp.maximum(i - tiles_a, 0), 0)),
                  pl.BlockSpec(wq_t.shape, lambda i: (0, 0)),
                  pl.BlockSpec(sub_keys.shape, lambda i: (0, 0, 0))],
        out_specs=[pl.BlockSpec((d, tm), lambda i: (0, i)),
                   pl.BlockSpec((PEER_HP, N_KEYS, tm), lambda i: (0, 0, i))],
        out_shape=[jax.ShapeDtypeStruct((d, n), BF16), jax.ShapeDtypeStruct((PEER_HP, N_KEYS, n), F32)],
        compiler_params=pltpu.CompilerParams(dimension_semantics=("parallel",), vmem_limit_bytes=VMEM_LIMIT),
        name="peer_scores",
    )(h_a, h_b, wq_t, sub_keys)


SUBLANES = 8
assert N_KEYS // SUBLANES == PEER_TOPK


def _bitonic_merge_desc(xs):
    xs = list(xs)
    j = len(xs) // 2
    while j >= 1:
        for i in range(len(xs)):
            if i & j == 0:
                xs[i], xs[i | j] = jnp.maximum(xs[i], xs[i | j]), jnp.minimum(xs[i], xs[i | j])
        j //= 2
    return xs


def _sort_desc(xs):
    xs = list(xs)
    n = len(xs)
    k = 2
    while k <= n:
        j = k // 2
        while j >= 1:
            for i in range(n):
                if i & j == 0:
                    hi, lo = jnp.maximum(xs[i], xs[i | j]), jnp.minimum(xs[i], xs[i | j])
                    xs[i], xs[i | j] = (hi, lo) if (i & k) == 0 else (lo, hi)
            j //= 2
        k *= 2
    return xs


def _peer_select_body(st_ref, e1_ref, cnt1_ref, rank2_ref, e2_ref, rank1_s, vtop_s, cnt_s, zinv_s):
    tm = st_ref.shape[-1]
    kio = lax.broadcasted_iota(jnp.int32, (N_KEYS, LANES), 0).astype(F32)
    neg = jnp.float32(-jnp.inf)

    def group(g, carry):
        ln = pl.ds(pl.multiple_of(g * LANES, LANES), LANES)

        def store_rank(hp, rank):
            if hp % 2 == 0:
                rank1_s[hp // 2, :, ln] = rank
            else:
                rank2_ref[hp // 2, :, ln] = rank

        ambiguous = jnp.zeros((SUBLANES, LANES), F32)
        for hp in range(PEER_HP):
            h, part = hp // 2, hp % 2
            tiles = [st_ref[hp, v * SUBLANES:(v + 1) * SUBLANES, ln] for v in range(N_KEYS // SUBLANES)]
            top = _sort_desc(list(tiles))
            for shift in (SUBLANES // 2, SUBLANES // 4, SUBLANES // 8):
                merged = [jnp.maximum(top[v], pltpu.roll(top[PEER_TOPK - 1 - v], shift, axis=0))
                          for v in range(PEER_TOPK)]
                top = _bitonic_merge_desc(merged)
            gap = top[0] - top[1]
            for a in range(1, PEER_TOPK - 1):
                gap = jnp.minimum(gap, top[a] - top[a + 1])
            n_sel = jnp.zeros((SUBLANES, LANES), jnp.int32)
            ranks = []
            for t in tiles:
                below = jnp.zeros((SUBLANES, LANES), jnp.int32)
                for a in range(PEER_TOPK):
                    below = below + lax.shift_right_arithmetic(pltpu.bitcast(t - top[a], jnp.int32), 31)
                ranks.append((-below).astype(F32))
                n_sel = n_sel - lax.shift_right_arithmetic(below + (PEER_TOPK - 1), 31) - 1
            n_sel = -n_sel
            for shift in (SUBLANES // 2, SUBLANES // 4, SUBLANES // 8):
                n_sel = n_sel + pltpu.roll(n_sel, shift, axis=0)
            ambiguous = ambiguous + jnp.where(gap == 0.0, 1.0, 0.0) + jnp.abs((n_sel - PEER_TOPK).astype(F32))
            for a in range(PEER_TOPK):
                vtop_s[part, a, h:h + 1, ln] = top[a][0:1, :]
            store_rank(hp, jnp.concatenate(ranks, axis=0))

        @pl.when(jnp.max(ambiguous) > 0.0)
        def _():
            for hp in range(PEER_HP):
                h, part = hp // 2, hp % 2

                def extract(it, sr):
                    s, rank = sr
                    m = jnp.max(s, axis=0, keepdims=True)
                    first = jnp.min(jnp.where(s == m, kio, float(N_KEYS)), axis=0, keepdims=True)
                    sel = kio == first
                    vtop_s[part, it, h:h + 1, ln] = m
                    return jnp.where(sel, neg, s), jnp.where(sel, it, rank)
                _, rank = lax.fori_loop(0, PEER_TOPK, extract,
                                        (st_ref[hp, :, ln], jnp.full((N_KEYS, LANES), PEER_TOPK, jnp.int32)))
                store_rank(hp, rank.astype(F32))

        v1 = [vtop_s[0, a, :, ln] for a in range(PEER_TOPK)]
        v2 = [vtop_s[1, b, :, ln] for b in range(PEER_TOPK)]
        sums = [v1[a] + v2[b] for a, b in PEER_CAND]
        n_c = len(PEER_CAND)
        beaten = [jnp.zeros((PEER_HEADS, LANES), F32) for _ in range(n_c)]
        for i in range(n_c):
            for j in range(i + 1, n_c):
                ge = sums[i] >= sums[j]
                beaten[j] = beaten[j] + jnp.where(ge, 1.0, 0.0)
                beaten[i] = beaten[i] + jnp.where(ge, 0.0, 1.0)
        z = jnp.zeros((PEER_HEADS, LANES), F32)
        cnt = [jnp.zeros((PEER_HEADS, LANES), F32) for _ in range(PEER_TOPK)]
        for i, (a, b) in enumerate(PEER_CAND):
            keep = beaten[i] < float(PEER_TOPK)
            z = z + jnp.where(keep, jnp.exp(sums[i] - sums[0]), 0.0)
            cnt[a] = cnt[a] + jnp.where(keep, 1.0, 0.0)
        for a in range(PEER_TOPK):
            cnt_s[a, :, ln] = cnt[a]
        zinv_s[:, ln] = 1.0 / z

        for h in range(PEER_HEADS):
            rank1 = rank1_s[h, :, ln]
            s1 = st_ref[2 * h, :, ln]
            e1 = jnp.exp(s1 - vtop_s[0, 0, h:h + 1, ln]) * zinv_s[h:h + 1, ln]
            e1_ref[h, :, ln] = jnp.where(rank1 < float(PEER_TOPK), e1, 0.0)
            c1 = jnp.zeros((N_KEYS, LANES), F32)
            for a in range(PEER_TOPK):
                c1 = c1 + jnp.where(rank1 == float(a), cnt_s[a, h:h + 1, ln], 0.0)
            cnt1_ref[h, :, ln] = c1
            e2_ref[h, :, ln] = jnp.exp(st_ref[2 * h + 1, :, ln] - vtop_s[1, 0, h:h + 1, ln])
        return carry
    lax.fori_loop(0, tm // LANES, group, 0)


def peer_select(st):
    _, _, n = st.shape
    tm = PEER_SELECT_TM
    spec = pl.BlockSpec((PEER_HEADS, N_KEYS, tm), lambda i: (0, 0, i))
    shp = jax.ShapeDtypeStruct((PEER_HEADS, N_KEYS, n), F32)
    return pl.pallas_call(
        _peer_select_body,
        grid=(n // tm,),
        in_specs=[pl.BlockSpec((PEER_HP, N_KEYS, tm), lambda i: (0, 0, i))],
        out_specs=[spec, spec, spec, spec],
        out_shape=[shp, shp, shp, shp],
        scratch_shapes=[pltpu.VMEM((PEER_HEADS, N_KEYS, tm), F32),
                        pltpu.VMEM((2, PEER_TOPK, PEER_HEADS, tm), F32),
                        pltpu.VMEM((PEER_TOPK, PEER_HEADS, tm), F32),
                        pltpu.VMEM((PEER_HEADS, tm), F32)],
        compiler_params=pltpu.CompilerParams(dimension_semantics=("parallel",), vmem_limit_bytes=VMEM_LIMIT),
        name="peer_select",
    )(st)


def _gelu_tanh(x):
    return 0.5 * x * (1.0 + jnp.tanh(math.sqrt(2.0 / math.pi) * (x + 0.044715 * (x * x * x))))


PEER_SUB_ROWS = 4
PEER_GATE_ROWS = 32


def _peer_expert_body(u_ref, vt_ref, xt_ref, e1_ref, cnt1_ref, rank2_ref, e2_ref, o_ref, st_s, at_s):
    j = pl.program_id(1)
    tm = xt_ref.shape[-1]

    @pl.when(j == 0)
    def _():
        o_ref[...] = jnp.zeros_like(o_ref)

    for c in range(PEER_ROWS // PEER_SUB_ROWS):
        sub = slice(c * PEER_SUB_ROWS * N_KEYS, (c + 1) * PEER_SUB_ROWS * N_KEYS)
        st_s[sub, :] = jnp.dot(u_ref[sub, :], xt_ref[...], preferred_element_type=F32)
        sub_rows = range(c * PEER_SUB_ROWS, (c + 1) * PEER_SUB_ROWS)
        n_blk = N_KEYS // PEER_GATE_ROWS

        def gate_block(idx, carry):
            ln = pl.ds(pl.multiple_of((idx // n_blk) * LANES, LANES), LANES)
            col0 = pl.multiple_of((idx % n_blk) * PEER_GATE_ROWS, PEER_GATE_ROWS)
            cols = pl.ds(col0, PEER_GATE_ROWS)
            gate = {r: jnp.zeros((PEER_GATE_ROWS, LANES), F32) for r in sub_rows}
            for h in range(PEER_HEADS):
                rank2 = rank2_ref[h, cols, ln]
                e2_bits = pltpu.bitcast(e2_ref[h, cols, ln], jnp.int32)
                for r in sub_rows:
                    keep = lax.shift_right_arithmetic(
                        pltpu.bitcast(rank2 - cnt1_ref[h, r:r + 1, ln], jnp.int32), 31)
                    gate[r] = gate[r] + pltpu.bitcast(e2_bits & keep, F32) * e1_ref[h, r:r + 1, ln]
            for r in sub_rows:
                rows = pl.ds(r * N_KEYS + col0, PEER_GATE_ROWS)
                at_s[rows, ln] = (_gelu_tanh(st_s[rows, ln]) * gate[r]).astype(BF16)
            return carry
        lax.fori_loop(0, (tm // LANES) * n_blk, gate_block, 0)
        o_ref[...] += jnp.dot(vt_ref[:, sub], at_s[sub, :], preferred_element_type=F32)


def peer_experts(ub, vt, xt, e1, cnt1, rank2, e2):
    n_e, d = ub.shape
    n = xt.shape[1]
    tm, te = PEER_TM, PEER_TE
    row_spec = pl.BlockSpec((PEER_HEADS, PEER_ROWS, tm), lambda i, j: (0, j, i))
    col_spec = pl.BlockSpec((PEER_HEADS, N_KEYS, tm), lambda i, j: (0, 0, i))
    return pl.pallas_call(
        _peer_expert_body,
        grid=(n // tm, n_e // te),
        in_specs=[pl.BlockSpec((te, d), lambda i, j: (j, 0)),
                  pl.BlockSpec((d, te), lambda i, j: (0, j)),
                  pl.BlockSpec((d, tm), lambda i, j: (0, i)),
                  row_spec, row_spec, col_spec, col_spec],
        out_specs=pl.BlockSpec((d, tm), lambda i, j: (0, i)),
        out_shape=jax.ShapeDtypeStruct((d, n), F32),
        scratch_shapes=[pltpu.VMEM((te, tm), F32), pltpu.VMEM((te, tm), BF16)],
        compiler_params=pltpu.CompilerParams(dimension_semantics=("parallel", "arbitrary"),
                                             vmem_limit_bytes=VMEM_LIMIT),
        name="peer_experts",
    )(ub, vt, xt, e1, cnt1, rank2, e2)


def peer_ffn(h_a, h_b, wq_t, sub_keys_b, ub, vt):
    xt, st = peer_scores(h_a, h_b, wq_t, sub_keys_b)
    e1, cnt1, rank2, e2 = peer_select(st)
    return peer_experts(ub, vt, xt, e1, cnt1, rank2, e2)


PROJ_ORDER = (0, 1, 2, 3, 4, 5, 8, 9, 14, 15, 16, 6, 10, 11, 7, 12, 13)
PROJ_W = 6528
COL_HQ, COL_HFF, COL_HFB, COL_HI, COL_HG, COL_CQ, COL_SX, COL_SZ, COL_NQ, COL_NK, COL_NV = range(11)
COL_CKV, COL_SB, COL_SC = 22, 23, 24
COL_SMALL = 50
SMALL_DT = ROPE
PROJ_TM = 512
PROJ_TN = 2176
MOD_SH1, MOD_SC1, MOD_G1, MOD_SH2, MOD_SC2, MOD_G2 = range(6)


def in_proj_weight(w_in):
    offs = np.cumsum((0,) + IN_SPLITS)
    cols = [w_in[:, offs[k]:offs[k + 1]] for k in PROJ_ORDER]
    cols.append(jnp.zeros((w_in.shape[0], PROJ_W - IN_W), w_in.dtype))
    return jnp.concatenate(cols, axis=1).astype(BF16)


def _adaln_body(c_ref, w_ref, b_ref, o_ref):
    a = jax.nn.silu(c_ref[...]).astype(BF16)
    o_ref[...] = jnp.dot(a, w_ref[...].astype(BF16), preferred_element_type=F32) + b_ref[...]


def adaln(cond, w_ada, b_ada):
    r, d = cond.shape
    n = w_ada.shape[1]
    tn = 1024
    return pl.pallas_call(
        _adaln_body,
        grid=(n // tn,),
        in_specs=[pl.BlockSpec((r, d), lambda j: (0, 0)), pl.BlockSpec((d, tn), lambda j: (0, j)),
                  pl.BlockSpec((1, tn), lambda j: (0, j))],
        out_specs=pl.BlockSpec((r, tn), lambda j: (0, j)),
        out_shape=jax.ShapeDtypeStruct((r, n), F32),
        compiler_params=pltpu.CompilerParams(dimension_semantics=("parallel",), vmem_limit_bytes=VMEM_LIMIT),
        name="adaln",
    )(cond, w_ada, b_ada.reshape(1, n))


def _rms_rows(x, w):
    return x * lax.rsqrt(jnp.mean(x * x, axis=-1, keepdims=True) + EPS) * w


def _in_proj_body(x_ref, nw_ref, sc_ref, sh_ref, w_ref, o_ref, h_s):
    @pl.when(pl.program_id(1) == 0)
    def _():
        h_s[...] = (_rms_rows(x_ref[...], nw_ref[...]) * (1.0 + sc_ref[...]) + sh_ref[...]).astype(BF16)
    o_ref[...] = jnp.dot(h_s[...], w_ref[...], preferred_element_type=F32)


def in_proj(x, norm_w, mod, w_perm, rows_per_mod):
    n, d = x.shape
    tm, tn = PROJ_TM, PROJ_TN

    def mod_spec(part):
        return pl.BlockSpec((None, 1, d), lambda i, j: ((i * tm) // rows_per_mod, 0, part))
    return pl.pallas_call(
        _in_proj_body,
        grid=(n // tm, PROJ_W // tn),
        in_specs=[pl.BlockSpec((tm, d), lambda i, j: (i, 0)), pl.BlockSpec((1, d), lambda i, j: (0, 0)),
                  mod_spec(MOD_SC1), mod_spec(MOD_SH1), pl.BlockSpec((d, tn), lambda i, j: (0, j))],
        out_specs=pl.BlockSpec((tm, tn), lambda i, j: (i, j)),
        out_shape=jax.ShapeDtypeStruct((n, PROJ_W), F32),
        scratch_shapes=[pltpu.VMEM((tm, d), BF16)],
        compiler_params=pltpu.CompilerParams(dimension_semantics=("parallel", "arbitrary"),
                                             vmem_limit_bytes=VMEM_LIMIT),
        name="in_proj",
    )(x, norm_w.reshape(1, d), mod, mod, w_perm)


def _out_proj_body(oa_ref, ob_ref, oc_ref, od_ref, w_ref, x_ref, g_ref, sc_ref, sh_ref, nw_ref, xo_ref, h_ref):
    mix = None
    for k, o_ref in enumerate((oa_ref, ob_ref, oc_ref, od_ref)):
        t = jnp.dot(o_ref[...].astype(BF16), w_ref[k * GROUP_W:(k + 1) * GROUP_W, :], preferred_element_type=F32)
        mix = t if mix is None else mix + t
    x = x_ref[...] + g_ref[...] * mix
    xo_ref[...] = x
    h_ref[...] = _rms_rows(x, nw_ref[...]) * (1.0 + sc_ref[...]) + sh_ref[...]


def out_proj(o_parts, w_out_b, x, mod, norm2_w, rows_per_mod):
    n, d = x.shape
    tm = 256

    def mod_spec(part):
        return pl.BlockSpec((None, 1, d), lambda i: ((i * tm) // rows_per_mod, 0, part))
    part = pl.BlockSpec((tm, GROUP_W), lambda i: (i, 0))
    row = pl.BlockSpec((tm, d), lambda i: (i, 0))
    return pl.pallas_call(
        _out_proj_body,
        grid=(n // tm,),
        in_specs=[part, part, part, part, pl.BlockSpec((D_MIX, d), lambda i: (0, 0)), row,
                  mod_spec(MOD_G1), mod_spec(MOD_SC2), mod_spec(MOD_SH2), pl.BlockSpec((1, d), lambda i: (0, 0))],
        out_specs=[row, row],
        out_shape=[jax.ShapeDtypeStruct((n, d), F32), jax.ShapeDtypeStruct((n, d), F32)],
        compiler_params=pltpu.CompilerParams(dimension_semantics=("parallel",), vmem_limit_bytes=VMEM_LIMIT),
        name="out_proj",
    )(*o_parts, w_out_b, x, mod, mod, mod, norm2_w.reshape(1, d))


def _norm_matmul_body(x_ref, nw_ref, w_ref, y_ref, xn_ref):
    xn = _rms_rows(x_ref[...], nw_ref[...])
    xn_ref[...] = xn
    y_ref[...] = jnp.dot(xn.astype(BF16), w_ref[...], preferred_element_type=F32)


def _matmul_body(x_ref, w_ref, y_ref):
    y_ref[...] = jnp.dot(x_ref[...].astype(BF16), w_ref[...], preferred_element_type=F32)


def norm_matmul(x, col, k, norm_w, w_b):
    n = x.shape[0]
    n_out = w_b.shape[1]
    tm = min(512, n)
    x_spec = pl.BlockSpec((tm, k), lambda i: (i, col))
    w_spec = pl.BlockSpec((k, n_out), lambda i: (0, 0))
    y_spec = pl.BlockSpec((tm, n_out), lambda i: (i, 0))
    params = pltpu.CompilerParams(dimension_semantics=("parallel",), vmem_limit_bytes=VMEM_LIMIT)
    if norm_w is None:
        return pl.pallas_call(_matmul_body, grid=(n // tm,), in_specs=[x_spec, w_spec], out_specs=y_spec,
                              out_shape=jax.ShapeDtypeStruct((n, n_out), F32), compiler_params=params,
                              name="matmul")(x, w_b)
    return pl.pallas_call(
        _norm_matmul_body, grid=(n // tm,),
        in_specs=[x_spec, pl.BlockSpec((1, k), lambda i: (0, 0)), w_spec],
        out_specs=[y_spec, pl.BlockSpec((tm, k), lambda i: (i, 0))],
        out_shape=[jax.ShapeDtypeStruct((n, n_out), F32), jax.ShapeDtypeStruct((n, k), F32)],
        compiler_params=params, name="norm_matmul")(x, norm_w.reshape(1, k), w_b)


ATTN_TQ = Q_BLOCK


def _attn_body(*refs, n_heads, parts, dv, scale, n_seg):
    n_p = len(parts)
    q_refs = refs[:n_p]
    pos = n_p
    segs = []
    for _ in range(n_seg):
        segs.append((refs[pos:pos + n_p], refs[pos + n_p]))
        pos += n_p + 1
    o_ref = refs[pos]
    outs = []
    for h in range(n_heads):
        scores = []
        for k_refs, _ in segs:
            s = None
            for q_ref, k_ref, (d, shared) in zip(q_refs, k_refs, parts):
                qh = q_ref[:, h * d:(h + 1) * d].astype(BF16)
                kh = (k_ref[:, 0:d] if shared else k_ref[:, h * d:(h + 1) * d]).astype(BF16)
                t = _dot_nt(qh, kh)
                s = t if s is None else s + t
            scores.append(s * scale)
        m = None
        for s in scores:
            ms = jnp.max(s, axis=-1, keepdims=True)
            m = ms if m is None else jnp.maximum(m, ms)
        probs = [jnp.exp(s - m) for s in scores]
        denom = None
        for p in probs:
            ps = jnp.sum(p, axis=-1, keepdims=True)
            denom = ps if denom is None else denom + ps
        inv = 1.0 / denom
        o = None
        for p, (_, v_ref) in zip(probs, segs):
            t = jnp.dot((p * inv).astype(BF16), v_ref[:, h * dv:(h + 1) * dv].astype(BF16),
                        preferred_element_type=F32)
            o = t if o is None else o + t
        outs.append(o)
    o_ref[...] = jnp.concatenate(outs, axis=-1)


def attention(q_parts, segments, n_heads, parts, dv, scale):
    b, tq, _ = q_parts[0][0].shape
    args, specs = [], []
    for arr, w, c in q_parts:
        args.append(arr)
        specs.append(pl.BlockSpec((None, ATTN_TQ, w), lambda bi, i, c=c: (bi, i, c)))
    for k_parts, v in segments:
        for arr, w, c in list(k_parts) + [v]:
            args.append(arr)
            specs.append(pl.BlockSpec((None, arr.shape[1], w), lambda bi, i, c=c: (bi, 0, c)))
    body = functools.partial(_attn_body, n_heads=n_heads, parts=parts, dv=dv, scale=scale, n_seg=len(segments))
    return pl.pallas_call(
        body,
        grid=(b, tq // ATTN_TQ),
        in_specs=specs,
        out_specs=pl.BlockSpec((None, ATTN_TQ, n_heads * dv), lambda bi, i: (bi, i, 0)),
        out_shape=jax.ShapeDtypeStruct((b, tq, n_heads * dv), F32),
        compiler_params=pltpu.CompilerParams(dimension_semantics=("parallel", "arbitrary"),
                                             vmem_limit_bytes=VMEM_LIMIT),
        name="attention",
    )(*args)


def mla_weights(w_q_up, w_kv_up):
    wq = w_q_up.reshape(Q_RANK, HB, NOPE + ROPE)
    wq = jnp.concatenate([wq[:, :, :NOPE].reshape(Q_RANK, HB * NOPE), wq[:, :, NOPE:].reshape(Q_RANK, HB * ROPE)], 1)
    wkv = w_kv_up.reshape(KV_RANK, HB, NOPE + VB)
    wkv = jnp.concatenate([wkv[:, :, :NOPE].reshape(KV_RANK, HB * NOPE), wkv[:, :, NOPE:].reshape(KV_RANK, HB * VB)], 1)
    return wq.astype(BF16), wkv.astype(BF16)


def rope_tables(t_len):
    t = jnp.arange(t_len)
    half = ROPE // 2
    inv = 1.0 / (ROPE_THETA ** (jnp.arange(0, half, 2, dtype=F32) / half))
    a_r = (t // GRID_W).astype(F32)[:, None] * inv[None, :]
    a_c = (t % GRID_W).astype(F32)[:, None] * inv[None, :]
    cos = jnp.concatenate([jnp.cos(a_r), jnp.cos(a_r), jnp.cos(a_c), jnp.cos(a_c)], axis=1)
    sin = jnp.concatenate([-jnp.sin(a_r), jnp.sin(a_r), -jnp.sin(a_c), jnp.sin(a_c)], axis=1)
    return cos, sin


def _rope_body(q_ref, small_ref, cos_ref, sin_ref, qo_ref, ko_ref):
    quarter = ROPE // 4
    r_io = lax.broadcasted_iota(jnp.int32, (ROPE, ROPE), 0)
    c_io = lax.broadcasted_iota(jnp.int32, (ROPE, ROPE), 1)
    partner = jnp.where((c_io // quarter) % 2 == 0, c_io + quarter, c_io - quarter)
    swap = jnp.where(r_io == partner, 1.0, 0.0).astype(BF16)
    cos, sin = cos_ref[...], sin_ref[...]

    def rot(x):
        xs = sum(jnp.dot(part, swap, preferred_element_type=F32) for part in _split3_bf16(x))
        return x * cos + xs * sin
    qo_ref[...] = jnp.concatenate([rot(q_ref[:, h * ROPE:(h + 1) * ROPE]) for h in range(HB)], axis=-1)
    ko_ref[...] = rot(small_ref[:, 0:ROPE])


def rope_rotate(q, proj3):
    b, t, _ = q.shape
    cos, sin = rope_tables(t)
    tb = 256
    tab = pl.BlockSpec((tb, ROPE), lambda bi, i: (i, 0))
    return pl.pallas_call(
        _rope_body,
        grid=(b, t // tb),
        in_specs=[pl.BlockSpec((None, tb, HB * ROPE), lambda bi, i: (bi, i, HB * NOPE // (HB * ROPE))),
                  pl.BlockSpec((None, tb, LANES), lambda bi, i: (bi, i, COL_SMALL)), tab, tab],
        out_specs=[pl.BlockSpec((None, tb, HB * ROPE), lambda bi, i: (bi, i, 0)),
                   pl.BlockSpec((None, tb, ROPE), lambda bi, i: (bi, i, 0))],
        out_shape=[jax.ShapeDtypeStruct((b, t, HB * ROPE), F32), jax.ShapeDtypeStruct((b, t, ROPE), F32)],
        compiler_params=pltpu.CompilerParams(dimension_semantics=("parallel", "parallel")),
        name="rope_rotate",
    )(q, proj3, cos, sin)


MLA_PARTS = ((NOPE, False), (ROPE, True))


def mla_mixer(proj2, b, t, p, cache):
    wq_b, wkv_b = p['mla_wq_b'], p['mla_wkv_b']
    q, _ = norm_matmul(proj2, COL_CQ, Q_RANK, p['mla_q_norm_w'], wq_b)
    kv, ckv_n = norm_matmul(proj2, COL_CKV, KV_RANK, p['mla_kv_norm_w'], wkv_b)
    q3, kv3, proj3 = q.reshape(b, t, -1), kv.reshape(b, t, -1), proj2.reshape(b, t, PROJ_W)
    if cache is None:
        o = attention([(q3, HB * NOPE, 0), (q3, HB * ROPE, 2)],
                      [([(kv3, HB * NOPE, 0), (proj3, LANES, COL_SMALL)], (kv3, HB * VB, 1))],
                      HB, MLA_PARTS, VB, MLA_SCALE)
        return o, ckv_n.reshape(b, t, KV_RANK), proj3[:, :, COL_SMALL * LANES:COL_SMALL * LANES + ROPE]
    ckv_ctx, krope_ctx = cache
    n_ctx = ckv_ctx.shape[1]
    kv_ctx = norm_matmul(ckv_ctx.reshape(b * n_ctx, KV_RANK).astype(F32), 0, KV_RANK, None, wkv_b)
    kv_ctx = kv_ctx.reshape(b, n_ctx, -1)
    q_rot, k_rot = rope_rotate(q3, proj3)
    o = attention([(q3, HB * NOPE, 0), (q_rot, HB * ROPE, 0)],
                  [([(kv3, HB * NOPE, 0), (k_rot, ROPE, 0)], (kv3, HB * VB, 1)),
                   ([(kv_ctx, HB * NOPE, 0), (krope_ctx.astype(F32), ROPE, 0)], (kv_ctx, HB * VB, 1))],
                  HB, MLA_PARTS, VB, MLA_SCALE)
    return o, None, None


def _ssd_conv_body(x_ref, b_ref, c_ref, w_ref, bias_ref, o_ref):
    t = x_ref.shape[0]
    xbc = jnp.concatenate([x_ref[...], b_ref[...], c_ref[...]], axis=-1)
    row = lax.broadcasted_iota(jnp.int32, (t, 1), 0)
    prev = jnp.where(row == 0, 0.0, pltpu.roll(xbc, 1, axis=0))
    nxt = jnp.where(row == t - 1, 0.0, pltpu.roll(xbc, t - 1, axis=0))
    o_ref[...] = jax.nn.silu(prev * w_ref[0:1] + xbc * w_ref[1:2] + nxt * w_ref[2:3] + bias_ref[...])


def ssd_conv(proj3, conv_w, conv_b):
    b, t, _ = proj3.shape
    return pl.pallas_call(
        _ssd_conv_body,
        grid=(b,),
        in_specs=[pl.BlockSpec((None, t, GROUP_W), lambda bi: (bi, 0, COL_SX)),
                  pl.BlockSpec((None, t, GC * NC), lambda bi: (bi, 0, COL_SB)),
                  pl.BlockSpec((None, t, GC * NC), lambda bi: (bi, 0, COL_SC)),
                  pl.BlockSpec((CONV_W, CONV_CH), lambda bi: (0, 0)), pl.BlockSpec((1, CONV_CH), lambda bi: (0, 0))],
        out_specs=pl.BlockSpec((None, t, CONV_CH), lambda bi: (bi, 0, 0)),
        out_shape=jax.ShapeDtypeStruct((b, t, CONV_CH), F32),
        compiler_params=pltpu.CompilerParams(dimension_semantics=("parallel",), vmem_limit_bytes=VMEM_LIMIT),
        name="ssd_conv",
    )(proj3, proj3, proj3, conv_w.astype(F32), conv_b.astype(F32).reshape(1, CONV_CH))


SSD_PAIRS = HC // 2


def _ssd_direction(rev, xbc_ref, dtc_ref, dtr_ref, bias_c, a_c, bias_r, a_r, st_s, d, y_ref):
    xbc = xbc_ref[...]
    dt_c = jax.nn.softplus(dtc_ref[...] + bias_c)
    la_c = dt_c * a_c
    la_r = jax.nn.softplus(dtr_ref[...] + bias_r) * a_r
    t_io = lax.broadcasted_iota(jnp.int32, (CHUNK, CHUNK), 0)
    s_io = lax.broadcasted_iota(jnp.int32, (CHUNK, CHUNK), 1)
    causal = (s_io >= t_io) if rev else (s_io <= t_io)
    tri = jnp.where(causal, 1.0, 0.0).astype(BF16)
    tri_t = jnp.where((t_io >= s_io) if rev else (t_io <= s_io), 1.0, 0.0).astype(BF16)
    cum_c = sum(jnp.dot(tri, part, preferred_element_type=F32) for part in _split3_bf16(la_c))
    cum_r = sum(jnp.dot(part, tri_t, preferred_element_type=F32) for part in _split3_bf16(la_r))
    end = 0 if rev else CHUNK - 1
    lane_lo = lax.broadcasted_iota(jnp.int32, (CHUNK, LANES), 1) < PC
    row_lo = lax.broadcasted_iota(jnp.int32, (LANES, NC), 0) < PC
    cbs = []
    for g in range(GC):
        bg = xbc[:, GROUP_W + g * NC:GROUP_W + (g + 1) * NC].astype(BF16)
        cg = xbc[:, GROUP_W + GC * NC + g * NC:GROUP_W + GC * NC + (g + 1) * NC].astype(BF16)
        cbs.append((bg, cg, _dot_nt(cg, bg)))
    outs = []
    for pr in range(SSD_PAIRS):
        ha, hb = 2 * pr, 2 * pr + 1
        bg, cg, cb = cbs[ha // (HC // GC)]
        xdt = xbc[:, pr * LANES:(pr + 1) * LANES] * jnp.where(lane_lo, dt_c[:, ha:ha + 1], dt_c[:, hb:hb + 1])
        ys = []
        for hx in (ha, hb):
            decay = jnp.exp(jnp.where(causal, cum_c[:, hx:hx + 1] - cum_r[hx:hx + 1, :], 0.0))
            m = (cb * jnp.where(causal, decay, 0.0)).astype(BF16)
            ys.append(jnp.dot(m, xdt.astype(BF16), preferred_element_type=F32))
        y = jnp.where(lane_lo, ys[0], ys[1])
        st = st_s[d, pr]
        e_cum = jnp.where(lane_lo, jnp.exp(cum_c[:, ha:ha + 1]), jnp.exp(cum_c[:, hb:hb + 1]))
        y = y + _dot_nt(cg, st.astype(BF16)) * e_cum
        outs.append(y)
        last_a, last_b = cum_c[end:end + 1, ha:ha + 1], cum_c[end:end + 1, hb:hb + 1]
        w = jnp.where(lane_lo, jnp.exp(last_a - cum_c[:, ha:ha + 1]), jnp.exp(last_b - cum_c[:, hb:hb + 1]))
        e_last = jnp.where(row_lo, jnp.exp(last_a), jnp.exp(last_b))
        st_s[d, pr] = e_last * st + _dot_tn((xdt * w).astype(BF16), bg)
    y_ref[...] = jnp.concatenate(outs, axis=-1)


def _ssd_body(xf_ref, xb_ref, dcf_ref, dcb_ref, drf_ref, drb_ref, bc_ref, ac_ref, br_ref, ar_ref, s0_ref,
              yf_ref, yb_ref, sfin_ref, st_s):
    i = pl.program_id(1)

    @pl.when(i == 0)
    def _():
        st_s[...] = s0_ref[...]

    _ssd_direction(False, xf_ref, dcf_ref, drf_ref, bc_ref[0], ac_ref[0], br_ref[0], ar_ref[0], st_s, 0, yf_ref)
    _ssd_direction(True, xb_ref, dcb_ref, drb_ref, bc_ref[1], ac_ref[1], br_ref[1], ar_ref[1], st_s, 1, yb_ref)

    @pl.when(i == pl.num_programs(1) - 1)
    def _():
        sfin_ref[...] = st_s[...]


def ssd_scan(xbc, dt_f, dt_b, dt_bias, a_log, s0):
    b, t, _ = xbc.shape
    n = t // CHUNK
    a = -jnp.exp(a_log.astype(F32))
    bias = dt_bias.astype(F32)
    dcf, dcb = dt_f.reshape(b, n, CHUNK, HC), dt_b.reshape(b, n, CHUNK, HC)
    drf, drb = dcf.transpose(0, 1, 3, 2), dcb.transpose(0, 1, 3, 2)
    x_f = pl.BlockSpec((None, CHUNK, CONV_CH), lambda bi, i: (bi, i, 0))
    x_b = pl.BlockSpec((None, CHUNK, CONV_CH), lambda bi, i: (bi, n - 1 - i, 0))
    y_f = pl.BlockSpec((None, CHUNK, GROUP_W), lambda bi, i: (bi, i, 0))
    y_b = pl.BlockSpec((None, CHUNK, GROUP_W), lambda bi, i: (bi, n - 1 - i, 0))
    c_f = pl.BlockSpec((None, None, CHUNK, HC), lambda bi, i: (bi, i, 0, 0))
    c_b = pl.BlockSpec((None, None, CHUNK, HC), lambda bi, i: (bi, n - 1 - i, 0, 0))
    r_f = pl.BlockSpec((None, None, HC, CHUNK), lambda bi, i: (bi, i, 0, 0))
    r_b = pl.BlockSpec((None, None, HC, CHUNK), lambda bi, i: (bi, n - 1 - i, 0, 0))
    p_c = pl.BlockSpec((2, 1, HC), lambda bi, i: (0, 0, 0))
    p_r = pl.BlockSpec((2, HC, 1), lambda bi, i: (0, 0, 0))
    st = pl.BlockSpec((None, 2, SSD_PAIRS, LANES, NC), lambda bi, i: (bi, 0, 0, 0, 0))
    y_fwd, y_bwd, s_fin = pl.pallas_call(
        _ssd_body,
        grid=(b, n),
        in_specs=[x_f, x_b, c_f, c_b, r_f, r_b, p_c, p_c, p_r, p_r, st],
        out_specs=[y_f, y_b, st],
        out_shape=[jax.ShapeDtypeStruct((b, t, GROUP_W), F32), jax.ShapeDtypeStruct((b, t, GROUP_W), F32),
                   jax.ShapeDtypeStruct((b, 2, SSD_PAIRS, LANES, NC), F32)],
        scratch_shapes=[pltpu.VMEM((2, SSD_PAIRS, LANES, NC), F32)],
        compiler_params=pltpu.CompilerParams(dimension_semantics=("parallel", "arbitrary"),
                                             vmem_limit_bytes=VMEM_LIMIT),
        name="ssd_scan",
    )(xbc, xbc, dcf, dcb, drf, drb, bias.reshape(2, 1, HC), a.reshape(2, 1, HC), bias.reshape(2, HC, 1),
      a.reshape(2, HC, 1), s0.astype(F32).reshape(b, 2, SSD_PAIRS, LANES, NC))
    return y_fwd, y_bwd, s_fin.reshape(b, 2, HC, PC, NC)


def _ssd_combine_body(x_ref, yf_ref, yb_ref, z_ref, d_ref, w_ref, o_ref):
    y = (d_ref[...] * x_ref[...] + yf_ref[...] + yb_ref[...]) * jax.nn.silu(z_ref[...])
    o_ref[...] = _rms_rows(y, w_ref[...])


def ssd_combine(xbc2, y_f, y_b, proj2, d_skip, norm_w):
    n = xbc2.shape[0]
    rows = 256
    blk = pl.BlockSpec((rows, GROUP_W), lambda i: (i, 0))
    par = pl.BlockSpec((1, GROUP_W), lambda i: (0, 0))
    return pl.pallas_call(
        _ssd_combine_body,
        grid=(n // rows,),
        in_specs=[blk, blk, blk, pl.BlockSpec((rows, GROUP_W), lambda i: (i, COL_SZ)), par, par],
        out_specs=blk,
        out_shape=jax.ShapeDtypeStruct((n, GROUP_W), F32),
        compiler_params=pltpu.CompilerParams(dimension_semantics=("parallel",)),
        name="ssd_combine",
    )(xbc2, y_f.reshape(n, GROUP_W), y_b.reshape(n, GROUP_W), proj2,
      jnp.repeat(d_skip.astype(F32), PC).reshape(1, GROUP_W), norm_w.astype(F32).reshape(1, GROUP_W))


def ssd_mixer(proj2, b, t, p, init_state):
    proj3 = proj2.reshape(b, t, PROJ_W)
    xbc = ssd_conv(proj3, p['ssd_conv_w'], p['ssd_conv_b'])
    dt0 = COL_SMALL * LANES + SMALL_DT
    dt_f, dt_b = proj3[:, :, dt0:dt0 + HC], proj3[:, :, dt0 + HC:dt0 + 2 * HC]
    s0 = jnp.zeros((b, 2, HC, PC, NC), F32) if init_state is None else init_state
    y_f, y_b, s_fin = ssd_scan(xbc, dt_f, dt_b, p['ssd_dt_bias'], p['ssd_a_log'], s0)
    return ssd_combine(xbc.reshape(b * t, CONV_CH), y_f, y_b, proj2, p['ssd_d'], p['ssd_norm_w']), s_fin


def _peer_finish_body(ot_ref, x_ref, g_ref, nw_ref, o_ref, *, final_norm):
    x = x_ref[...] + g_ref[...] * ot_ref[...].T
    o_ref[...] = _rms_rows(x, nw_ref[...]) if final_norm else x


def peer_finish(out_t, tok0, x, mod, rows_per_mod, final_norm_w=None):
    n, d = x.shape
    tm = 256
    norm_w = jnp.ones((1, d), F32) if final_norm_w is None else final_norm_w.astype(F32).reshape(1, d)
    return pl.pallas_call(
        functools.partial(_peer_finish_body, final_norm=final_norm_w is not None),
        grid=(n // tm,),
        in_specs=[pl.BlockSpec((d, tm), lambda i: (0, tok0 // tm + i)), pl.BlockSpec((tm, d), lambda i: (i, 0)),
                  pl.BlockSpec((None, 1, d), lambda i: ((i * tm) // rows_per_mod, 0, MOD_G2)),
                  pl.BlockSpec((1, d), lambda i: (0, 0))],
        out_specs=pl.BlockSpec((tm, d), lambda i: (i, 0)),
        out_shape=jax.ShapeDtypeStruct((n, d), F32),
        compiler_params=pltpu.CompilerParams(dimension_semantics=("parallel",), vmem_limit_bytes=VMEM_LIMIT),
        name="peer_finish",
    )(out_t, x, mod, norm_w)


def trunk_layer(x2, b, t, mod, p, cache):
    rows_per_mod = (b * t) // mod.shape[0]
    proj2 = in_proj(x2, p['norm1_w'], mod, p['w_in_b'], rows_per_mod)
    proj3 = proj2.reshape(b, t, PROJ_W)
    latent = cache is not None
    s0 = cache[0].astype(F32) if latent else jnp.zeros((b, 2, HA, DKA, DVA), F32)
    o_f, o_b, st_a = hgrn_scan(proj3, p['lb'], s0)
    o_a = hgrn_combine(o_f, o_b, proj2, p['hgrn_norm_w'])
    o_c, st_c = ssd_mixer(proj2, b, t, p, cache[1] if latent else None)
    na_q, na_k, na_v = (proj3, GROUP_W, COL_NQ), (proj3, GROUP_W, COL_NK), (proj3, GROUP_W, COL_NV)
    if latent:
        o_m, _, _ = mla_mixer(proj2, b, t, p, (cache[2], cache[3]))
        o_d = natten_latent(proj3, p['na_rpb'], cache[4], cache[5])
        new_state = None
    else:
        o_m, ckv, krope = mla_mixer(proj2, b, t, p, None)
        o_d = attention([na_q], [([na_k], na_v)], HD, ((DHD, False),), DHD, DHD ** -0.5)
        k_na = proj3[:, :, COL_NK * GROUP_W:(COL_NK + 1) * GROUP_W].reshape(b, t, HD, DHD)
        v_na = proj3[:, :, COL_NV * GROUP_W:(COL_NV + 1) * GROUP_W].reshape(b, t, HD, DHD)
        new_state = (st_a, st_c, ckv, krope, k_na, v_na)
    n = b * t
    x2, h2 = out_proj([o_a, o_m.reshape(n, GROUP_W), o_c, o_d.reshape(n, GROUP_W)], p['w_out_b'], x2, mod,
                      p['norm2_w'], rows_per_mod)
    return x2, h2, new_state


def kernel(x_prompt, x_sample, c, state_hgrn, state_ssd, cache_mla_ckv, cache_mla_krope, cache_na_k,
           cache_na_v, c_ctx, w_ada, b_ada, norm1_w, norm2_w, w_in, w_out, hgrn_lb_logits, hgrn_norm_w,
           mla_q_norm_w, mla_w_q_up, mla_kv_norm_w, mla_w_kv_up, ssd_conv_w, ssd_conv_b, ssd_dt_bias,
           ssd_a_log, ssd_d, ssd_norm_w, na_rpb, peer_w_q, peer_sub_keys, peer_u, peer_v, final_norm_w):
    lb_soft = jax.nn.softmax(hgrn_lb_logits.astype(F32), axis=0)
    lb_all = jnp.cumsum(lb_soft, axis=0) - lb_soft[0]
    stacked = {'w_ada': w_ada, 'b_ada': b_ada, 'norm1_w': norm1_w, 'norm2_w': norm2_w, 'w_in': w_in,
               'w_out': w_out, 'hgrn_norm_w': hgrn_norm_w, 'mla_q_norm_w': mla_q_norm_w,
               'mla_w_q_up': mla_w_q_up, 'mla_kv_norm_w': mla_kv_norm_w, 'mla_w_kv_up': mla_w_kv_up,
               'ssd_conv_w': ssd_conv_w, 'ssd_conv_b': ssd_conv_b, 'ssd_dt_bias': ssd_dt_bias,
               'ssd_a_log': ssd_a_log, 'ssd_d': ssd_d, 'ssd_norm_w': ssd_norm_w, 'na_rpb': na_rpb,
               'peer_w_q': peer_w_q, 'peer_sub_keys': peer_sub_keys, 'peer_u': peer_u, 'peer_v': peer_v}

    ub_all, vt_all = peer_prep_tables(peer_u, peer_v)
    n_ctx, n_lat = BATCH * SEQ, DEC_BATCH * DEC_SEQ
    cond = jnp.concatenate([c_ctx[None, :], c, jnp.zeros((8 - 1 - DEC_BATCH, D_MODEL), F32)], axis=0)

    xp, xs = x_prompt.reshape(n_ctx, D_MODEL), x_sample.reshape(n_lat, D_MODEL)
    ctx_states = []
    for l in range(DEPTH):
        p = {name: arr[l] for name, arr in stacked.items()}
        p['lb'] = lb_all[l]
        p['w_in_b'] = in_proj_weight(w_in[l])
        p['w_out_b'] = w_out[l].astype(BF16)
        p['mla_wq_b'], p['mla_wkv_b'] = mla_weights(mla_w_q_up[l], mla_w_kv_up[l])
        mod = adaln(cond, w_ada[l], b_ada[l])
        mod_p, mod_s = mod[0:1, None, :], mod[1:1 + DEC_BATCH, None, :]
        xp, h2p, st = trunk_layer(xp, BATCH, SEQ, mod_p, p, None)
        ctx_states.append(st)
        cache_l = (state_hgrn[:, l], state_ssd[:, l], cache_mla_ckv[:, l], cache_mla_krope[:, l],
                   cache_na_k[:, l], cache_na_v[:, l])
        xs, h2s, _ = trunk_layer(xs, DEC_BATCH, DEC_SEQ, mod_s, p, cache_l)
        out_t = peer_ffn(h2p, h2s, peer_w_q[l].T.astype(BF16), peer_sub_keys[l].astype(BF16), ub_all[l], vt_all[l])
        last_w = final_norm_w if l == DEPTH - 1 else None
        xp = peer_finish(out_t, 0, xp, mod_p, n_ctx, last_w)
        xs = peer_finish(out_t, n_ctx, xs, mod_s, DEC_SEQ, last_w)

    y_prompt = xp.reshape(BATCH, SEQ, D_MODEL)
    y_sample = xs.reshape(DEC_BATCH, DEC_SEQ, D_MODEL)
    new_state_hgrn = jnp.stack([s[0] for s in ctx_states], axis=1)
    new_state_ssd = jnp.stack([s[1] for s in ctx_states], axis=1)
    new_cache_mla_ckv = jnp.stack([s[2] for s in ctx_states], axis=1)
    new_cache_mla_krope = jnp.stack([s[3] for s in ctx_states], axis=1)
    new_cache_na_k = jnp.stack([s[4] for s in ctx_states], axis=1)
    new_cache_na_v = jnp.stack([s[5] for s in ctx_states], axis=1)
    return (y_prompt, y_sample, new_state_hgrn, new_state_ssd, new_cache_mla_ckv, new_cache_mla_krope,
            new_cache_na_k, new_cache_na_v)
```

```python
import functools
import math
import jax
import jax.numpy as jnp
from jax import lax
import numpy as np
from jax.experimental import pallas as pl
from jax.experimental.pallas import tpu as pltpu

D_MODEL = 2048
BATCH = 32
SEQ = 256
DEPTH = 2
DEC_BATCH = 4
DEC_SEQ = 1024
PAST_LEN = 256

GRID_W = 64
EPS = 1e-6
ROPE_THETA = 10000.0
Q_BLOCK = 128
CHUNK = 64
N_MIXERS = 4
GROUP_W = D_MODEL // N_MIXERS
D_MIX = N_MIXERS * GROUP_W
HA = 4
DKA = GROUP_W // HA
DVA = GROUP_W // HA
LB_FLOOR = 1e-30
HB = 4
Q_RANK = D_MODEL // 4
KV_RANK = D_MODEL // 8
NOPE = 128
ROPE = 64
VB = GROUP_W // HB
MLA_SCALE = (NOPE + ROPE) ** -0.5
HC = 8
PC = GROUP_W // HC
NC = 128
GC = 2
CONV_W = 3
CONV_CH = GROUP_W + 2 * GC * NC
HD = 8
DHD = GROUP_W // HD
WIN_R = 8
WIN_C = 16
N_KEYS = 128
N_EXPERTS = N_KEYS * N_KEYS
PEER_HEADS = 8
PEER_QDIM = 256
PEER_TOPK = 16

IN_SPLITS = (HA * DKA, HA * DKA, HA * DKA, HA * DVA, HA * DVA,
             Q_RANK, KV_RANK, ROPE,
             GROUP_W, GROUP_W, GC * NC, GC * NC, HC, HC,
             GROUP_W, GROUP_W, GROUP_W)
IN_W = sum(IN_SPLITS)
F32 = jnp.float32


BF16 = jnp.bfloat16
LANES = 128
VMEM_LIMIT = 56 * 1024 * 1024
NEG_BIG = -1e30


def _dot_nt(a, b):
    return lax.dot_general(a, b, (((1,), (1,)), ((), ())), preferred_element_type=F32)


def _dot_tn(a, b):
    return lax.dot_general(a, b, (((0,), (0,)), ((), ())), preferred_element_type=F32)


HG_SUB = 16
HG_NSUB = CHUNK // HG_SUB


def _split3_bf16(x):
    hi = x.astype(BF16)
    r1 = x - hi.astype(F32)
    mid = r1.astype(BF16)
    lo = (r1 - mid.astype(F32)).astype(BF16)
    return hi, mid, lo


def _hgrn_direction(rev, q_ref, f_ref, v_ref, la, lc, om, st_s, d, o_ref):
    fx = f_ref[...]
    logf = jnp.logaddexp(la, lc + jax.nn.log_sigmoid(fx))
    kk = om * jax.nn.sigmoid(-fx)
    qq = jax.nn.silu(q_ref[...]) * (DKA ** -0.5)
    vv = v_ref[...]
    t_io = lax.broadcasted_iota(jnp.int32, (CHUNK, CHUNK), 0)
    s_io = lax.broadcasted_iota(jnp.int32, (CHUNK, CHUNK), 1)
    tri = jnp.where((s_io >= t_io) if rev else (s_io <= t_io), 1.0, 0.0).astype(BF16)
    c = sum(jnp.dot(tri, part, preferred_element_type=F32) for part in _split3_bf16(logf))
    row = lax.broadcasted_iota(jnp.int32, (CHUNK, 1), 0)
    sub_row = lax.broadcasted_iota(jnp.int32, (HG_SUB, 1), 0)
    lane = lax.broadcasted_iota(jnp.int32, (HG_SUB, CHUNK), 1)
    outs = []
    for h in range(HA):
        cs = slice(h * DKA, (h + 1) * DKA)
        ch, qh, kh, vh = c[:, cs], qq[:, cs], kk[:, cs], vv[:, cs]
        st = st_s[d, h]
        o = _dot_nt((qh * jnp.exp(ch)).astype(BF16), st.astype(BF16))
        att = jnp.zeros((CHUNK, CHUNK), F32)
        for i in range(HG_NSUB):
            if rev:
                if i == HG_NSUB - 1:
                    continue
                c_ref = ch[(i + 1) * HG_SUB:(i + 1) * HG_SUB + 1]
                k_side = row >= (i + 1) * HG_SUB
            else:
                if i == 0:
                    continue
                c_ref = ch[i * HG_SUB - 1:i * HG_SUB]
                k_side = row < i * HG_SUB
            q_side = (row >= i * HG_SUB) & (row < (i + 1) * HG_SUB)
            qs = jnp.where(q_side, qh * jnp.exp(jnp.where(q_side, ch - c_ref, 0.0)), 0.0)
            ks = jnp.where(k_side, kh * jnp.exp(jnp.where(k_side, c_ref - ch, 0.0)), 0.0)
            att = att + _dot_nt(qs.astype(BF16), ks.astype(BF16))
        strips = []
        for i in range(HG_NSUB):
            blk = slice(i * HG_SUB, (i + 1) * HG_SUB)
            cb, qb, kb = ch[blk], qh[blk], kh[blk]
            strip = jnp.zeros((HG_SUB, CHUNK), F32)
            for s in range(HG_SUB):
                causal = (sub_row <= s) if rev else (sub_row >= s)
                w = jnp.exp(jnp.where(causal, cb - cb[s:s + 1], 0.0))
                col = jnp.sum(jnp.where(causal, w * qb * kb[s:s + 1], 0.0), axis=-1, keepdims=True)
                strip = jnp.where(lane == i * HG_SUB + s, col, strip)
            strips.append(strip)
        att = att + jnp.concatenate(strips, axis=0)
        o = o + jnp.dot(att.astype(BF16), vh.astype(BF16), preferred_element_type=F32)
        outs.append(o)
        c_end = ch[0:1] if rev else ch[CHUNK - 1:CHUNK]
        kd = kh * jnp.exp(c_end - ch)
        st_s[d, h] = st * jnp.exp(c_end) + _dot_tn(vh.astype(BF16), kd.astype(BF16))
    o_ref[...] = jnp.concatenate(outs, axis=-1)


def _hgrn_body(qf_ref, ff_ref, vf_ref, qb_ref, fb_ref, vb_ref, la_ref, lc_ref, om_ref, s0_ref,
               of_ref, ob_ref, sfin_ref, st_s):
    i = pl.program_id(1)

    @pl.when(i == 0)
    def _():
        for d in range(2):
            for h in range(HA):
                st_s[d, h] = s0_ref[d, h].T

    _hgrn_direction(False, qf_ref, ff_ref, vf_ref, la_ref[0:1], lc_ref[0:1], om_ref[0:1], st_s, 0, of_ref)
    _hgrn_direction(True, qb_ref, fb_ref, vb_ref, la_ref[1:2], lc_ref[1:2], om_ref[1:2], st_s, 1, ob_ref)

    @pl.when(i == pl.num_programs(1) - 1)
    def _():
        for d in range(2):
            for h in range(HA):
                sfin_ref[d, h] = st_s[d, h].T


def hgrn_scan(proj3, lb, s0):
    b, t, _ = proj3.shape
    w = HA * DKA
    n = t // CHUNK
    lb = lb.astype(F32)
    la = jnp.log(jnp.maximum(lb, LB_FLOOR))
    lc = jnp.log1p(-lb)
    om = 1.0 - lb

    def fwd_col(c):
        return pl.BlockSpec((None, CHUNK, w), lambda bi, i: (bi, i, c))

    def bwd_col(c):
        return pl.BlockSpec((None, CHUNK, w), lambda bi, i: (bi, n - 1 - i, c))
    fwd, bwd = fwd_col(0), bwd_col(0)
    par = pl.BlockSpec((2, w), lambda bi, i: (0, 0))
    st = pl.BlockSpec((None, 2, HA, DKA, DVA), lambda bi, i: (bi, 0, 0, 0, 0))
    q, f_fwd, f_bwd, v = proj3, proj3, proj3, proj3
    return pl.pallas_call(
        _hgrn_body,
        grid=(b, n),
        in_specs=[fwd_col(COL_HQ), fwd_col(COL_HFF), fwd_col(COL_HI), bwd_col(COL_HQ), bwd_col(COL_HFB),
                  bwd_col(COL_HI), par, par, par, st],
        out_specs=[fwd, bwd, st],
        out_shape=[jax.ShapeDtypeStruct((b, t, w), F32), jax.ShapeDtypeStruct((b, t, w), F32),
                   jax.ShapeDtypeStruct((b, 2, HA, DKA, DVA), F32)],
        scratch_shapes=[pltpu.VMEM((2, HA, DVA, DKA), F32)],
        compiler_params=pltpu.CompilerParams(dimension_semantics=("parallel", "arbitrary"),
                                             vmem_limit_bytes=VMEM_LIMIT),
        name="hgrn_scan",
    )(q, f_fwd, v, q, f_bwd, v, la, lc, om, s0)


def _hgrn_combine_body(of_ref, ob_ref, g_ref, w_ref, o_ref):
    y = of_ref[...] + ob_ref[...]
    g = g_ref[...]
    outs = []
    for h in range(HA):
        cs = slice(h * DVA, (h + 1) * DVA)
        yh = y[:, cs]
        yn = yh * lax.rsqrt(jnp.mean(yh * yh, axis=-1, keepdims=True) + EPS)
        outs.append(yn * w_ref[:, cs] * jax.nn.silu(g[:, cs]))
    o_ref[...] = jnp.concatenate(outs, axis=-1)


def hgrn_combine(o_f, o_b, proj2, norm_w):
    b, t, w = o_f.shape
    rows = 256
    blk = pl.BlockSpec((rows, w), lambda i: (i, 0))
    return pl.pallas_call(
        _hgrn_combine_body,
        grid=(b * t // rows,),
        in_specs=[blk, blk, pl.BlockSpec((rows, w), lambda i: (i, COL_HG)), pl.BlockSpec((1, w), lambda i: (0, 0))],
        out_specs=blk,
        out_shape=jax.ShapeDtypeStruct((b * t, w), F32),
        compiler_params=pltpu.CompilerParams(dimension_semantics=("parallel",)),
        name="hgrn_combine",
    )(o_f.reshape(b * t, w), o_b.reshape(b * t, w), proj2, norm_w.astype(F32).reshape(1, w))


NA_ROWS = DEC_SEQ // GRID_W
NA_WR = min(WIN_R, NA_ROWS)
NA_WIN = NA_WR * GRID_W


def natten_bias_table(rpb):
    cols = jnp.arange(GRID_W)
    start = jnp.clip(cols - WIN_C // 2, 0, GRID_W - WIN_C)
    in_win = (cols[None, :] >= start[:, None]) & (cols[None, :] < start[:, None] + WIN_C)
    c_off = jnp.clip(cols[None, :] - cols[:, None] + (WIN_C - 1), 0, 2 * WIN_C - 2)
    r_off = jnp.arange(NA_WR)[:, None] - (NA_WR - 1) + jnp.arange(NA_WR)[None, :] + (WIN_R - 1)
    tab = rpb.astype(F32)[:, r_off][:, :, :, c_off]
    tab = jnp.where(in_win[None, None, None], tab, NEG_BIG)
    return tab.transpose(0, 1, 3, 2, 4).reshape(HD, NA_WR, GRID_W, NA_WIN)


def _natten_body(q_ref, k_ref, v_ref, kc_ref, vc_ref, bias_ref, o_ref):
    r = pl.program_id(1)
    rs = jnp.clip(r - NA_WR // 2, 0, NA_ROWS - NA_WR)
    win = pl.ds(pl.multiple_of(rs * GRID_W, GRID_W), NA_WIN)
    q = q_ref[...] * (DHD ** -0.5)
    outs = []
    for h in range(HD):
        cs = slice(h * DHD, (h + 1) * DHD)
        qh = q[:, cs].astype(BF16)
        s_loc = _dot_nt(qh, k_ref[win, cs].astype(BF16)) + bias_ref[h]
        s_ctx = _dot_nt(qh, kc_ref[:, cs].astype(BF16))
        m = jnp.maximum(jnp.max(s_loc, axis=-1, keepdims=True), jnp.max(s_ctx, axis=-1, keepdims=True))
        p_loc = jnp.exp(s_loc - m)
        p_ctx = jnp.exp(s_ctx - m)
        inv = 1.0 / (jnp.sum(p_loc, axis=-1, keepdims=True) + jnp.sum(p_ctx, axis=-1, keepdims=True))
        outs.append(jnp.dot((p_loc * inv).astype(BF16), v_ref[win, cs].astype(BF16), preferred_element_type=F32)
                    + jnp.dot((p_ctx * inv).astype(BF16), vc_ref[:, cs].astype(BF16),
                              preferred_element_type=F32))
    o_ref[...] = jnp.concatenate(outs, axis=-1)


def natten_latent(proj3, rpb, k_ctx, v_ctx):
    b, t, _ = proj3.shape
    w = GROUP_W
    assert t == DEC_SEQ
    n_ctx = k_ctx.shape[1]
    bias = natten_bias_table(rpb)
    q = k = v = proj3

    def delta_idx(r):
        return jnp.clip(r - NA_WR // 2, 0, NA_ROWS - NA_WR) - r + (NA_WR - 1)
    ctx = pl.BlockSpec((None, n_ctx, w), lambda bi, r: (bi, 0, 0))
    return pl.pallas_call(
        _natten_body,
        grid=(b, NA_ROWS),
        in_specs=[pl.BlockSpec((None, GRID_W, w), lambda bi, r: (bi, r, COL_NQ)),
                  pl.BlockSpec((None, t, w), lambda bi, r: (bi, 0, COL_NK)),
                  pl.BlockSpec((None, t, w), lambda bi, r: (bi, 0, COL_NV)), ctx, ctx,
                  pl.BlockSpec((HD, None, GRID_W, NA_WIN), lambda bi, r: (0, delta_idx(r), 0, 0))],
        out_specs=pl.BlockSpec((None, GRID_W, w), lambda bi, r: (bi, r, 0)),
        out_shape=jax.ShapeDtypeStruct((b, t, w), F32),
        compiler_params=pltpu.CompilerParams(dimension_semantics=("parallel", "arbitrary"),
                                             vmem_limit_bytes=VMEM_LIMIT),
        name="natten_latent",
    )(q, k, v, k_ctx.reshape(b, n_ctx, w), v_ctx.reshape(b, n_ctx, w), bias)


PEER_KDIM = PEER_QDIM // 2
PEER_HP = PEER_HEADS * 2
PEER_PREP_TE = 512
PEER_SCORE_TM = 512
PEER_SELECT_TM = 256
PEER_TM = 512
PEER_TE = 1024
PEER_ROWS = PEER_TE // N_KEYS
PEER_CAND = [(a, b) for a in range(PEER_TOPK) for b in range(PEER_TOPK) if (a + 1) * (b + 1) <= PEER_TOPK]


def _peer_prep_body(u_ref, v_ref, ub_ref, vt_ref):
    ub_ref[...] = u_ref[...].astype(BF16)
    vt_ref[...] = v_ref[...].T.astype(BF16)


def peer_prep_tables(peer_u, peer_v):
    depth, n_e, d = peer_u.shape
    te = PEER_PREP_TE
    return pl.pallas_call(
        _peer_prep_body,
        grid=(depth, n_e // te),
        in_specs=[pl.BlockSpec((None, te, d), lambda l, j: (l, j, 0)),
                  pl.BlockSpec((None, te, d), lambda l, j: (l, j, 0))],
        out_specs=[pl.BlockSpec((None, te, d), lambda l, j: (l, j, 0)),
                   pl.BlockSpec((None, d, te), lambda l, j: (l, 0, j))],
        out_shape=[jax.ShapeDtypeStruct((depth, n_e, d), BF16), jax.ShapeDtypeStruct((depth, d, n_e), BF16)],
        compiler_params=pltpu.CompilerParams(dimension_semantics=("parallel", "parallel"),
                                             vmem_limit_bytes=VMEM_LIMIT),
        name="peer_prep_tables",
    )(peer_u, peer_v)


def _peer_score_body(ha_ref, hb_ref, wqt_ref, sk_ref, xt_ref, st_ref, *, tiles_a):
    h = jnp.where(pl.program_id(0) < tiles_a, ha_ref[...], hb_ref[...])
    xt = h.T.astype(BF16)
    xt_ref[...] = xt
    qt = jnp.dot(wqt_ref[...], xt, preferred_element_type=F32).astype(BF16)
    for hp in range(PEER_HP):
        st_ref[hp] = jnp.dot(sk_ref[hp % 2], qt[hp * PEER_KDIM:(hp + 1) * PEER_KDIM, :],
                             preferred_element_type=F32)


def peer_scores(h_a, h_b, wq_t, sub_keys):
    n_a, d = h_a.shape
    n = n_a + h_b.shape[0]
    tm = PEER_SCORE_TM
    tiles_a = n_a // tm
    return pl.pallas_call(
        functools.partial(_peer_score_body, tiles_a=tiles_a),
        grid=(n // tm,),
        in_specs=[pl.BlockSpec((tm, d), lambda i: (jnp.minimum(i, tiles_a - 1), 0)),
                  pl.BlockSpec((tm, d), lambda i: (jnp.maximum(i - tiles_a, 0), 0)),
                  pl.BlockSpec(wq_t.shape, lambda i: (0, 0)),
                  pl.BlockSpec(sub_keys.shape, lambda i: (0, 0, 0))],
        out_specs=[pl.BlockSpec((d, tm), lambda i: (0, i)),
                   pl.BlockSpec((PEER_HP, N_KEYS, tm), lambda i: (0, 0, i))],
        out_shape=[jax.ShapeDtypeStruct((d, n), BF16), jax.ShapeDtypeStruct((PEER_HP, N_KEYS, n), F32)],
        compiler_params=pltpu.CompilerParams(dimension_semantics=("parallel",), vmem_limit_bytes=VMEM_LIMIT),
        name="peer_scores",
    )(h_a, h_b, wq_t, sub_keys)


SUBLANES = 8
assert N_KEYS // SUBLANES == PEER_TOPK


def _bitonic_merge_desc(xs):
    xs = list(xs)
    j = len(xs) // 2
    while j >= 1:
        for i in range(len(xs)):
            if i & j == 0:
                xs[i], xs[i | j] = jnp.maximum(xs[i], xs[i | j]), jnp.minimum(xs[i], xs[i | j])
        j //= 2
    return xs


def _sort_desc(xs):
    xs = list(xs)
    n = len(xs)
    k = 2
    while k <= n:
        j = k // 2
        while j >= 1:
            for i in range(n):
                if i & j == 0:
                    hi, lo = jnp.maximum(xs[i], xs[i | j]), jnp.minimum(xs[i], xs[i | j])
                    xs[i], xs[i | j] = (hi, lo) if (i & k) == 0 else (lo, hi)
            j //= 2
        k *= 2
    return xs


def _peer_select_body(st_ref, e1_ref, cnt1_ref, rank2_ref, e2_ref, rank1_s, vtop_s, cnt_s, zinv_s):
    tm = st_ref.shape[-1]
    kio = lax.broadcasted_iota(jnp.int32, (N_KEYS, LANES), 0).astype(F32)
    neg = jnp.float32(-jnp.inf)

    def group(g, carry):
        ln = pl.ds(pl.multiple_of(g * LANES, LANES), LANES)

        def store_rank(hp, rank):
            if hp % 2 == 0:
                rank1_s[hp // 2, :, ln] = rank
            else:
                rank2_ref[hp // 2, :, ln] = rank

        ambiguous = jnp.zeros((SUBLANES, LANES), F32)
        for hp in range(PEER_HP):
            h, part = hp // 2, hp % 2
            tiles = [st_ref[hp, v * SUBLANES:(v + 1) * SUBLANES, ln] for v in range(N_KEYS // SUBLANES)]
            top = _sort_desc(list(tiles))
            for shift in (SUBLANES // 2, SUBLANES // 4, SUBLANES // 8):
                merged = [jnp.maximum(top[v], pltpu.roll(top[PEER_TOPK - 1 - v], shift, axis=0))
                          for v in range(PEER_TOPK)]
                top = _bitonic_merge_desc(merged)
            gap = top[0] - top[1]
            for a in range(1, PEER_TOPK - 1):
                gap = jnp.minimum(gap, top[a] - top[a + 1])
            n_sel = jnp.zeros((SUBLANES, LANES), jnp.int32)
            ranks = []
            for t in tiles:
                below = jnp.zeros((SUBLANES, LANES), jnp.int32)
                for a in range(PEER_TOPK):
                    below = below + lax.shift_right_arithmetic(pltpu.bitcast(t - top[a], jnp.int32), 31)
                ranks.append((-below).astype(F32))
                n_sel = n_sel - lax.shift_right_arithmetic(below + (PEER_TOPK - 1), 31) - 1
            n_sel = -n_sel
            for shift in (SUBLANES // 2, SUBLANES // 4, SUBLANES // 8):
                n_sel = n_sel + pltpu.roll(n_sel, shift, axis=0)
            ambiguous = ambiguous + jnp.where(gap == 0.0, 1.0, 0.0) + jnp.abs((n_sel - PEER_TOPK).astype(F32))
            for a in range(PEER_TOPK):
                vtop_s[part, a, h:h + 1, ln] = top[a][0:1, :]
            store_rank(hp, jnp.concatenate(ranks, axis=0))

        @pl.when(jnp.max(ambiguous) > 0.0)
        def _():
            for hp in range(PEER_HP):
                h, part = hp // 2, hp % 2

                def extract(it, sr):
                    s, rank = sr
                    m = jnp.max(s, axis=0, keepdims=True)
                    first = jnp.min(jnp.where(s == m, kio, float(N_KEYS)), axis=0, keepdims=True)
                    sel = kio == first
                    vtop_s[part, it, h:h + 1, ln] = m
                    return jnp.where(sel, neg, s), jnp.where(sel, it, rank)
                _, rank = lax.fori_loop(0, PEER_TOPK, extract,
                                        (st_ref[hp, :, ln], jnp.full((N_KEYS, LANES), PEER_TOPK, jnp.int32)))
                store_rank(hp, rank.astype(F32))

        v1 = [vtop_s[0, a, :, ln] for a in range(PEER_TOPK)]
        v2 = [vtop_s[1, b, :, ln] for b in range(PEER_TOPK)]
        sums = [v1[a] + v2[b] for a, b in PEER_CAND]
        n_c = len(PEER_CAND)
        beaten = [jnp.zeros((PEER_HEADS, LANES), F32) for _ in range(n_c)]
        for i in range(n_c):
            for j in range(i + 1, n_c):
                ge = sums[i] >= sums[j]
                beaten[j] = beaten[j] + jnp.where(ge, 1.0, 0.0)
                beaten[i] = beaten[i] + jnp.where(ge, 0.0, 1.0)
        z = jnp.zeros((PEER_HEADS, LANES), F32)
        cnt = [jnp.zeros((PEER_HEADS, LANES), F32) for _ in range(PEER_TOPK)]
        for i, (a, b) in enumerate(PEER_CAND):
            keep = beaten[i] < float(PEER_TOPK)
            z = z + jnp.where(keep, jnp.exp(sums[i] - sums[0]), 0.0)
            cnt[a] = cnt[a] + jnp.where(keep, 1.0, 0.0)
        for a in range(PEER_TOPK):
            cnt_s[a, :, ln] = cnt[a]
        zinv_s[:, ln] = 1.0 / z

        for h in range(PEER_HEADS):
            rank1 = rank1_s[h, :, ln]
            s1 = st_ref[2 * h, :, ln]
            e1 = jnp.exp(s1 - vtop_s[0, 0, h:h + 1, ln]) * zinv_s[h:h + 1, ln]
            e1_ref[h, :, ln] = jnp.where(rank1 < float(PEER_TOPK), e1, 0.0)
            c1 = jnp.zeros((N_KEYS, LANES), F32)
            for a in range(PEER_TOPK):
                c1 = c1 + jnp.where(rank1 == float(a), cnt_s[a, h:h + 1, ln], 0.0)
            cnt1_ref[h, :, ln] = c1
            e2_ref[h, :, ln] = jnp.exp(st_ref[2 * h + 1, :, ln] - vtop_s[1, 0, h:h + 1, ln])
        return carry
    lax.fori_loop(0, tm // LANES, group, 0)


def peer_select(st):
    _, _, n = st.shape
    tm = PEER_SELECT_TM
    spec = pl.BlockSpec((PEER_HEADS, N_KEYS, tm), lambda i: (0, 0, i))
    shp = jax.ShapeDtypeStruct((PEER_HEADS, N_KEYS, n), F32)
    return pl.pallas_call(
        _peer_select_body,
        grid=(n // tm,),
        in_specs=[pl.BlockSpec((PEER_HP, N_KEYS, tm), lambda i: (0, 0, i))],
        out_specs=[spec, spec, spec, spec],
        out_shape=[shp, shp, shp, shp],
        scratch_shapes=[pltpu.VMEM((PEER_HEADS, N_KEYS, tm), F32),
                        pltpu.VMEM((2, PEER_TOPK, PEER_HEADS, tm), F32),
                        pltpu.VMEM((PEER_TOPK, PEER_HEADS, tm), F32),
                        pltpu.VMEM((PEER_HEADS, tm), F32)],
        compiler_params=pltpu.CompilerParams(dimension_semantics=("parallel",), vmem_limit_bytes=VMEM_LIMIT),
        name="peer_select",
    )(st)


def _gelu_tanh(x):
    return 0.5 * x * (1.0 + jnp.tanh(math.sqrt(2.0 / math.pi) * (x + 0.044715 * (x * x * x))))


PEER_SUB_ROWS = 8
PEER_GATE_ROWS = 32


def _peer_expert_body(u_ref, vt_ref, xt_ref, e1_ref, cnt1_ref, rank2_ref, e2_ref, o_ref, st_s, at_s):
    j = pl.program_id(1)
    tm = xt_ref.shape[-1]

    @pl.when(j == 0)
    def _():
        o_ref[...] = jnp.zeros_like(o_ref)

    for c in range(PEER_ROWS // PEER_SUB_ROWS):
        sub = slice(c * PEER_SUB_ROWS * N_KEYS, (c + 1) * PEER_SUB_ROWS * N_KEYS)
        st_s[sub, :] = jnp.dot(u_ref[sub, :], xt_ref[...], preferred_element_type=F32)
        sub_rows = range(c * PEER_SUB_ROWS, (c + 1) * PEER_SUB_ROWS)
        n_blk = N_KEYS // PEER_GATE_ROWS

        def gate_block(idx, carry):
            ln = pl.ds(pl.multiple_of((idx // n_blk) * LANES, LANES), LANES)
            col0 = pl.multiple_of((idx % n_blk) * PEER_GATE_ROWS, PEER_GATE_ROWS)
            cols = pl.ds(col0, PEER_GATE_ROWS)
            gate = {r: jnp.zeros((PEER_GATE_ROWS, LANES), F32) for r in sub_rows}
            for h in range(PEER_HEADS):
                rank2 = rank2_ref[h, cols, ln]
                e2_bits = pltpu.bitcast(e2_ref[h, cols, ln], jnp.int32)
                for r in sub_rows:
                    keep = lax.shift_right_arithmetic(
                        pltpu.bitcast(rank2 - cnt1_ref[h, r:r + 1, ln], jnp.int32), 31)
                    gate[r] = gate[r] + pltpu.bitcast(e2_bits & keep, F32) * e1_ref[h, r:r + 1, ln]
            for r in sub_rows:
                rows = pl.ds(r * N_KEYS + col0, PEER_GATE_ROWS)
                at_s[rows, ln] = (_gelu_tanh(st_s[rows, ln]) * gate[r]).astype(BF16)
            return carry
        lax.fori_loop(0, (tm // LANES) * n_blk, gate_block, 0)
        o_ref[...] += jnp.dot(vt_ref[:, sub], at_s[sub, :], preferred_element_type=F32)


def peer_experts(ub, vt, xt, e1, cnt1, rank2, e2):
    n_e, d = ub.shape
    n = xt.shape[1]
    tm, te = PEER_TM, PEER_TE
    row_spec = pl.BlockSpec((PEER_HEADS, PEER_ROWS, tm), lambda i, j: (0, j, i))
    col_spec = pl.BlockSpec((PEER_HEADS, N_KEYS, tm), lambda i, j: (0, 0, i))
    return pl.pallas_call(
        _peer_expert_body,
        grid=(n // tm, n_e // te),
        in_specs=[pl.BlockSpec((te, d), lambda i, j: (j, 0)),
                  pl.BlockSpec((d, te), lambda i, j: (0, j)),
                  pl.BlockSpec((d, tm), lambda i, j: (0, i)),
                  row_spec, row_spec, col_spec, col_spec],
        out_specs=pl.BlockSpec((d, tm), lambda i, j: (0, i)),
        out_shape=jax.ShapeDtypeStruct((d, n), F32),
        scratch_shapes=[pltpu.VMEM((te, tm), F32), pltpu.VMEM((te, tm), BF16)],
        compiler_params=pltpu.CompilerParams(dimension_semantics=("parallel", "arbitrary"),
                                             vmem_limit_bytes=VMEM_LIMIT),
        name="peer_experts",
    )(ub, vt, xt, e1, cnt1, rank2, e2)


def peer_ffn(h_a, h_b, wq_t, sub_keys_b, ub, vt):
    xt, st = peer_scores(h_a, h_b, wq_t, sub_keys_b)
    e1, cnt1, rank2, e2 = peer_select(st)
    return peer_experts(ub, vt, xt, e1, cnt1, rank2, e2)


PROJ_ORDER = (0, 1, 2, 3, 4, 5, 8, 9, 14, 15, 16, 6, 10, 11, 7, 12, 13)
PROJ_W = 6528
COL_HQ, COL_HFF, COL_HFB, COL_HI, COL_HG, COL_CQ, COL_SX, COL_SZ, COL_NQ, COL_NK, COL_NV = range(11)
COL_CKV, COL_SB, COL_SC = 22, 23, 24
COL_SMALL = 50
SMALL_DT = ROPE
PROJ_TM = 512
PROJ_TN = 2176
MOD_SH1, MOD_SC1, MOD_G1, MOD_SH2, MOD_SC2, MOD_G2 = range(6)


def in_proj_weight(w_in):
    offs = np.cumsum((0,) + IN_SPLITS)
    cols = [w_in[:, offs[k]:offs[k + 1]] for k in PROJ_ORDER]
    cols.append(jnp.zeros((w_in.shape[0], PROJ_W - IN_W), w_in.dtype))
    return jnp.concatenate(cols, axis=1).astype(BF16)


def _adaln_body(c_ref, w_ref, b_ref, o_ref):
    a = jax.nn.silu(c_ref[...]).astype(BF16)
    o_ref[...] = jnp.dot(a, w_ref[...].astype(BF16), preferred_element_type=F32) + b_ref[...]


def adaln(cond, w_ada, b_ada):
    r, d = cond.shape
    n = w_ada.shape[1]
    tn = 1024
    return pl.pallas_call(
        _adaln_body,
        grid=(n // tn,),
        in_specs=[pl.BlockSpec((r, d), lambda j: (0, 0)), pl.BlockSpec((d, tn), lambda j: (0, j)),
                  pl.BlockSpec((1, tn), lambda j: (0, j))],
        out_specs=pl.BlockSpec((r, tn), lambda j: (0, j)),
        out_shape=jax.ShapeDtypeStruct((r, n), F32),
        compiler_params=pltpu.CompilerParams(dimension_semantics=("parallel",), vmem_limit_bytes=VMEM_LIMIT),
        name="adaln",
    )(cond, w_ada, b_ada.reshape(1, n))


def _rms_rows(x, w):
    return x * lax.rsqrt(jnp.mean(x * x, axis=-1, keepdims=True) + EPS) * w


def _in_proj_body(x_ref, nw_ref, sc_ref, sh_ref, w_ref, o_ref, h_s):
    @pl.when(pl.program_id(1) == 0)
    def _():
        h_s[...] = (_rms_rows(x_ref[...], nw_ref[...]) * (1.0 + sc_ref[...]) + sh_ref[...]).astype(BF16)
    o_ref[...] = jnp.dot(h_s[...], w_ref[...], preferred_element_type=F32)


def in_proj(x, norm_w, mod, w_perm, rows_per_mod):
    n, d = x.shape
    tm, tn = PROJ_TM, PROJ_TN

    def mod_spec(part):
        return pl.BlockSpec((None, 1, d), lambda i, j: ((i * tm) // rows_per_mod, 0, part))
    return pl.pallas_call(
        _in_proj_body,
        grid=(n // tm, PROJ_W // tn),
        in_specs=[pl.BlockSpec((tm, d), lambda i, j: (i, 0)), pl.BlockSpec((1, d), lambda i, j: (0, 0)),
                  mod_spec(MOD_SC1), mod_spec(MOD_SH1), pl.BlockSpec((d, tn), lambda i, j: (0, j))],
        out_specs=pl.BlockSpec((tm, tn), lambda i, j: (i, j)),
        out_shape=jax.ShapeDtypeStruct((n, PROJ_W), F32),
        scratch_shapes=[pltpu.VMEM((tm, d), BF16)],
        compiler_params=pltpu.CompilerParams(dimension_semantics=("parallel", "arbitrary"),
                                             vmem_limit_bytes=VMEM_LIMIT),
        name="in_proj",
    )(x, norm_w.reshape(1, d), mod, mod, w_perm)


def _out_proj_body(oa_ref, ob_ref, oc_ref, od_ref, w_ref, x_ref, g_ref, sc_ref, sh_ref, nw_ref, xo_ref, h_ref):
    mix = None
    for k, o_ref in enumerate((oa_ref, ob_ref, oc_ref, od_ref)):
        t = jnp.dot(o_ref[...].astype(BF16), w_ref[k * GROUP_W:(k + 1) * GROUP_W, :], preferred_element_type=F32)
        mix = t if mix is None else mix + t
    x = x_ref[...] + g_ref[...] * mix
    xo_ref[...] = x
    h_ref[...] = _rms_rows(x, nw_ref[...]) * (1.0 + sc_ref[...]) + sh_ref[...]


def out_proj(o_parts, w_out_b, x, mod, norm2_w, rows_per_mod):
    n, d = x.shape
    tm = 256

    def mod_spec(part):
        return pl.BlockSpec((None, 1, d), lambda i: ((i * tm) // rows_per_mod, 0, part))
    part = pl.BlockSpec((tm, GROUP_W), lambda i: (i, 0))
    row = pl.BlockSpec((tm, d), lambda i: (i, 0))
    return pl.pallas_call(
        _out_proj_body,
        grid=(n // tm,),
        in_specs=[part, part, part, part, pl.BlockSpec((D_MIX, d), lambda i: (0, 0)), row,
                  mod_spec(MOD_G1), mod_spec(MOD_SC2), mod_spec(MOD_SH2), pl.BlockSpec((1, d), lambda i: (0, 0))],
        out_specs=[row, row],
        out_shape=[jax.ShapeDtypeStruct((n, d), F32), jax.ShapeDtypeStruct((n, d), F32)],
        compiler_params=pltpu.CompilerParams(dimension_semantics=("parallel",), vmem_limit_bytes=VMEM_LIMIT),
        name="out_proj",
    )(*o_parts, w_out_b, x, mod, mod, mod, norm2_w.reshape(1, d))


def _norm_matmul_body(x_ref, nw_ref, w_ref, y_ref, xn_ref):
    xn = _rms_rows(x_ref[...], nw_ref[...])
    xn_ref[...] = xn
    y_ref[...] = jnp.dot(xn.astype(BF16), w_ref[...], preferred_element_type=F32)


def _matmul_body(x_ref, w_ref, y_ref):
    y_ref[...] = jnp.dot(x_ref[...].astype(BF16), w_ref[...], preferred_element_type=F32)


def norm_matmul(x, col, k, norm_w, w_b):
    n = x.shape[0]
    n_out = w_b.shape[1]
    tm = min(512, n)
    x_spec = pl.BlockSpec((tm, k), lambda i: (i, col))
    w_spec = pl.BlockSpec((k, n_out), lambda i: (0, 0))
    y_spec = pl.BlockSpec((tm, n_out), lambda i: (i, 0))
    params = pltpu.CompilerParams(dimension_semantics=("parallel",), vmem_limit_bytes=VMEM_LIMIT)
    if norm_w is None:
        return pl.pallas_call(_matmul_body, grid=(n // tm,), in_specs=[x_spec, w_spec], out_specs=y_spec,
                              out_shape=jax.ShapeDtypeStruct((n, n_out), F32), compiler_params=params,
                              name="matmul")(x, w_b)
    return pl.pallas_call(
        _norm_matmul_body, grid=(n // tm,),
        in_specs=[x_spec, pl.BlockSpec((1, k), lambda i: (0, 0)), w_spec],
        out_specs=[y_spec, pl.BlockSpec((tm, k), lambda i: (i, 0))],
        out_shape=[jax.ShapeDtypeStruct((n, n_out), F32), jax.ShapeDtypeStruct((n, k), F32)],
        compiler_params=params, name="norm_matmul")(x, norm_w.reshape(1, k), w_b)


ATTN_TQ = Q_BLOCK


def _attn_body(*refs, n_heads, parts, dv, scale, n_seg):
    n_p = len(parts)
    q_refs = refs[:n_p]
    pos = n_p
    segs = []
    for _ in range(n_seg):
        segs.append((refs[pos:pos + n_p], refs[pos + n_p]))
        pos += n_p + 1
    o_ref = refs[pos]
    outs = []
    for h in range(n_heads):
        scores = []
        for k_refs, _ in segs:
            s = None
            for q_ref, k_ref, (d, shared) in zip(q_refs, k_refs, parts):
                qh = q_ref[:, h * d:(h + 1) * d].astype(BF16)
                kh = (k_ref[:, 0:d] if shared else k_ref[:, h * d:(h + 1) * d]).astype(BF16)
                t = _dot_nt(qh, kh)
                s = t if s is None else s + t
            scores.append(s * scale)
        m = None
        for s in scores:
            ms = jnp.max(s, axis=-1, keepdims=True)
            m = ms if m is None else jnp.maximum(m, ms)
        probs = [jnp.exp(s - m) for s in scores]
        denom = None
        for p in probs:
            ps = jnp.sum(p, axis=-1, keepdims=True)
            denom = ps if denom is None else denom + ps
        inv = 1.0 / denom
        o = None
        for p, (_, v_ref) in zip(probs, segs):
            t = jnp.dot((p * inv).astype(BF16), v_ref[:, h * dv:(h + 1) * dv].astype(BF16),
                        preferred_element_type=F32)
            o = t if o is None else o + t
        outs.append(o)
    o_ref[...] = jnp.concatenate(outs, axis=-1)


def attention(q_parts, segments, n_heads, parts, dv, scale):
    b, tq, _ = q_parts[0][0].shape
    args, specs = [], []
    for arr, w, c in q_parts:
        args.append(arr)
        specs.append(pl.BlockSpec((None, ATTN_TQ, w), lambda bi, i, c=c: (bi, i, c)))
    for k_parts, v in segments:
        for arr, w, c in list(k_parts) + [v]:
            args.append(arr)
            specs.append(pl.BlockSpec((None, arr.shape[1], w), lambda bi, i, c=c: (bi, 0, c)))
    body = functools.partial(_attn_body, n_heads=n_heads, parts=parts, dv=dv, scale=scale, n_seg=len(segments))
    return pl.pallas_call(
        body,
        grid=(b, tq // ATTN_TQ),
        in_specs=specs,
        out_specs=pl.BlockSpec((None, ATTN_TQ, n_heads * dv), lambda bi, i: (bi, i, 0)),
        out_shape=jax.ShapeDtypeStruct((b, tq, n_heads * dv), F32),
        compiler_params=pltpu.CompilerParams(dimension_semantics=("parallel", "arbitrary"),
                                             vmem_limit_bytes=VMEM_LIMIT),
        name="attention",
    )(*args)


def mla_weights(w_q_up, w_kv_up):
    wq = w_q_up.reshape(Q_RANK, HB, NOPE + ROPE)
    wq = jnp.concatenate([wq[:, :, :NOPE].reshape(Q_RANK, HB * NOPE), wq[:, :, NOPE:].reshape(Q_RANK, HB * ROPE)], 1)
    wkv = w_kv_up.reshape(KV_RANK, HB, NOPE + VB)
    wkv = jnp.concatenate([wkv[:, :, :NOPE].reshape(KV_RANK, HB * NOPE), wkv[:, :, NOPE:].reshape(KV_RANK, HB * VB)], 1)
    return wq.astype(BF16), wkv.astype(BF16)


def rope_tables(t_len):
    t = jnp.arange(t_len)
    half = ROPE // 2
    inv = 1.0 / (ROPE_THETA ** (jnp.arange(0, half, 2, dtype=F32) / half))
    a_r = (t // GRID_W).astype(F32)[:, None] * inv[None, :]
    a_c = (t % GRID_W).astype(F32)[:, None] * inv[None, :]
    cos = jnp.concatenate([jnp.cos(a_r), jnp.cos(a_r), jnp.cos(a_c), jnp.cos(a_c)], axis=1)
    sin = jnp.concatenate([-jnp.sin(a_r), jnp.sin(a_r), -jnp.sin(a_c), jnp.sin(a_c)], axis=1)
    return cos, sin


def _rope_body(q_ref, small_ref, cos_ref, sin_ref, qo_ref, ko_ref):
    quarter = ROPE // 4
    r_io = lax.broadcasted_iota(jnp.int32, (ROPE, ROPE), 0)
    c_io = lax.broadcasted_iota(jnp.int32, (ROPE, ROPE), 1)
    partner = jnp.where((c_io // quarter) % 2 == 0, c_io + quarter, c_io - quarter)
    swap = jnp.where(r_io == partner, 1.0, 0.0).astype(BF16)
    cos, sin = cos_ref[...], sin_ref[...]

    def rot(x):
        xs = sum(jnp.dot(part, swap, preferred_element_type=F32) for part in _split3_bf16(x))
        return x * cos + xs * sin
    qo_ref[...] = jnp.concatenate([rot(q_ref[:, h * ROPE:(h + 1) * ROPE]) for h in range(HB)], axis=-1)
    ko_ref[...] = rot(small_ref[:, 0:ROPE])


def rope_rotate(q, proj3):
    b, t, _ = q.shape
    cos, sin = rope_tables(t)
    tb = 256
    tab = pl.BlockSpec((tb, ROPE), lambda bi, i: (i, 0))
    return pl.pallas_call(
        _rope_body,
        grid=(b, t // tb),
        in_specs=[pl.BlockSpec((None, tb, HB * ROPE), lambda bi, i: (bi, i, HB * NOPE // (HB * ROPE))),
                  pl.BlockSpec((None, tb, LANES), lambda bi, i: (bi, i, COL_SMALL)), tab, tab],
        out_specs=[pl.BlockSpec((None, tb, HB * ROPE), lambda bi, i: (bi, i, 0)),
                   pl.BlockSpec((None, tb, ROPE), lambda bi, i: (bi, i, 0))],
        out_shape=[jax.ShapeDtypeStruct((b, t, HB * ROPE), F32), jax.ShapeDtypeStruct((b, t, ROPE), F32)],
        compiler_params=pltpu.CompilerParams(dimension_semantics=("parallel", "parallel")),
        name="rope_rotate",
    )(q, proj3, cos, sin)


MLA_PARTS = ((NOPE, False), (ROPE, True))


def mla_mixer(proj2, b, t, p, cache):
    wq_b, wkv_b = p['mla_wq_b'], p['mla_wkv_b']
    q, _ = norm_matmul(proj2, COL_CQ, Q_RANK, p['mla_q_norm_w'], wq_b)
    kv, ckv_n = norm_matmul(proj2, COL_CKV, KV_RANK, p['mla_kv_norm_w'], wkv_b)
    q3, kv3, proj3 = q.reshape(b, t, -1), kv.reshape(b, t, -1), proj2.reshape(b, t, PROJ_W)
    if cache is None:
        o = attention([(q3, HB * NOPE, 0), (q3, HB * ROPE, 2)],
                      [([(kv3, HB * NOPE, 0), (proj3, LANES, COL_SMALL)], (kv3, HB * VB, 1))],
                      HB, MLA_PARTS, VB, MLA_SCALE)
        return o, ckv_n.reshape(b, t, KV_RANK), proj3[:, :, COL_SMALL * LANES:COL_SMALL * LANES + ROPE]
    ckv_ctx, krope_ctx = cache
    n_ctx = ckv_ctx.shape[1]
    kv_ctx = norm_matmul(ckv_ctx.reshape(b * n_ctx, KV_RANK).astype(F32), 0, KV_RANK, None, wkv_b)
    kv_ctx = kv_ctx.reshape(b, n_ctx, -1)
    q_rot, k_rot = rope_rotate(q3, proj3)
    o = attention([(q3, HB * NOPE, 0), (q_rot, HB * ROPE, 0)],
                  [([(kv3, HB * NOPE, 0), (k_rot, ROPE, 0)], (kv3, HB * VB, 1)),
                   ([(kv_ctx, HB * NOPE, 0), (krope_ctx.astype(F32), ROPE, 0)], (kv_ctx, HB * VB, 1))],
                  HB, MLA_PARTS, VB, MLA_SCALE)
    return o, None, None


def _ssd_conv_body(x_ref, b_ref, c_ref, w_ref, bias_ref, o_ref):
    t = x_ref.shape[0]
    xbc = jnp.concatenate([x_ref[...], b_ref[...], c_ref[...]], axis=-1)
    row = lax.broadcasted_iota(jnp.int32, (t, 1), 0)
    prev = jnp.where(row == 0, 0.0, pltpu.roll(xbc, 1, axis=0))
    nxt = jnp.where(row == t - 1, 0.0, pltpu.roll(xbc, t - 1, axis=0))
    o_ref[...] = jax.nn.silu(prev * w_ref[0:1] + xbc * w_ref[1:2] + nxt * w_ref[2:3] + bias_ref[...])


def ssd_conv(proj3, conv_w, conv_b):
    b, t, _ = proj3.shape
    return pl.pallas_call(
        _ssd_conv_body,
        grid=(b,),
        in_specs=[pl.BlockSpec((None, t, GROUP_W), lambda bi: (bi, 0, COL_SX)),
                  pl.BlockSpec((None, t, GC * NC), lambda bi: (bi, 0, COL_SB)),
                  pl.BlockSpec((None, t, GC * NC), lambda bi: (bi, 0, COL_SC)),
                  pl.BlockSpec((CONV_W, CONV_CH), lambda bi: (0, 0)), pl.BlockSpec((1, CONV_CH), lambda bi: (0, 0))],
        out_specs=pl.BlockSpec((None, t, CONV_CH), lambda bi: (bi, 0, 0)),
        out_shape=jax.ShapeDtypeStruct((b, t, CONV_CH), F32),
        compiler_params=pltpu.CompilerParams(dimension_semantics=("parallel",), vmem_limit_bytes=VMEM_LIMIT),
        name="ssd_conv",
    )(proj3, proj3, proj3, conv_w.astype(F32), conv_b.astype(F32).reshape(1, CONV_CH))


SSD_PAIRS = HC // 2


def _ssd_direction(rev, xbc_ref, dtc_ref, dtr_ref, bias_c, a_c, bias_r, a_r, st_s, d, y_ref):
    xbc = xbc_ref[...]
    dt_c = jax.nn.softplus(dtc_ref[...] + bias_c)
    la_c = dt_c * a_c
    la_r = jax.nn.softplus(dtr_ref[...] + bias_r) * a_r
    t_io = lax.broadcasted_iota(jnp.int32, (CHUNK, CHUNK), 0)
    s_io = lax.broadcasted_iota(jnp.int32, (CHUNK, CHUNK), 1)
    causal = (s_io >= t_io) if rev else (s_io <= t_io)
    tri = jnp.where(causal, 1.0, 0.0).astype(BF16)
    tri_t = jnp.where((t_io >= s_io) if rev else (t_io <= s_io), 1.0, 0.0).astype(BF16)
    cum_c = sum(jnp.dot(tri, part, preferred_element_type=F32) for part in _split3_bf16(la_c))
    cum_r = sum(jnp.dot(part, tri_t, preferred_element_type=F32) for part in _split3_bf16(la_r))
    end = 0 if rev else CHUNK - 1
    lane_lo = lax.broadcasted_iota(jnp.int32, (CHUNK, LANES), 1) < PC
    row_lo = lax.broadcasted_iota(jnp.int32, (LANES, NC), 0) < PC
    cbs = []
    for g in range(GC):
        bg = xbc[:, GROUP_W + g * NC:GROUP_W + (g + 1) * NC].astype(BF16)
        cg = xbc[:, GROUP_W + GC * NC + g * NC:GROUP_W + GC * NC + (g + 1) * NC].astype(BF16)
        cbs.append((bg, cg, _dot_nt(cg, bg)))
    outs = []
    for pr in range(SSD_PAIRS):
        ha, hb = 2 * pr, 2 * pr + 1
        bg, cg, cb = cbs[ha // (HC // GC)]
        xdt = xbc[:, pr * LANES:(pr + 1) * LANES] * jnp.where(lane_lo, dt_c[:, ha:ha + 1], dt_c[:, hb:hb + 1])
        ys = []
        for hx in (ha, hb):
            decay = jnp.exp(jnp.where(causal, cum_c[:, hx:hx + 1] - cum_r[hx:hx + 1, :], 0.0))
            m = (cb * jnp.where(causal, decay, 0.0)).astype(BF16)
            ys.append(jnp.dot(m, xdt.astype(BF16), preferred_element_type=F32))
        y = jnp.where(lane_lo, ys[0], ys[1])
        st = st_s[d, pr]
        e_cum = jnp.where(lane_lo, jnp.exp(cum_c[:, ha:ha + 1]), jnp.exp(cum_c[:, hb:hb + 1]))
        y = y + _dot_nt(cg, st.astype(BF16)) * e_cum
        outs.append(y)
        last_a, last_b = cum_c[end:end + 1, ha:ha + 1], cum_c[end:end + 1, hb:hb + 1]
        w = jnp.where(lane_lo, jnp.exp(last_a - cum_c[:, ha:ha + 1]), jnp.exp(last_b - cum_c[:, hb:hb + 1]))
        e_last = jnp.where(row_lo, jnp.exp(last_a), jnp.exp(last_b))
        st_s[d, pr] = e_last * st + _dot_tn((xdt * w).astype(BF16), bg)
    y_ref[...] = jnp.concatenate(outs, axis=-1)


def _ssd_body(xf_ref, xb_ref, dcf_ref, dcb_ref, drf_ref, drb_ref, bc_ref, ac_ref, br_ref, ar_ref, s0_ref,
              yf_ref, yb_ref, sfin_ref, st_s):
    i = pl.program_id(1)

    @pl.when(i == 0)
    def _():
        st_s[...] = s0_ref[...]

    _ssd_direction(False, xf_ref, dcf_ref, drf_ref, bc_ref[0], ac_ref[0], br_ref[0], ar_ref[0], st_s, 0, yf_ref)
    _ssd_direction(True, xb_ref, dcb_ref, drb_ref, bc_ref[1], ac_ref[1], br_ref[1], ar_ref[1], st_s, 1, yb_ref)

    @pl.when(i == pl.num_programs(1) - 1)
    def _():
        sfin_ref[...] = st_s[...]


def ssd_scan(xbc, dt_f, dt_b, dt_bias, a_log, s0):
    b, t, _ = xbc.shape
    n = t // CHUNK
    a = -jnp.exp(a_log.astype(F32))
    bias = dt_bias.astype(F32)
    dcf, dcb = dt_f.reshape(b, n, CHUNK, HC), dt_b.reshape(b, n, CHUNK, HC)
    drf, drb = dcf.transpose(0, 1, 3, 2), dcb.transpose(0, 1, 3, 2)
    x_f = pl.BlockSpec((None, CHUNK, CONV_CH), lambda bi, i: (bi, i, 0))
    x_b = pl.BlockSpec((None, CHUNK, CONV_CH), lambda bi, i: (bi, n - 1 - i, 0))
    y_f = pl.BlockSpec((None, CHUNK, GROUP_W), lambda bi, i: (bi, i, 0))
    y_b = pl.BlockSpec((None, CHUNK, GROUP_W), lambda bi, i: (bi, n - 1 - i, 0))
    c_f = pl.BlockSpec((None, None, CHUNK, HC), lambda bi, i: (bi, i, 0, 0))
    c_b = pl.BlockSpec((None, None, CHUNK, HC), lambda bi, i: (bi, n - 1 - i, 0, 0))
    r_f = pl.BlockSpec((None, None, HC, CHUNK), lambda bi, i: (bi, i, 0, 0))
    r_b = pl.BlockSpec((None, None, HC, CHUNK), lambda bi, i: (bi, n - 1 - i, 0, 0))
    p_c = pl.BlockSpec((2, 1, HC), lambda bi, i: (0, 0, 0))
    p_r = pl.BlockSpec((2, HC, 1), lambda bi, i: (0, 0, 0))
    st = pl.BlockSpec((None, 2, SSD_PAIRS, LANES, NC), lambda bi, i: (bi, 0, 0, 0, 0))
    y_fwd, y_bwd, s_fin = pl.pallas_call(
        _ssd_body,
        grid=(b, n),
        in_specs=[x_f, x_b, c_f, c_b, r_f, r_b, p_c, p_c, p_r, p_r, st],
        out_specs=[y_f, y_b, st],
        out_shape=[jax.ShapeDtypeStruct((b, t, GROUP_W), F32), jax.ShapeDtypeStruct((b, t, GROUP_W), F32),
                   jax.ShapeDtypeStruct((b, 2, SSD_PAIRS, LANES, NC), F32)],
        scratch_shapes=[pltpu.VMEM((2, SSD_PAIRS, LANES, NC), F32)],
        compiler_params=pltpu.CompilerParams(dimension_semantics=("parallel", "arbitrary"),
                                             vmem_limit_bytes=VMEM_LIMIT),
        name="ssd_scan",
    )(xbc, xbc, dcf, dcb, drf, drb, bias.reshape(2, 1, HC), a.reshape(2, 1, HC), bias.reshape(2, HC, 1),
      a.reshape(2, HC, 1), s0.astype(F32).reshape(b, 2, SSD_PAIRS, LANES, NC))
    return y_fwd, y_bwd, s_fin.reshape(b, 2, HC, PC, NC)


def _ssd_combine_body(x_ref, yf_ref, yb_ref, z_ref, d_ref, w_ref, o_ref):
    y = (d_ref[...] * x_ref[...] + yf_ref[...] + yb_ref[...]) * jax.nn.silu(z_ref[...])
    o_ref[...] = _rms_rows(y, w_ref[...])


def ssd_combine(xbc2, y_f, y_b, proj2, d_skip, norm_w):
    n = xbc2.shape[0]
    rows = 256
    blk = pl.BlockSpec((rows, GROUP_W), lambda i: (i, 0))
    par = pl.BlockSpec((1, GROUP_W), lambda i: (0, 0))
    return pl.pallas_call(
        _ssd_combine_body,
        grid=(n // rows,),
        in_specs=[blk, blk, blk, pl.BlockSpec((rows, GROUP_W), lambda i: (i, COL_SZ)), par, par],
        out_specs=blk,
        out_shape=jax.ShapeDtypeStruct((n, GROUP_W), F32),
        compiler_params=pltpu.CompilerParams(dimension_semantics=("parallel",)),
        name="ssd_combine",
    )(xbc2, y_f.reshape(n, GROUP_W), y_b.reshape(n, GROUP_W), proj2,
      jnp.repeat(d_skip.astype(F32), PC).reshape(1, GROUP_W), norm_w.astype(F32).reshape(1, GROUP_W))


def ssd_mixer(proj2, b, t, p, init_state):
    proj3 = proj2.reshape(b, t, PROJ_W)
    xbc = ssd_conv(proj3, p['ssd_conv_w'], p['ssd_conv_b'])
    dt0 = COL_SMALL * LANES + SMALL_DT
    dt_f, dt_b = proj3[:, :, dt0:dt0 + HC], proj3[:, :, dt0 + HC:dt0 + 2 * HC]
    s0 = jnp.zeros((b, 2, HC, PC, NC), F32) if init_state is None else init_state
    y_f, y_b, s_fin = ssd_scan(xbc, dt_f, dt_b, p['ssd_dt_bias'], p['ssd_a_log'], s0)
    return ssd_combine(xbc.reshape(b * t, CONV_CH), y_f, y_b, proj2, p['ssd_d'], p['ssd_norm_w']), s_fin


def _peer_finish_body(ot_ref, x_ref, g_ref, nw_ref, o_ref, *, final_norm):
    x = x_ref[...] + g_ref[...] * ot_ref[...].T
    o_ref[...] = _rms_rows(x, nw_ref[...]) if final_norm else x


def peer_finish(out_t, tok0, x, mod, rows_per_mod, final_norm_w=None):
    n, d = x.shape
    tm = 256
    norm_w = jnp.ones((1, d), F32) if final_norm_w is None else final_norm_w.astype(F32).reshape(1, d)
    return pl.pallas_call(
        functools.partial(_peer_finish_body, final_norm=final_norm_w is not None),
        grid=(n // tm,),
        in_specs=[pl.BlockSpec((d, tm), lambda i: (0, tok0 // tm + i)), pl.BlockSpec((tm, d), lambda i: (i, 0)),
                  pl.BlockSpec((None, 1, d), lambda i: ((i * tm) // rows_per_mod, 0, MOD_G2)),
                  pl.BlockSpec((1, d), lambda i: (0, 0))],
        out_specs=pl.BlockSpec((tm, d), lambda i: (i, 0)),
        out_shape=jax.ShapeDtypeStruct((n, d), F32),
        compiler_params=pltpu.CompilerParams(dimension_semantics=("parallel",), vmem_limit_bytes=VMEM_LIMIT),
        name="peer_finish",
    )(out_t, x, mod, norm_w)


def trunk_layer(x2, b, t, mod, p, cache):
    rows_per_mod = (b * t) // mod.shape[0]
    proj2 = in_proj(x2, p['norm1_w'], mod, p['w_in_b'], rows_per_mod)
    proj3 = proj2.reshape(b, t, PROJ_W)
    latent = cache is not None
    s0 = cache[0].astype(F32) if latent else jnp.zeros((b, 2, HA, DKA, DVA), F32)
    o_f, o_b, st_a = hgrn_scan(proj3, p['lb'], s0)
    o_a = hgrn_combine(o_f, o_b, proj2, p['hgrn_norm_w'])
    o_c, st_c = ssd_mixer(proj2, b, t, p, cache[1] if latent else None)
    na_q, na_k, na_v = (proj3, GROUP_W, COL_NQ), (proj3, GROUP_W, COL_NK), (proj3, GROUP_W, COL_NV)
    if latent:
        o_m, _, _ = mla_mixer(proj2, b, t, p, (cache[2], cache[3]))
        o_d = natten_latent(proj3, p['na_rpb'], cache[4], cache[5])
        new_state = None
    else:
        o_m, ckv, krope = mla_mixer(proj2, b, t, p, None)
        o_d = attention([na_q], [([na_k], na_v)], HD, ((DHD, False),), DHD, DHD ** -0.5)
        k_na = proj3[:, :, COL_NK * GROUP_W:(COL_NK + 1) * GROUP_W].reshape(b, t, HD, DHD)
        v_na = proj3[:, :, COL_NV * GROUP_W:(COL_NV + 1) * GROUP_W].reshape(b, t, HD, DHD)
        new_state = (st_a, st_c, ckv, krope, k_na, v_na)
    n = b * t
    x2, h2 = out_proj([o_a, o_m.reshape(n, GROUP_W), o_c, o_d.reshape(n, GROUP_W)], p['w_out_b'], x2, mod,
                      p['norm2_w'], rows_per_mod)
    return x2, h2, new_state


def kernel(x_prompt, x_sample, c, state_hgrn, state_ssd, cache_mla_ckv, cache_mla_krope, cache_na_k,
           cache_na_v, c_ctx, w_ada, b_ada, norm1_w, norm2_w, w_in, w_out, hgrn_lb_logits, hgrn_norm_w,
           mla_q_norm_w, mla_w_q_up, mla_kv_norm_w, mla_w_kv_up, ssd_conv_w, ssd_conv_b, ssd_dt_bias,
           ssd_a_log, ssd_d, ssd_norm_w, na_rpb, peer_w_q, peer_sub_keys, peer_u, peer_v, final_norm_w):
    lb_soft = jax.nn.softmax(hgrn_lb_logits.astype(F32), axis=0)
    lb_all = jnp.cumsum(lb_soft, axis=0) - lb_soft[0]
    stacked = {'w_ada': w_ada, 'b_ada': b_ada, 'norm1_w': norm1_w, 'norm2_w': norm2_w, 'w_in': w_in,
               'w_out': w_out, 'hgrn_norm_w': hgrn_norm_w, 'mla_q_norm_w': mla_q_norm_w,
               'mla_w_q_up': mla_w_q_up, 'mla_kv_norm_w': mla_kv_norm_w, 'mla_w_kv_up': mla_w_kv_up,
               'ssd_conv_w': ssd_conv_w, 'ssd_conv_b': ssd_conv_b, 'ssd_dt_bias': ssd_dt_bias,
               'ssd_a_log': ssd_a_log, 'ssd_d': ssd_d, 'ssd_norm_w': ssd_norm_w, 'na_rpb': na_rpb,
               'peer_w_q': peer_w_q, 'peer_sub_keys': peer_sub_keys, 'peer_u': peer_u, 'peer_v': peer_v}

    ub_all, vt_all = peer_prep_tables(peer_u, peer_v)
    n_ctx, n_lat = BATCH * SEQ, DEC_BATCH * DEC_SEQ
    cond = jnp.concatenate([c_ctx[None, :], c, jnp.zeros((8 - 1 - DEC_BATCH, D_MODEL), F32)], axis=0)

    xp, xs = x_prompt.reshape(n_ctx, D_MODEL), x_sample.reshape(n_lat, D_MODEL)
    ctx_states = []
    for l in range(DEPTH):
        p = {name: arr[l] for name, arr in stacked.items()}
        p['lb'] = lb_all[l]
        p['w_in_b'] = in_proj_weight(w_in[l])
        p['w_out_b'] = w_out[l].astype(BF16)
        p['mla_wq_b'], p['mla_wkv_b'] = mla_weights(mla_w_q_up[l], mla_w_kv_up[l])
        mod = adaln(cond, w_ada[l], b_ada[l])
        mod_p, mod_s = mod[0:1, None, :], mod[1:1 + DEC_BATCH, None, :]
        xp, h2p, st = trunk_layer(xp, BATCH, SEQ, mod_p, p, None)
        ctx_states.append(st)
        cache_l = (state_hgrn[:, l], state_ssd[:, l], cache_mla_ckv[:, l], cache_mla_krope[:, l],
                   cache_na_k[:, l], cache_na_v[:, l])
        xs, h2s, _ = trunk_layer(xs, DEC_BATCH, DEC_SEQ, mod_s, p, cache_l)
        out_t = peer_ffn(h2p, h2s, peer_w_q[l].T.astype(BF16), peer_sub_keys[l].astype(BF16), ub_all[l], vt_all[l])
        last_w = final_norm_w if l == DEPTH - 1 else None
        xp = peer_finish(out_t, 0, xp, mod_p, n_ctx, last_w)
        xs = peer_finish(out_t, n_ctx, xs, mod_s, DEC_SEQ, last_w)

    y_prompt = xp.reshape(BATCH, SEQ, D_MODEL)
    y_sample = xs.reshape(DEC_BATCH, DEC_SEQ, D_MODEL)
    new_state_hgrn = jnp.stack([s[0] for s in ctx_states], axis=1)
    new_state_ssd = jnp.stack([s[1] for s in ctx_states], axis=1)
    new_cache_mla_ckv = jnp.stack([s[2] for s in ctx_states], axis=1)
    new_cache_mla_krope = jnp.stack([s[3] for s in ctx_states], axis=1)
    new_cache_na_k = jnp.stack([s[4] for s in ctx_states], axis=1)
    new_cache_na_v = jnp.stack([s[5] for s in ctx_states], axis=1)
    return (y_prompt, y_sample, new_state_hgrn, new_state_ssd, new_cache_mla_ckv, new_cache_mla_krope,
            new_cache_na_k, new_cache_na_v)
```

```python
import functools
import math
import jax
import jax.numpy as jnp
from jax import lax
import numpy as np
from jax.experimental import pallas as pl
from jax.experimental.pallas import tpu as pltpu

D_MODEL = 2048
BATCH = 32
SEQ = 256
DEPTH = 2
DEC_BATCH = 4
DEC_SEQ = 1024
PAST_LEN = 256

GRID_W = 64
EPS = 1e-6
ROPE_THETA = 10000.0
Q_BLOCK = 128
CHUNK = 64
N_MIXERS = 4
GROUP_W = D_MODEL // N_MIXERS
D_MIX = N_MIXERS * GROUP_W
HA = 4
DKA = GROUP_W // HA
DVA = GROUP_W // HA
LB_FLOOR = 1e-30
HB = 4
Q_RANK = D_MODEL // 4
KV_RANK = D_MODEL // 8
NOPE = 128
ROPE = 64
VB = GROUP_W // HB
MLA_SCALE = (NOPE + ROPE) ** -0.5
HC = 8
PC = GROUP_W // HC
NC = 128
GC = 2
CONV_W = 3
CONV_CH = GROUP_W + 2 * GC * NC
HD = 8
DHD = GROUP_W // HD
WIN_R = 8
WIN_C = 16
N_KEYS = 128
N_EXPERTS = N_KEYS * N_KEYS
PEER_HEADS = 8
PEER_QDIM = 256
PEER_TOPK = 16

IN_SPLITS = (HA * DKA, HA * DKA, HA * DKA, HA * DVA, HA * DVA,
             Q_RANK, KV_RANK, ROPE,
             GROUP_W, GROUP_W, GC * NC, GC * NC, HC, HC,
             GROUP_W, GROUP_W, GROUP_W)
IN_W = sum(IN_SPLITS)
F32 = jnp.float32


BF16 = jnp.bfloat16
LANES = 128
VMEM_LIMIT = 56 * 1024 * 1024
NEG_BIG = -1e30


def _dot_nt(a, b):
    return lax.dot_general(a, b, (((1,), (1,)), ((), ())), preferred_element_type=F32)


def _dot_tn(a, b):
    return lax.dot_general(a, b, (((0,), (0,)), ((), ())), preferred_element_type=F32)


HG_SUB = 16
HG_NSUB = CHUNK // HG_SUB


def _split3_bf16(x):
    hi = x.astype(BF16)
    r1 = x - hi.astype(F32)
    mid = r1.astype(BF16)
    lo = (r1 - mid.astype(F32)).astype(BF16)
    return hi, mid, lo


def _hgrn_direction(rev, q_ref, f_ref, v_ref, la, lc, om, st_s, d, o_ref):
    fx = f_ref[...]
    logf = jnp.logaddexp(la, lc + jax.nn.log_sigmoid(fx))
    kk = om * jax.nn.sigmoid(-fx)
    qq = jax.nn.silu(q_ref[...]) * (DKA ** -0.5)
    vv = v_ref[...]
    t_io = lax.broadcasted_iota(jnp.int32, (CHUNK, CHUNK), 0)
    s_io = lax.broadcasted_iota(jnp.int32, (CHUNK, CHUNK), 1)
    tri = jnp.where((s_io >= t_io) if rev else (s_io <= t_io), 1.0, 0.0).astype(BF16)
    c = sum(jnp.dot(tri, part, preferred_element_type=F32) for part in _split3_bf16(logf))
    row = lax.broadcasted_iota(jnp.int32, (CHUNK, 1), 0)
    sub_row = lax.broadcasted_iota(jnp.int32, (HG_SUB, 1), 0)
    lane = lax.broadcasted_iota(jnp.int32, (HG_SUB, CHUNK), 1)
    outs = []
    for h in range(HA):
        cs = slice(h * DKA, (h + 1) * DKA)
        ch, qh, kh, vh = c[:, cs], qq[:, cs], kk[:, cs], vv[:, cs]
        st = st_s[d, h]
        o = _dot_nt((qh * jnp.exp(ch)).astype(BF16), st.astype(BF16))
        att = jnp.zeros((CHUNK, CHUNK), F32)
        for i in range(HG_NSUB):
            if rev:
                if i == HG_NSUB - 1:
                    continue
                c_ref = ch[(i + 1) * HG_SUB:(i + 1) * HG_SUB + 1]
                k_side = row >= (i + 1) * HG_SUB
            else:
                if i == 0:
                    continue
                c_ref = ch[i * HG_SUB - 1:i * HG_SUB]
                k_side = row < i * HG_SUB
            q_side = (row >= i * HG_SUB) & (row < (i + 1) * HG_SUB)
            qs = jnp.where(q_side, qh * jnp.exp(jnp.where(q_side, ch - c_ref, 0.0)), 0.0)
            ks = jnp.where(k_side, kh * jnp.exp(jnp.where(k_side, c_ref - ch, 0.0)), 0.0)
            att = att + _dot_nt(qs.astype(BF16), ks.astype(BF16))
        strips = []
        for i in range(HG_NSUB):
            blk = slice(i * HG_SUB, (i + 1) * HG_SUB)
            cb, qb, kb = ch[blk], qh[blk], kh[blk]
            strip = jnp.zeros((HG_SUB, CHUNK), F32)
            for s in range(HG_SUB):
                causal = (sub_row <= s) if rev else (sub_row >= s)
                w = jnp.exp(jnp.where(causal, cb - cb[s:s + 1], 0.0))
                col = jnp.sum(jnp.where(causal, w * qb * kb[s:s + 1], 0.0), axis=-1, keepdims=True)
                strip = jnp.where(lane == i * HG_SUB + s, col, strip)
            strips.append(strip)
        att = att + jnp.concatenate(strips, axis=0)
        o = o + jnp.dot(att.astype(BF16), vh.astype(BF16), preferred_element_type=F32)
        outs.append(o)
        c_end = ch[0:1] if rev else ch[CHUNK - 1:CHUNK]
        kd = kh * jnp.exp(c_end - ch)
        st_s[d, h] = st * jnp.exp(c_end) + _dot_tn(vh.astype(BF16), kd.astype(BF16))
    o_ref[...] = jnp.concatenate(outs, axis=-1)


def _hgrn_body(qf_ref, ff_ref, vf_ref, qb_ref, fb_ref, vb_ref, la_ref, lc_ref, om_ref, s0_ref,
               of_ref, ob_ref, sfin_ref, st_s):
    i = pl.program_id(1)

    @pl.when(i == 0)
    def _():
        for d in range(2):
            for h in range(HA):
                st_s[d, h] = s0_ref[d, h].T

    _hgrn_direction(False, qf_ref, ff_ref, vf_ref, la_ref[0:1], lc_ref[0:1], om_ref[0:1], st_s, 0, of_ref)
    _hgrn_direction(True, qb_ref, fb_ref, vb_ref, la_ref[1:2], lc_ref[1:2], om_ref[1:2], st_s, 1, ob_ref)

    @pl.when(i == pl.num_programs(1) - 1)
    def _():
        for d in range(2):
            for h in range(HA):
                sfin_ref[d, h] = st_s[d, h].T


def hgrn_scan(proj3, lb, s0):
    b, t, _ = proj3.shape
    w = HA * DKA
    n = t // CHUNK
    lb = lb.astype(F32)
    la = jnp.log(jnp.maximum(lb, LB_FLOOR))
    lc = jnp.log1p(-lb)
    om = 1.0 - lb

    def fwd_col(c):
        return pl.BlockSpec((None, CHUNK, w), lambda bi, i: (bi, i, c))

    def bwd_col(c):
        return pl.BlockSpec((None, CHUNK, w), lambda bi, i: (bi, n - 1 - i, c))
    fwd, bwd = fwd_col(0), bwd_col(0)
    par = pl.BlockSpec((2, w), lambda bi, i: (0, 0))
    st = pl.BlockSpec((None, 2, HA, DKA, DVA), lambda bi, i: (bi, 0, 0, 0, 0))
    q, f_fwd, f_bwd, v = proj3, proj3, proj3, proj3
    return pl.pallas_call(
        _hgrn_body,
        grid=(b, n),
        in_specs=[fwd_col(COL_HQ), fwd_col(COL_HFF), fwd_col(COL_HI), bwd_col(COL_HQ), bwd_col(COL_HFB),
                  bwd_col(COL_HI), par, par, par, st],
        out_specs=[fwd, bwd, st],
        out_shape=[jax.ShapeDtypeStruct((b, t, w), F32), jax.ShapeDtypeStruct((b, t, w), F32),
                   jax.ShapeDtypeStruct((b, 2, HA, DKA, DVA), F32)],
        scratch_shapes=[pltpu.VMEM((2, HA, DVA, DKA), F32)],
        compiler_params=pltpu.CompilerParams(dimension_semantics=("parallel", "arbitrary"),
                                             vmem_limit_bytes=VMEM_LIMIT),
        name="hgrn_scan",
    )(q, f_fwd, v, q, f_bwd, v, la, lc, om, s0)


def _hgrn_combine_body(of_ref, ob_ref, g_ref, w_ref, o_ref):
    y = of_ref[...] + ob_ref[...]
    g = g_ref[...]
    outs = []
    for h in range(HA):
        cs = slice(h * DVA, (h + 1) * DVA)
        yh = y[:, cs]
        yn = yh * lax.rsqrt(jnp.mean(yh * yh, axis=-1, keepdims=True) + EPS)
        outs.append(yn * w_ref[:, cs] * jax.nn.silu(g[:, cs]))
    o_ref[...] = jnp.concatenate(outs, axis=-1)


def hgrn_combine(o_f, o_b, proj2, norm_w):
    b, t, w = o_f.shape
    rows = 256
    blk = pl.BlockSpec((rows, w), lambda i: (i, 0))
    return pl.pallas_call(
        _hgrn_combine_body,
        grid=(b * t // rows,),
        in_specs=[blk, blk, pl.BlockSpec((rows, w), lambda i: (i, COL_HG)), pl.BlockSpec((1, w), lambda i: (0, 0))],
        out_specs=blk,
        out_shape=jax.ShapeDtypeStruct((b * t, w), F32),
        compiler_params=pltpu.CompilerParams(dimension_semantics=("parallel",)),
        name="hgrn_combine",
    )(o_f.reshape(b * t, w), o_b.reshape(b * t, w), proj2, norm_w.astype(F32).reshape(1, w))


NA_ROWS = DEC_SEQ // GRID_W
NA_WR = min(WIN_R, NA_ROWS)
NA_WIN = NA_WR * GRID_W


def natten_bias_table(rpb):
    cols = jnp.arange(GRID_W)
    start = jnp.clip(cols - WIN_C // 2, 0, GRID_W - WIN_C)
    in_win = (cols[None, :] >= start[:, None]) & (cols[None, :] < start[:, None] + WIN_C)
    c_off = jnp.clip(cols[None, :] - cols[:, None] + (WIN_C - 1), 0, 2 * WIN_C - 2)
    cols_tab = rpb.astype(F32)[:, :, c_off]
    row0 = WIN_R - NA_WR
    tab = jnp.stack([cols_tab[:, row0 + d:row0 + d + NA_WR] for d in range(NA_WR)], axis=1)
    tab = jnp.where(in_win[None, None, None], tab, NEG_BIG)
    return tab.transpose(0, 1, 3, 2, 4).reshape(HD, NA_WR, GRID_W, NA_WIN)


def _natten_body(q_ref, k_ref, v_ref, kc_ref, vc_ref, bias_ref, o_ref):
    r = pl.program_id(1)
    rs = jnp.clip(r - NA_WR // 2, 0, NA_ROWS - NA_WR)
    win = pl.ds(pl.multiple_of(rs * GRID_W, GRID_W), NA_WIN)
    q = q_ref[...] * (DHD ** -0.5)
    outs = []
    for h in range(HD):
        cs = slice(h * DHD, (h + 1) * DHD)
        qh = q[:, cs].astype(BF16)
        s_loc = _dot_nt(qh, k_ref[win, cs].astype(BF16)) + bias_ref[h]
        s_ctx = _dot_nt(qh, kc_ref[:, cs].astype(BF16))
        m = jnp.maximum(jnp.max(s_loc, axis=-1, keepdims=True), jnp.max(s_ctx, axis=-1, keepdims=True))
        p_loc = jnp.exp(s_loc - m)
        p_ctx = jnp.exp(s_ctx - m)
        inv = 1.0 / (jnp.sum(p_loc, axis=-1, keepdims=True) + jnp.sum(p_ctx, axis=-1, keepdims=True))
        outs.append(jnp.dot((p_loc * inv).astype(BF16), v_ref[win, cs].astype(BF16), preferred_element_type=F32)
                    + jnp.dot((p_ctx * inv).astype(BF16), vc_ref[:, cs].astype(BF16),
                              preferred_element_type=F32))
    o_ref[...] = jnp.concatenate(outs, axis=-1)


def natten_latent(proj3, rpb, k_ctx, v_ctx):
    b, t, _ = proj3.shape
    w = GROUP_W
    assert t == DEC_SEQ
    n_ctx = k_ctx.shape[1]
    bias = natten_bias_table(rpb)
    q = k = v = proj3

    def delta_idx(r):
        return jnp.clip(r - NA_WR // 2, 0, NA_ROWS - NA_WR) - r + (NA_WR - 1)
    ctx = pl.BlockSpec((None, n_ctx, w), lambda bi, r: (bi, 0, 0))
    return pl.pallas_call(
        _natten_body,
        grid=(b, NA_ROWS),
        in_specs=[pl.BlockSpec((None, GRID_W, w), lambda bi, r: (bi, r, COL_NQ)),
                  pl.BlockSpec((None, t, w), lambda bi, r: (bi, 0, COL_NK)),
                  pl.BlockSpec((None, t, w), lambda bi, r: (bi, 0, COL_NV)), ctx, ctx,
                  pl.BlockSpec((HD, None, GRID_W, NA_WIN), lambda bi, r: (0, delta_idx(r), 0, 0))],
        out_specs=pl.BlockSpec((None, GRID_W, w), lambda bi, r: (bi, r, 0)),
        out_shape=jax.ShapeDtypeStruct((b, t, w), F32),
        compiler_params=pltpu.CompilerParams(dimension_semantics=("parallel", "arbitrary"),
                                             vmem_limit_bytes=VMEM_LIMIT),
        name="natten_latent",
    )(q, k, v, k_ctx.reshape(b, n_ctx, w), v_ctx.reshape(b, n_ctx, w), bias)


PEER_KDIM = PEER_QDIM // 2
PEER_HP = PEER_HEADS * 2
PEER_PREP_TE = 512
PEER_SCORE_TM = 512
PEER_SELECT_TM = 256
PEER_TM = 512
PEER_TE = 1024
PEER_ROWS = PEER_TE // N_KEYS
PEER_CAND = [(a, b) for a in range(PEER_TOPK) for b in range(PEER_TOPK) if (a + 1) * (b + 1) <= PEER_TOPK]


def _peer_prep_body(u_ref, v_ref, ub_ref, vt_ref):
    ub_ref[...] = u_ref[...].astype(BF16)
    vt_ref[...] = v_ref[...].T.astype(BF16)


def peer_prep_tables(peer_u, peer_v):
    depth, n_e, d = peer_u.shape
    te = PEER_PREP_TE
    return pl.pallas_call(
        _peer_prep_body,
        grid=(depth, n_e // te),
        in_specs=[pl.BlockSpec((None, te, d), lambda l, j: (l, j, 0)),
                  pl.BlockSpec((None, te, d), lambda l, j: (l, j, 0))],
        out_specs=[pl.BlockSpec((None, te, d), lambda l, j: (l, j, 0)),
                   pl.BlockSpec((None, d, te), lambda l, j: (l, 0, j))],
        out_shape=[jax.ShapeDtypeStruct((depth, n_e, d), BF16), jax.ShapeDtypeStruct((depth, d, n_e), BF16)],
        compiler_params=pltpu.CompilerParams(dimension_semantics=("parallel", "parallel"),
                                             vmem_limit_bytes=VMEM_LIMIT),
        name="peer_prep_tables",
    )(peer_u, peer_v)


def _peer_score_body(ha_ref, hb_ref, wqt_ref, sk_ref, xt_ref, st_ref, *, tiles_a):
    h = jnp.where(pl.program_id(0) < tiles_a, ha_ref[...], hb_ref[...])
    xt = h.T.astype(BF16)
    xt_ref[...] = xt
    qt = jnp.dot(wqt_ref[...], xt, preferred_element_type=F32).astype(BF16)
    for hp in range(PEER_HP):
        st_ref[hp] = jnp.dot(sk_ref[hp % 2], qt[hp * PEER_KDIM:(hp + 1) * PEER_KDIM, :],
                             preferred_element_type=F32)


def peer_scores(h_a, h_b, wq_t, sub_keys):
    n_a, d = h_a.shape
    n = n_a + h_b.shape[0]
    tm = PEER_SCORE_TM
    tiles_a = n_a // tm
    return pl.pallas_call(
        functools.partial(_peer_score_body, tiles_a=tiles_a),
        grid=(n // tm,),
        in_specs=[pl.BlockSpec((tm, d), lambda i: (jnp.minimum(i, tiles_a - 1), 0)),
                  pl.BlockSpec((tm, d), lambda i: (jnp.maximum(i - tiles_a, 0), 0)),
                  pl.BlockSpec(wq_t.shape, lambda i: (0, 0)),
                  pl.BlockSpec(sub_keys.shape, lambda i: (0, 0, 0))],
        out_specs=[pl.BlockSpec((d, tm), lambda i: (0, i)),
                   pl.BlockSpec((PEER_HP, N_KEYS, tm), lambda i: (0, 0, i))],
        out_shape=[jax.ShapeDtypeStruct((d, n), BF16), jax.ShapeDtypeStruct((PEER_HP, N_KEYS, n), F32)],
        compiler_params=pltpu.CompilerParams(dimension_semantics=("parallel",), vmem_limit_bytes=VMEM_LIMIT),
        name="peer_scores",
    )(h_a, h_b, wq_t, sub_keys)


SUBLANES = 8
assert N_KEYS // SUBLANES == PEER_TOPK


def _bitonic_merge_desc(xs):
    xs = list(xs)
    j = len(xs) // 2
    while j >= 1:
        for i in range(len(xs)):
            if i & j == 0:
                xs[i], xs[i | j] = jnp.maximum(xs[i], xs[i | j]), jnp.minimum(xs[i], xs[i | j])
        j //= 2
    return xs


def _sort_desc(xs):
    xs = list(xs)
    n = len(xs)
    k = 2
    while k <= n:
        j = k // 2
        while j >= 1:
            for i in range(n):
                if i & j == 0:
                    hi, lo = jnp.maximum(xs[i], xs[i | j]), jnp.minimum(xs[i], xs[i | j])
                    xs[i], xs[i | j] = (hi, lo) if (i & k) == 0 else (lo, hi)
            j //= 2
        k *= 2
    return xs


def _peer_select_body(st_ref, e1_ref, cnt1_ref, rank2_ref, e2_ref, rank1_s, vtop_s, cnt_s, zinv_s):
    tm = st_ref.shape[-1]
    kio = lax.broadcasted_iota(jnp.int32, (N_KEYS, LANES), 0).astype(F32)
    neg = jnp.float32(-jnp.inf)

    def group(g, carry):
        ln = pl.ds(pl.multiple_of(g * LANES, LANES), LANES)

        def store_rank(hp, rank):
            if hp % 2 == 0:
                rank1_s[hp // 2, :, ln] = rank
            else:
                rank2_ref[hp // 2, :, ln] = rank

        ambiguous = jnp.zeros((SUBLANES, LANES), F32)
        for hp in range(PEER_HP):
            h, part = hp // 2, hp % 2
            tiles = [st_ref[hp, v * SUBLANES:(v + 1) * SUBLANES, ln] for v in range(N_KEYS // SUBLANES)]
            top = _sort_desc(list(tiles))
            for shift in (SUBLANES // 2, SUBLANES // 4, SUBLANES // 8):
                merged = [jnp.maximum(top[v], pltpu.roll(top[PEER_TOPK - 1 - v], shift, axis=0))
                          for v in range(PEER_TOPK)]
                top = _bitonic_merge_desc(merged)
            gap = top[0] - top[1]
            for a in range(1, PEER_TOPK - 1):
                gap = jnp.minimum(gap, top[a] - top[a + 1])
            n_sel = jnp.zeros((SUBLANES, LANES), jnp.int32)
            ranks = []
            for t in tiles:
                below = jnp.zeros((SUBLANES, LANES), jnp.int32)
                for a in range(PEER_TOPK):
                    below = below + lax.shift_right_arithmetic(pltpu.bitcast(t - top[a], jnp.int32), 31)
                ranks.append((-below).astype(F32))
                n_sel = n_sel - lax.shift_right_arithmetic(below + (PEER_TOPK - 1), 31) - 1
            n_sel = -n_sel
            for shift in (SUBLANES // 2, SUBLANES // 4, SUBLANES // 8):
                n_sel = n_sel + pltpu.roll(n_sel, shift, axis=0)
            ambiguous = ambiguous + jnp.where(gap == 0.0, 1.0, 0.0) + jnp.abs((n_sel - PEER_TOPK).astype(F32))
            for a in range(PEER_TOPK):
                vtop_s[part, a, h:h + 1, ln] = top[a][0:1, :]
            store_rank(hp, jnp.concatenate(ranks, axis=0))

        @pl.when(jnp.max(ambiguous) > 0.0)
        def _():
            for hp in range(PEER_HP):
                h, part = hp // 2, hp % 2

                def extract(it, sr):
                    s, rank = sr
                    m = jnp.max(s, axis=0, keepdims=True)
                    first = jnp.min(jnp.where(s == m, kio, float(N_KEYS)), axis=0, keepdims=True)
                    sel = kio == first
                    vtop_s[part, it, h:h + 1, ln] = m
                    return jnp.where(sel, neg, s), jnp.where(sel, it, rank)
                _, rank = lax.fori_loop(0, PEER_TOPK, extract,
                                        (st_ref[hp, :, ln], jnp.full((N_KEYS, LANES), PEER_TOPK, jnp.int32)))
                store_rank(hp, rank.astype(F32))

        v1 = [vtop_s[0, a, :, ln] for a in range(PEER_TOPK)]
        v2 = [vtop_s[1, b, :, ln] for b in range(PEER_TOPK)]
        sums = [v1[a] + v2[b] for a, b in PEER_CAND]
        n_c = len(PEER_CAND)
        beaten = [jnp.zeros((PEER_HEADS, LANES), F32) for _ in range(n_c)]
        for i in range(n_c):
            for j in range(i + 1, n_c):
                ge = sums[i] >= sums[j]
                beaten[j] = beaten[j] + jnp.where(ge, 1.0, 0.0)
                beaten[i] = beaten[i] + jnp.where(ge, 0.0, 1.0)
        z = jnp.zeros((PEER_HEADS, LANES), F32)
        cnt = [jnp.zeros((PEER_HEADS, LANES), F32) for _ in range(PEER_TOPK)]
        for i, (a, b) in enumerate(PEER_CAND):
            keep = beaten[i] < float(PEER_TOPK)
            z = z + jnp.where(keep, jnp.exp(sums[i] - sums[0]), 0.0)
            cnt[a] = cnt[a] + jnp.where(keep, 1.0, 0.0)
        for a in range(PEER_TOPK):
            cnt_s[a, :, ln] = cnt[a]
        zinv_s[:, ln] = 1.0 / z

        for h in range(PEER_HEADS):
            rank1 = rank1_s[h, :, ln]
            s1 = st_ref[2 * h, :, ln]
            e1 = jnp.exp(s1 - vtop_s[0, 0, h:h + 1, ln]) * zinv_s[h:h + 1, ln]
            e1_ref[h, :, ln] = jnp.where(rank1 < float(PEER_TOPK), e1, 0.0)
            c1 = jnp.zeros((N_KEYS, LANES), F32)
            for a in range(PEER_TOPK):
                c1 = c1 + jnp.where(rank1 == float(a), cnt_s[a, h:h + 1, ln], 0.0)
            cnt1_ref[h, :, ln] = c1
            e2_ref[h, :, ln] = jnp.exp(st_ref[2 * h + 1, :, ln] - vtop_s[1, 0, h:h + 1, ln])
        return carry
    lax.fori_loop(0, tm // LANES, group, 0)


def peer_select(st):
    _, _, n = st.shape
    tm = PEER_SELECT_TM
    spec = pl.BlockSpec((PEER_HEADS, N_KEYS, tm), lambda i: (0, 0, i))
    shp = jax.ShapeDtypeStruct((PEER_HEADS, N_KEYS, n), F32)
    return pl.pallas_call(
        _peer_select_body,
        grid=(n // tm,),
        in_specs=[pl.BlockSpec((PEER_HP, N_KEYS, tm), lambda i: (0, 0, i))],
        out_specs=[spec, spec, spec, spec],
        out_shape=[shp, shp, shp, shp],
        scratch_shapes=[pltpu.VMEM((PEER_HEADS, N_KEYS, tm), F32),
                        pltpu.VMEM((2, PEER_TOPK, PEER_HEADS, tm), F32),
                        pltpu.VMEM((PEER_TOPK, PEER_HEADS, tm), F32),
                        pltpu.VMEM((PEER_HEADS, tm), F32)],
        compiler_params=pltpu.CompilerParams(dimension_semantics=("parallel",), vmem_limit_bytes=VMEM_LIMIT),
        name="peer_select",
    )(st)


def _gelu_tanh(x):
    return 0.5 * x * (1.0 + jnp.tanh(math.sqrt(2.0 / math.pi) * (x + 0.044715 * (x * x * x))))


PEER_SUB_ROWS = 8
PEER_GATE_ROWS = 32


def _peer_expert_body(u_ref, vt_ref, xt_ref, e1_ref, cnt1_ref, rank2_ref, e2_ref, o_ref, st_s, at_s):
    j = pl.program_id(1)
    tm = xt_ref.shape[-1]

    @pl.when(j == 0)
    def _():
        o_ref[...] = jnp.zeros_like(o_ref)

    for c in range(PEER_ROWS // PEER_SUB_ROWS):
        sub = slice(c * PEER_SUB_ROWS * N_KEYS, (c + 1) * PEER_SUB_ROWS * N_KEYS)
        st_s[sub, :] = jnp.dot(u_ref[sub, :], xt_ref[...], preferred_element_type=F32)
        sub_rows = range(c * PEER_SUB_ROWS, (c + 1) * PEER_SUB_ROWS)
        n_blk = N_KEYS // PEER_GATE_ROWS

        def gate_block(idx, carry):
            ln = pl.ds(pl.multiple_of((idx // n_blk) * LANES, LANES), LANES)
            col0 = pl.multiple_of((idx % n_blk) * PEER_GATE_ROWS, PEER_GATE_ROWS)
            cols = pl.ds(col0, PEER_GATE_ROWS)
            gate = {r: jnp.zeros((PEER_GATE_ROWS, LANES), F32) for r in sub_rows}
            for h in range(PEER_HEADS):
                rank2 = rank2_ref[h, cols, ln]
                e2_bits = pltpu.bitcast(e2_ref[h, cols, ln], jnp.int32)
                for r in sub_rows:
                    keep = lax.shift_right_arithmetic(
                        pltpu.bitcast(rank2 - cnt1_ref[h, r:r + 1, ln], jnp.int32), 31)
                    gate[r] = gate[r] + pltpu.bitcast(e2_bits & keep, F32) * e1_ref[h, r:r + 1, ln]
            for r in sub_rows:
                rows = pl.ds(r * N_KEYS + col0, PEER_GATE_ROWS)
                at_s[rows, ln] = (_gelu_tanh(st_s[rows, ln]) * gate[r]).astype(BF16)
            return carry
        lax.fori_loop(0, (tm // LANES) * n_blk, gate_block, 0)
        o_ref[...] += jnp.dot(vt_ref[:, sub], at_s[sub, :], preferred_element_type=F32)


def peer_experts(ub, vt, xt, e1, cnt1, rank2, e2):
    n_e, d = ub.shape
    n = xt.shape[1]
    tm, te = PEER_TM, PEER_TE
    row_spec = pl.BlockSpec((PEER_HEADS, PEER_ROWS, tm), lambda i, j: (0, j, i))
    col_spec = pl.BlockSpec((PEER_HEADS, N_KEYS, tm), lambda i, j: (0, 0, i))
    return pl.pallas_call(
        _peer_expert_body,
        grid=(n // tm, n_e // te),
        in_specs=[pl.BlockSpec((te, d), lambda i, j: (j, 0)),
                  pl.BlockSpec((d, te), lambda i, j: (0, j)),
                  pl.BlockSpec((d, tm), lambda i, j: (0, i)),
                  row_spec, row_spec, col_spec, col_spec],
        out_specs=pl.BlockSpec((d, tm), lambda i, j: (0, i)),
        out_shape=jax.ShapeDtypeStruct((d, n), F32),
        scratch_shapes=[pltpu.VMEM((te, tm), F32), pltpu.VMEM((te, tm), BF16)],
        compiler_params=pltpu.CompilerParams(dimension_semantics=("parallel", "arbitrary"),
                                             vmem_limit_bytes=VMEM_LIMIT),
        name="peer_experts",
    )(ub, vt, xt, e1, cnt1, rank2, e2)


def peer_ffn(h_a, h_b, wq_t, sub_keys_b, ub, vt):
    xt, st = peer_scores(h_a, h_b, wq_t, sub_keys_b)
    e1, cnt1, rank2, e2 = peer_select(st)
    return peer_experts(ub, vt, xt, e1, cnt1, rank2, e2)


PROJ_ORDER = (0, 1, 2, 3, 4, 5, 8, 9, 14, 15, 16, 6, 10, 11, 7, 12, 13)
PROJ_W = 6528
COL_HQ, COL_HFF, COL_HFB, COL_HI, COL_HG, COL_CQ, COL_SX, COL_SZ, COL_NQ, COL_NK, COL_NV = range(11)
COL_CKV, COL_SB, COL_SC = 22, 23, 24
COL_SMALL = 50
SMALL_DT = ROPE
PROJ_TM = 512
PROJ_TN = 2176
MOD_SH1, MOD_SC1, MOD_G1, MOD_SH2, MOD_SC2, MOD_G2 = range(6)


def in_proj_weight(w_in):
    offs = np.cumsum((0,) + IN_SPLITS)
    cols = [w_in[:, offs[k]:offs[k + 1]] for k in PROJ_ORDER]
    cols.append(jnp.zeros((w_in.shape[0], PROJ_W - IN_W), w_in.dtype))
    return jnp.concatenate(cols, axis=1).astype(BF16)


def _adaln_body(c_ref, w_ref, b_ref, o_ref):
    a = jax.nn.silu(c_ref[...]).astype(BF16)
    o_ref[...] = jnp.dot(a, w_ref[...].astype(BF16), preferred_element_type=F32) + b_ref[...]


def adaln(cond, w_ada, b_ada):
    r, d = cond.shape
    n = w_ada.shape[1]
    tn = 1024
    return pl.pallas_call(
        _adaln_body,
        grid=(n // tn,),
        in_specs=[pl.BlockSpec((r, d), lambda j: (0, 0)), pl.BlockSpec((d, tn), lambda j: (0, j)),
                  pl.BlockSpec((1, tn), lambda j: (0, j))],
        out_specs=pl.BlockSpec((r, tn), lambda j: (0, j)),
        out_shape=jax.ShapeDtypeStruct((r, n), F32),
        compiler_params=pltpu.CompilerParams(dimension_semantics=("parallel",), vmem_limit_bytes=VMEM_LIMIT),
        name="adaln",
    )(cond, w_ada, b_ada.reshape(1, n))


def _rms_rows(x, w):
    return x * lax.rsqrt(jnp.mean(x * x, axis=-1, keepdims=True) + EPS) * w


def _in_proj_body(x_ref, nw_ref, sc_ref, sh_ref, w_ref, o_ref, h_s):
    @pl.when(pl.program_id(1) == 0)
    def _():
        h_s[...] = (_rms_rows(x_ref[...], nw_ref[...]) * (1.0 + sc_ref[...]) + sh_ref[...]).astype(BF16)
    o_ref[...] = jnp.dot(h_s[...], w_ref[...], preferred_element_type=F32)


def in_proj(x, norm_w, mod, w_perm, rows_per_mod):
    n, d = x.shape
    tm, tn = PROJ_TM, PROJ_TN

    def mod_spec(part):
        return pl.BlockSpec((None, 1, d), lambda i, j: ((i * tm) // rows_per_mod, 0, part))
    return pl.pallas_call(
        _in_proj_body,
        grid=(n // tm, PROJ_W // tn),
        in_specs=[pl.BlockSpec((tm, d), lambda i, j: (i, 0)), pl.BlockSpec((1, d), lambda i, j: (0, 0)),
                  mod_spec(MOD_SC1), mod_spec(MOD_SH1), pl.BlockSpec((d, tn), lambda i, j: (0, j))],
        out_specs=pl.BlockSpec((tm, tn), lambda i, j: (i, j)),
        out_shape=jax.ShapeDtypeStruct((n, PROJ_W), F32),
        scratch_shapes=[pltpu.VMEM((tm, d), BF16)],
        compiler_params=pltpu.CompilerParams(dimension_semantics=("parallel", "arbitrary"),
                                             vmem_limit_bytes=VMEM_LIMIT),
        name="in_proj",
    )(x, norm_w.reshape(1, d), mod, mod, w_perm)


def _out_proj_body(oa_ref, ob_ref, oc_ref, od_ref, w_ref, x_ref, g_ref, sc_ref, sh_ref, nw_ref, xo_ref, h_ref):
    mix = None
    for k, o_ref in enumerate((oa_ref, ob_ref, oc_ref, od_ref)):
        t = jnp.dot(o_ref[...].astype(BF16), w_ref[k * GROUP_W:(k + 1) * GROUP_W, :], preferred_element_type=F32)
        mix = t if mix is None else mix + t
    x = x_ref[...] + g_ref[...] * mix
    xo_ref[...] = x
    h_ref[...] = _rms_rows(x, nw_ref[...]) * (1.0 + sc_ref[...]) + sh_ref[...]


def out_proj(o_parts, w_out_b, x, mod, norm2_w, rows_per_mod):
    n, d = x.shape
    tm = 256

    def mod_spec(part):
        return pl.BlockSpec((None, 1, d), lambda i: ((i * tm) // rows_per_mod, 0, part))
    part = pl.BlockSpec((tm, GROUP_W), lambda i: (i, 0))
    row = pl.BlockSpec((tm, d), lambda i: (i, 0))
    return pl.pallas_call(
        _out_proj_body,
        grid=(n // tm,),
        in_specs=[part, part, part, part, pl.BlockSpec((D_MIX, d), lambda i: (0, 0)), row,
                  mod_spec(MOD_G1), mod_spec(MOD_SC2), mod_spec(MOD_SH2), pl.BlockSpec((1, d), lambda i: (0, 0))],
        out_specs=[row, row],
        out_shape=[jax.ShapeDtypeStruct((n, d), F32), jax.ShapeDtypeStruct((n, d), F32)],
        compiler_params=pltpu.CompilerParams(dimension_semantics=("parallel",), vmem_limit_bytes=VMEM_LIMIT),
        name="out_proj",
    )(*o_parts, w_out_b, x, mod, mod, mod, norm2_w.reshape(1, d))


def _norm_matmul_body(x_ref, nw_ref, w_ref, y_ref, xn_ref):
    xn = _rms_rows(x_ref[...], nw_ref[...])
    xn_ref[...] = xn
    y_ref[...] = jnp.dot(xn.astype(BF16), w_ref[...], preferred_element_type=F32)


def _matmul_body(x_ref, w_ref, y_ref):
    y_ref[...] = jnp.dot(x_ref[...].astype(BF16), w_ref[...], preferred_element_type=F32)


def norm_matmul(x, col, k, norm_w, w_b):
    n = x.shape[0]
    n_out = w_b.shape[1]
    tm = min(512, n)
    x_spec = pl.BlockSpec((tm, k), lambda i: (i, col))
    w_spec = pl.BlockSpec((k, n_out), lambda i: (0, 0))
    y_spec = pl.BlockSpec((tm, n_out), lambda i: (i, 0))
    params = pltpu.CompilerParams(dimension_semantics=("parallel",), vmem_limit_bytes=VMEM_LIMIT)
    if norm_w is None:
        return pl.pallas_call(_matmul_body, grid=(n // tm,), in_specs=[x_spec, w_spec], out_specs=y_spec,
                              out_shape=jax.ShapeDtypeStruct((n, n_out), F32), compiler_params=params,
                              name="matmul")(x, w_b)
    return pl.pallas_call(
        _norm_matmul_body, grid=(n // tm,),
        in_specs=[x_spec, pl.BlockSpec((1, k), lambda i: (0, 0)), w_spec],
        out_specs=[y_spec, pl.BlockSpec((tm, k), lambda i: (i, 0))],
        out_shape=[jax.ShapeDtypeStruct((n, n_out), F32), jax.ShapeDtypeStruct((n, k), F32)],
        compiler_params=params, name="norm_matmul")(x, norm_w.reshape(1, k), w_b)


ATTN_TQ = Q_BLOCK


def _attn_body(*refs, n_heads, parts, dv, scale, n_seg):
    n_p = len(parts)
    q_refs = refs[:n_p]
    pos = n_p
    segs = []
    for _ in range(n_seg):
        segs.append((refs[pos:pos + n_p], refs[pos + n_p]))
        pos += n_p + 1
    o_ref = refs[pos]
    outs = []
    for h in range(n_heads):
        scores = []
        for k_refs, _ in segs:
            s = None
            for q_ref, k_ref, (d, shared) in zip(q_refs, k_refs, parts):
                qh = q_ref[:, h * d:(h + 1) * d].astype(BF16)
                kh = (k_ref[:, 0:d] if shared else k_ref[:, h * d:(h + 1) * d]).astype(BF16)
                t = _dot_nt(qh, kh)
                s = t if s is None else s + t
            scores.append(s * scale)
        m = None
        for s in scores:
            ms = jnp.max(s, axis=-1, keepdims=True)
            m = ms if m is None else jnp.maximum(m, ms)
        probs = [jnp.exp(s - m) for s in scores]
        denom = None
        for p in probs:
            ps = jnp.sum(p, axis=-1, keepdims=True)
            denom = ps if denom is None else denom + ps
        inv = 1.0 / denom
        o = None
        for p, (_, v_ref) in zip(probs, segs):
            t = jnp.dot((p * inv).astype(BF16), v_ref[:, h * dv:(h + 1) * dv].astype(BF16),
                        preferred_element_type=F32)
            o = t if o is None else o + t
        outs.append(o)
    o_ref[...] = jnp.concatenate(outs, axis=-1)


def attention(q_parts, segments, n_heads, parts, dv, scale):
    b, tq, _ = q_parts[0][0].shape
    args, specs = [], []
    for arr, w, c in q_parts:
        args.append(arr)
        specs.append(pl.BlockSpec((None, ATTN_TQ, w), lambda bi, i, c=c: (bi, i, c)))
    for k_parts, v in segments:
        for arr, w, c in list(k_parts) + [v]:
            args.append(arr)
            specs.append(pl.BlockSpec((None, arr.shape[1], w), lambda bi, i, c=c: (bi, 0, c)))
    body = functools.partial(_attn_body, n_heads=n_heads, parts=parts, dv=dv, scale=scale, n_seg=len(segments))
    return pl.pallas_call(
        body,
        grid=(b, tq // ATTN_TQ),
        in_specs=specs,
        out_specs=pl.BlockSpec((None, ATTN_TQ, n_heads * dv), lambda bi, i: (bi, i, 0)),
        out_shape=jax.ShapeDtypeStruct((b, tq, n_heads * dv), F32),
        compiler_params=pltpu.CompilerParams(dimension_semantics=("parallel", "arbitrary"),
                                             vmem_limit_bytes=VMEM_LIMIT),
        name="attention",
    )(*args)


def mla_weights(w_q_up, w_kv_up):
    wq = w_q_up.reshape(Q_RANK, HB, NOPE + ROPE)
    wq = jnp.concatenate([wq[:, :, :NOPE].reshape(Q_RANK, HB * NOPE), wq[:, :, NOPE:].reshape(Q_RANK, HB * ROPE)], 1)
    wkv = w_kv_up.reshape(KV_RANK, HB, NOPE + VB)
    wkv = jnp.concatenate([wkv[:, :, :NOPE].reshape(KV_RANK, HB * NOPE), wkv[:, :, NOPE:].reshape(KV_RANK, HB * VB)], 1)
    return wq.astype(BF16), wkv.astype(BF16)


def rope_tables(t_len):
    t = jnp.arange(t_len)
    half = ROPE // 2
    inv = 1.0 / (ROPE_THETA ** (jnp.arange(0, half, 2, dtype=F32) / half))
    a_r = (t // GRID_W).astype(F32)[:, None] * inv[None, :]
    a_c = (t % GRID_W).astype(F32)[:, None] * inv[None, :]
    cos = jnp.concatenate([jnp.cos(a_r), jnp.cos(a_r), jnp.cos(a_c), jnp.cos(a_c)], axis=1)
    sin = jnp.concatenate([-jnp.sin(a_r), jnp.sin(a_r), -jnp.sin(a_c), jnp.sin(a_c)], axis=1)
    return cos, sin


def _rope_body(q_ref, small_ref, cos_ref, sin_ref, qo_ref, ko_ref):
    quarter = ROPE // 4
    r_io = lax.broadcasted_iota(jnp.int32, (ROPE, ROPE), 0)
    c_io = lax.broadcasted_iota(jnp.int32, (ROPE, ROPE), 1)
    partner = jnp.where((c_io // quarter) % 2 == 0, c_io + quarter, c_io - quarter)
    swap = jnp.where(r_io == partner, 1.0, 0.0).astype(BF16)
    cos, sin = cos_ref[...], sin_ref[...]

    def rot(x):
        xs = sum(jnp.dot(part, swap, preferred_element_type=F32) for part in _split3_bf16(x))
        return x * cos + xs * sin
    qo_ref[...] = jnp.concatenate([rot(q_ref[:, h * ROPE:(h + 1) * ROPE]) for h in range(HB)], axis=-1)
    ko_ref[...] = rot(small_ref[:, 0:ROPE])


def rope_rotate(q, proj3):
    b, t, _ = q.shape
    cos, sin = rope_tables(t)
    tb = 256
    tab = pl.BlockSpec((tb, ROPE), lambda bi, i: (i, 0))
    return pl.pallas_call(
        _rope_body,
        grid=(b, t // tb),
        in_specs=[pl.BlockSpec((None, tb, HB * ROPE), lambda bi, i: (bi, i, HB * NOPE // (HB * ROPE))),
                  pl.BlockSpec((None, tb, LANES), lambda bi, i: (bi, i, COL_SMALL)), tab, tab],
        out_specs=[pl.BlockSpec((None, tb, HB * ROPE), lambda bi, i: (bi, i, 0)),
                   pl.BlockSpec((None, tb, ROPE), lambda bi, i: (bi, i, 0))],
        out_shape=[jax.ShapeDtypeStruct((b, t, HB * ROPE), F32), jax.ShapeDtypeStruct((b, t, ROPE), F32)],
        compiler_params=pltpu.CompilerParams(dimension_semantics=("parallel", "parallel")),
        name="rope_rotate",
    )(q, proj3, cos, sin)


MLA_PARTS = ((NOPE, False), (ROPE, True))


def mla_mixer(proj2, b, t, p, cache):
    wq_b, wkv_b = p['mla_wq_b'], p['mla_wkv_b']
    q, _ = norm_matmul(proj2, COL_CQ, Q_RANK, p['mla_q_norm_w'], wq_b)
    kv, ckv_n = norm_matmul(proj2, COL_CKV, KV_RANK, p['mla_kv_norm_w'], wkv_b)
    q3, kv3, proj3 = q.reshape(b, t, -1), kv.reshape(b, t, -1), proj2.reshape(b, t, PROJ_W)
    if cache is None:
        o = attention([(q3, HB * NOPE, 0), (q3, HB * ROPE, 2)],
                      [([(kv3, HB * NOPE, 0), (proj3, LANES, COL_SMALL)], (kv3, HB * VB, 1))],
                      HB, MLA_PARTS, VB, MLA_SCALE)
        return o, ckv_n.reshape(b, t, KV_RANK), proj3[:, :, COL_SMALL * LANES:COL_SMALL * LANES + ROPE]
    ckv_ctx, krope_ctx = cache
    n_ctx = ckv_ctx.shape[1]
    kv_ctx = norm_matmul(ckv_ctx.reshape(b * n_ctx, KV_RANK).astype(F32), 0, KV_RANK, None, wkv_b)
    kv_ctx = kv_ctx.reshape(b, n_ctx, -1)
    q_rot, k_rot = rope_rotate(q3, proj3)
    o = attention([(q3, HB * NOPE, 0), (q_rot, HB * ROPE, 0)],
                  [([(kv3, HB * NOPE, 0), (k_rot, ROPE, 0)], (kv3, HB * VB, 1)),
                   ([(kv_ctx, HB * NOPE, 0), (krope_ctx.astype(F32), ROPE, 0)], (kv_ctx, HB * VB, 1))],
                  HB, MLA_PARTS, VB, MLA_SCALE)
    return o, None, None


def _ssd_conv_body(x_ref, b_ref, c_ref, w_ref, bias_ref, o_ref):
    t = x_ref.shape[0]
    xbc = jnp.concatenate([x_ref[...], b_ref[...], c_ref[...]], axis=-1)
    row = lax.broadcasted_iota(jnp.int32, (t, 1), 0)
    prev = jnp.where(row == 0, 0.0, pltpu.roll(xbc, 1, axis=0))
    nxt = jnp.where(row == t - 1, 0.0, pltpu.roll(xbc, t - 1, axis=0))
    o_ref[...] = jax.nn.silu(prev * w_ref[0:1] + xbc * w_ref[1:2] + nxt * w_ref[2:3] + bias_ref[...])


def ssd_conv(proj3, conv_w, conv_b):
    b, t, _ = proj3.shape
    return pl.pallas_call(
        _ssd_conv_body,
        grid=(b,),
        in_specs=[pl.BlockSpec((None, t, GROUP_W), lambda bi: (bi, 0, COL_SX)),
                  pl.BlockSpec((None, t, GC * NC), lambda bi: (bi, 0, COL_SB)),
                  pl.BlockSpec((None, t, GC * NC), lambda bi: (bi, 0, COL_SC)),
                  pl.BlockSpec((CONV_W, CONV_CH), lambda bi: (0, 0)), pl.BlockSpec((1, CONV_CH), lambda bi: (0, 0))],
        out_specs=pl.BlockSpec((None, t, CONV_CH), lambda bi: (bi, 0, 0)),
        out_shape=jax.ShapeDtypeStruct((b, t, CONV_CH), F32),
        compiler_params=pltpu.CompilerParams(dimension_semantics=("parallel",), vmem_limit_bytes=VMEM_LIMIT),
        name="ssd_conv",
    )(proj3, proj3, proj3, conv_w.astype(F32), conv_b.astype(F32).reshape(1, CONV_CH))


SSD_PAIRS = HC // 2


def _ssd_direction(rev, xbc_ref, dtc_ref, dtr_ref, bias_c, a_c, bias_r, a_r, st_s, d, y_ref):
    xbc = xbc_ref[...]
    dt_c = jax.nn.softplus(dtc_ref[...] + bias_c)
    la_c = dt_c * a_c
    la_r = jax.nn.softplus(dtr_ref[...] + bias_r) * a_r
    t_io = lax.broadcasted_iota(jnp.int32, (CHUNK, CHUNK), 0)
    s_io = lax.broadcasted_iota(jnp.int32, (CHUNK, CHUNK), 1)
    causal = (s_io >= t_io) if rev else (s_io <= t_io)
    tri = jnp.where(causal, 1.0, 0.0).astype(BF16)
    tri_t = jnp.where((t_io >= s_io) if rev else (t_io <= s_io), 1.0, 0.0).astype(BF16)
    cum_c = sum(jnp.dot(tri, part, preferred_element_type=F32) for part in _split3_bf16(la_c))
    cum_r = sum(jnp.dot(part, tri_t, preferred_element_type=F32) for part in _split3_bf16(la_r))
    end = 0 if rev else CHUNK - 1
    lane_lo = lax.broadcasted_iota(jnp.int32, (CHUNK, LANES), 1) < PC
    row_lo = lax.broadcasted_iota(jnp.int32, (LANES, NC), 0) < PC
    cbs = []
    for g in range(GC):
        bg = xbc[:, GROUP_W + g * NC:GROUP_W + (g + 1) * NC].astype(BF16)
        cg = xbc[:, GROUP_W + GC * NC + g * NC:GROUP_W + GC * NC + (g + 1) * NC].astype(BF16)
        cbs.append((bg, cg, _dot_nt(cg, bg)))
    outs = []
    for pr in range(SSD_PAIRS):
        ha, hb = 2 * pr, 2 * pr + 1
        bg, cg, cb = cbs[ha // (HC // GC)]
        xdt = xbc[:, pr * LANES:(pr + 1) * LANES] * jnp.where(lane_lo, dt_c[:, ha:ha + 1], dt_c[:, hb:hb + 1])
        ys = []
        for hx in (ha, hb):
            decay = jnp.exp(jnp.where(causal, cum_c[:, hx:hx + 1] - cum_r[hx:hx + 1, :], 0.0))
            m = (cb * jnp.where(causal, decay, 0.0)).astype(BF16)
            ys.append(jnp.dot(m, xdt.astype(BF16), preferred_element_type=F32))
        y = jnp.where(lane_lo, ys[0], ys[1])
        st = st_s[d, pr]
        e_cum = jnp.where(lane_lo, jnp.exp(cum_c[:, ha:ha + 1]), jnp.exp(cum_c[:, hb:hb + 1]))
        y = y + _dot_nt(cg, st.astype(BF16)) * e_cum
        outs.append(y)
        last_a, last_b = cum_c[end:end + 1, ha:ha + 1], cum_c[end:end + 1, hb:hb + 1]
        w = jnp.where(lane_lo, jnp.exp(last_a - cum_c[:, ha:ha + 1]), jnp.exp(last_b - cum_c[:, hb:hb + 1]))
        e_last = jnp.where(row_lo, jnp.exp(last_a), jnp.exp(last_b))
        st_s[d, pr] = e_last * st + _dot_tn((xdt * w).astype(BF16), bg)
    y_ref[...] = jnp.concatenate(outs, axis=-1)


def _ssd_body(xf_ref, xb_ref, dcf_ref, dcb_ref, drf_ref, drb_ref, bc_ref, ac_ref, br_ref, ar_ref, s0_ref,
              yf_ref, yb_ref, sfin_ref, st_s):
    i = pl.program_id(1)

    @pl.when(i == 0)
    def _():
        st_s[...] = s0_ref[...]

    _ssd_direction(False, xf_ref, dcf_ref, drf_ref, bc_ref[0], ac_ref[0], br_ref[0], ar_ref[0], st_s, 0, yf_ref)
    _ssd_direction(True, xb_ref, dcb_ref, drb_ref, bc_ref[1], ac_ref[1], br_ref[1], ar_ref[1], st_s, 1, yb_ref)

    @pl.when(i == pl.num_programs(1) - 1)
    def _():
        sfin_ref[...] = st_s[...]


def ssd_scan(xbc, dt_f, dt_b, dt_bias, a_log, s0):
    b, t, _ = xbc.shape
    n = t // CHUNK
    a = -jnp.exp(a_log.astype(F32))
    bias = dt_bias.astype(F32)
    dcf, dcb = dt_f.reshape(b, n, CHUNK, HC), dt_b.reshape(b, n, CHUNK, HC)
    drf, drb = dcf.transpose(0, 1, 3, 2), dcb.transpose(0, 1, 3, 2)
    x_f = pl.BlockSpec((None, CHUNK, CONV_CH), lambda bi, i: (bi, i, 0))
    x_b = pl.BlockSpec((None, CHUNK, CONV_CH), lambda bi, i: (bi, n - 1 - i, 0))
    y_f = pl.BlockSpec((None, CHUNK, GROUP_W), lambda bi, i: (bi, i, 0))
    y_b = pl.BlockSpec((None, CHUNK, GROUP_W), lambda bi, i: (bi, n - 1 - i, 0))
    c_f = pl.BlockSpec((None, None, CHUNK, HC), lambda bi, i: (bi, i, 0, 0))
    c_b = pl.BlockSpec((None, None, CHUNK, HC), lambda bi, i: (bi, n - 1 - i, 0, 0))
    r_f = pl.BlockSpec((None, None, HC, CHUNK), lambda bi, i: (bi, i, 0, 0))
    r_b = pl.BlockSpec((None, None, HC, CHUNK), lambda bi, i: (bi, n - 1 - i, 0, 0))
    p_c = pl.BlockSpec((2, 1, HC), lambda bi, i: (0, 0, 0))
    p_r = pl.BlockSpec((2, HC, 1), lambda bi, i: (0, 0, 0))
    st = pl.BlockSpec((None, 2, SSD_PAIRS, LANES, NC), lambda bi, i: (bi, 0, 0, 0, 0))
    y_fwd, y_bwd, s_fin = pl.pallas_call(
        _ssd_body,
        grid=(b, n),
        in_specs=[x_f, x_b, c_f, c_b, r_f, r_b, p_c, p_c, p_r, p_r, st],
        out_specs=[y_f, y_b, st],
        out_shape=[jax.ShapeDtypeStruct((b, t, GROUP_W), F32), jax.ShapeDtypeStruct((b, t, GROUP_W), F32),
                   jax.ShapeDtypeStruct((b, 2, SSD_PAIRS, LANES, NC), F32)],
        scratch_shapes=[pltpu.VMEM((2, SSD_PAIRS, LANES, NC), F32)],
        compiler_params=pltpu.CompilerParams(dimension_semantics=("parallel", "arbitrary"),
                                             vmem_limit_bytes=VMEM_LIMIT),
        name="ssd_scan",
    )(xbc, xbc, dcf, dcb, drf, drb, bias.reshape(2, 1, HC), a.reshape(2, 1, HC), bias.reshape(2, HC, 1),
      a.reshape(2, HC, 1), s0.astype(F32).reshape(b, 2, SSD_PAIRS, LANES, NC))
    return y_fwd, y_bwd, s_fin.reshape(b, 2, HC, PC, NC)


def _ssd_combine_body(x_ref, yf_ref, yb_ref, z_ref, d_ref, w_ref, o_ref):
    y = (d_ref[...] * x_ref[...] + yf_ref[...] + yb_ref[...]) * jax.nn.silu(z_ref[...])
    o_ref[...] = _rms_rows(y, w_ref[...])


def ssd_combine(xbc2, y_f, y_b, proj2, d_skip, norm_w):
    n = xbc2.shape[0]
    rows = 256
    blk = pl.BlockSpec((rows, GROUP_W), lambda i: (i, 0))
    par = pl.BlockSpec((1, GROUP_W), lambda i: (0, 0))
    return pl.pallas_call(
        _ssd_combine_body,
        grid=(n // rows,),
        in_specs=[blk, blk, blk, pl.BlockSpec((rows, GROUP_W), lambda i: (i, COL_SZ)), par, par],
        out_specs=blk,
        out_shape=jax.ShapeDtypeStruct((n, GROUP_W), F32),
        compiler_params=pltpu.CompilerParams(dimension_semantics=("parallel",)),
        name="ssd_combine",
    )(xbc2, y_f.reshape(n, GROUP_W), y_b.reshape(n, GROUP_W), proj2,
      jnp.repeat(d_skip.astype(F32), PC).reshape(1, GROUP_W), norm_w.astype(F32).reshape(1, GROUP_W))


def ssd_mixer(proj2, b, t, p, init_state):
    proj3 = proj2.reshape(b, t, PROJ_W)
    xbc = ssd_conv(proj3, p['ssd_conv_w'], p['ssd_conv_b'])
    dt0 = COL_SMALL * LANES + SMALL_DT
    dt_f, dt_b = proj3[:, :, dt0:dt0 + HC], proj3[:, :, dt0 + HC:dt0 + 2 * HC]
    s0 = jnp.zeros((b, 2, HC, PC, NC), F32) if init_state is None else init_state
    y_f, y_b, s_fin = ssd_scan(xbc, dt_f, dt_b, p['ssd_dt_bias'], p['ssd_a_log'], s0)
    return ssd_combine(xbc.reshape(b * t, CONV_CH), y_f, y_b, proj2, p['ssd_d'], p['ssd_norm_w']), s_fin


def _peer_finish_body(ot_ref, x_ref, g_ref, nw_ref, o_ref, *, final_norm):
    x = x_ref[...] + g_ref[...] * ot_ref[...].T
    o_ref[...] = _rms_rows(x, nw_ref[...]) if final_norm else x


def peer_finish(out_t, tok0, x, mod, rows_per_mod, final_norm_w=None):
    n, d = x.shape
    tm = 256
    norm_w = jnp.ones((1, d), F32) if final_norm_w is None else final_norm_w.astype(F32).reshape(1, d)
    return pl.pallas_call(
        functools.partial(_peer_finish_body, final_norm=final_norm_w is not None),
        grid=(n // tm,),
        in_specs=[pl.BlockSpec((d, tm), lambda i: (0, tok0 // tm + i)), pl.BlockSpec((tm, d), lambda i: (i, 0)),
                  pl.BlockSpec((None, 1, d), lambda i: ((i * tm) // rows_per_mod, 0, MOD_G2)),
                  pl.BlockSpec((1, d), lambda i: (0, 0))],
        out_specs=pl.BlockSpec((tm, d), lambda i: (i, 0)),
        out_shape=jax.ShapeDtypeStruct((n, d), F32),
        compiler_params=pltpu.CompilerParams(dimension_semantics=("parallel",), vmem_limit_bytes=VMEM_LIMIT),
        name="peer_finish",
    )(out_t, x, mod, norm_w)


def trunk_layer(x2, b, t, mod, p, cache):
    rows_per_mod = (b * t) // mod.shape[0]
    proj2 = in_proj(x2, p['norm1_w'], mod, p['w_in_b'], rows_per_mod)
    proj3 = proj2.reshape(b, t, PROJ_W)
    latent = cache is not None
    s0 = cache[0].astype(F32) if latent else jnp.zeros((b, 2, HA, DKA, DVA), F32)
    o_f, o_b, st_a = hgrn_scan(proj3, p['lb'], s0)
    o_a = hgrn_combine(o_f, o_b, proj2, p['hgrn_norm_w'])
    o_c, st_c = ssd_mixer(proj2, b, t, p, cache[1] if latent else None)
    na_q, na_k, na_v = (proj3, GROUP_W, COL_NQ), (proj3, GROUP_W, COL_NK), (proj3, GROUP_W, COL_NV)
    if latent:
        o_m, _, _ = mla_mixer(proj2, b, t, p, (cache[2], cache[3]))
        o_d = natten_latent(proj3, p['na_rpb'], cache[4], cache[5])
        new_state = None
    else:
        o_m, ckv, krope = mla_mixer(proj2, b, t, p, None)
        o_d = attention([na_q], [([na_k], na_v)], HD, ((DHD, False),), DHD, DHD ** -0.5)
        k_na = proj3[:, :, COL_NK * GROUP_W:(COL_NK + 1) * GROUP_W].reshape(b, t, HD, DHD)
        v_na = proj3[:, :, COL_NV * GROUP_W:(COL_NV + 1) * GROUP_W].reshape(b, t, HD, DHD)
        new_state = (st_a, st_c, ckv, krope, k_na, v_na)
    n = b * t
    x2, h2 = out_proj([o_a, o_m.reshape(n, GROUP_W), o_c, o_d.reshape(n, GROUP_W)], p['w_out_b'], x2, mod,
                      p['norm2_w'], rows_per_mod)
    return x2, h2, new_state


def kernel(x_prompt, x_sample, c, state_hgrn, state_ssd, cache_mla_ckv, cache_mla_krope, cache_na_k,
           cache_na_v, c_ctx, w_ada, b_ada, norm1_w, norm2_w, w_in, w_out, hgrn_lb_logits, hgrn_norm_w,
           mla_q_norm_w, mla_w_q_up, mla_kv_norm_w, mla_w_kv_up, ssd_conv_w, ssd_conv_b, ssd_dt_bias,
           ssd_a_log, ssd_d, ssd_norm_w, na_rpb, peer_w_q, peer_sub_keys, peer_u, peer_v, final_norm_w):
    lb_soft = jax.nn.softmax(hgrn_lb_logits.astype(F32), axis=0)
    lb_all = jnp.cumsum(lb_soft, axis=0) - lb_soft[0]
    stacked = {'w_ada': w_ada, 'b_ada': b_ada, 'norm1_w': norm1_w, 'norm2_w': norm2_w, 'w_in': w_in,
               'w_out': w_out, 'hgrn_norm_w': hgrn_norm_w, 'mla_q_norm_w': mla_q_norm_w,
               'mla_w_q_up': mla_w_q_up, 'mla_kv_norm_w': mla_kv_norm_w, 'mla_w_kv_up': mla_w_kv_up,
               'ssd_conv_w': ssd_conv_w, 'ssd_conv_b': ssd_conv_b, 'ssd_dt_bias': ssd_dt_bias,
               'ssd_a_log': ssd_a_log, 'ssd_d': ssd_d, 'ssd_norm_w': ssd_norm_w, 'na_rpb': na_rpb,
               'peer_w_q': peer_w_q, 'peer_sub_keys': peer_sub_keys, 'peer_u': peer_u, 'peer_v': peer_v}

    ub_all, vt_all = peer_prep_tables(peer_u, peer_v)
    n_ctx, n_lat = BATCH * SEQ, DEC_BATCH * DEC_SEQ
    cond = jnp.concatenate([c_ctx[None, :], c, jnp.zeros((8 - 1 - DEC_BATCH, D_MODEL), F32)], axis=0)

    xp, xs = x_prompt.reshape(n_ctx, D_MODEL), x_sample.reshape(n_lat, D_MODEL)
    ctx_states = []
    for l in range(DEPTH):
        p = {name: arr[l] for name, arr in stacked.items()}
        p['lb'] = lb_all[l]
        p['w_in_b'] = in_proj_weight(w_in[l])
        p['w_out_b'] = w_out[l].astype(BF16)
        p['mla_wq_b'], p['mla_wkv_b'] = mla_weights(mla_w_q_up[l], mla_w_kv_up[l])
        mod = adaln(cond, w_ada[l], b_ada[l])
        mod_p, mod_s = mod[0:1, None, :], mod[1:1 + DEC_BATCH, None, :]
        xp, h2p, st = trunk_layer(xp, BATCH, SEQ, mod_p, p, None)
        ctx_states.append(st)
        cache_l = (state_hgrn[:, l], state_ssd[:, l], cache_mla_ckv[:, l], cache_mla_krope[:, l],
                   cache_na_k[:, l], cache_na_v[:, l])
        xs, h2s, _ = trunk_layer(xs, DEC_BATCH, DEC_SEQ, mod_s, p, cache_l)
        out_t = peer_ffn(h2p, h2s, peer_w_q[l].T.astype(BF16), peer_sub_keys[l].astype(BF16), ub_all[l], vt_all[l])
        last_w = final_norm_w if l == DEPTH - 1 else None
        xp = peer_finish(out_t, 0, xp, mod_p, n_ctx, last_w)
        xs = peer_finish(out_t, n_ctx, xs, mod_s, DEC_SEQ, last_w)

    y_prompt = xp.reshape(BATCH, SEQ, D_MODEL)
    y_sample = xs.reshape(DEC_BATCH, DEC_SEQ, D_MODEL)
    new_state_hgrn = jnp.stack([s[0] for s in ctx_states], axis=1)
    new_state_ssd = jnp.stack([s[1] for s in ctx_states], axis=1)
    new_cache_mla_ckv = jnp.stack([s[2] for s in ctx_states], axis=1)
    new_cache_mla_krope = jnp.stack([s[3] for s in ctx_states], axis=1)
    new_cache_na_k = jnp.stack([s[4] for s in ctx_states], axis=1)
    new_cache_na_v = jnp.stack([s[5] for s in ctx_states], axis=1)
    return (y_prompt, y_sample, new_state_hgrn, new_state_ssd, new_cache_mla_ckv, new_cache_mla_krope,
            new_cache_na_k, new_cache_na_v)
```
